```python
import jax
import jax.numpy as jnp
from jax import lax
import numpy as np

D_MODEL = 1024
BATCH = 16
SEQ = 4096
DEPTH = 4
DEC_BATCH = 4
DEC_SEQ = 4096
PAST_LEN = 128

GRID_W = 64
HEAD_DIM = 64
EPS = 1e-6

ATTN_HEADS = 8
ATTN_KV_HEADS = 2
Q_BLOCK = 128
ROPE_THETA = 10000.0

GLA_HEADS = 4
GLA_RANK = 16
GLA_TAU = 16.0
GLA_CHUNK = 64

MLSTM_HEADS = 4
MLSTM_CHUNK = 64

ATTN_W = ATTN_HEADS * HEAD_DIM
KV_W = ATTN_KV_HEADS * HEAD_DIM
GLA_W = GLA_HEADS * HEAD_DIM
MLSTM_W = MLSTM_HEADS * HEAD_DIM
MIX_W = ATTN_W + GLA_W + MLSTM_W

D_FF = 2816
CONV_WIDTH = 3

IN_SPLITS = (ATTN_W, KV_W, KV_W,
             GLA_W, GLA_W, GLA_W, GLA_W, 2 * GLA_RANK,
             MLSTM_W, MLSTM_W, MLSTM_W, MLSTM_W, 4 * MLSTM_HEADS)
IN_DIM = sum(IN_SPLITS)

kernel_name = "hybrid_bidir_encoder_heads"


def rms_norm(x, g):
    xf = x.astype(jnp.float32)
    y = xf * lax.rsqrt(jnp.mean(xf * xf, axis=-1, keepdims=True) + EPS)
    return (y * g.astype(jnp.float32)).astype(x.dtype)


def axial_rope_tables(seq):
    n_rows = seq // GRID_W
    row = jnp.repeat(jnp.arange(n_rows, dtype=jnp.float32), GRID_W)
    col = jnp.tile(jnp.arange(GRID_W, dtype=jnp.float32), n_rows)
    n_freq = HEAD_DIM // 4
    inv_freq = jnp.power(ROPE_THETA, -jnp.arange(n_freq, dtype=jnp.float32) / n_freq)
    ang = jnp.stack([row[:, None] * inv_freq, col[:, None] * inv_freq], axis=1)
    return jnp.cos(ang), jnp.sin(ang)


def apply_axial_rope(x, cos, sin):
    b, s, h, d = x.shape
    xr = x.astype(jnp.float32).reshape(b, s, h, 2, 2, d // 4)
    x1, x2 = xr[..., 0, :], xr[..., 1, :]
    c = cos[None, :, None]
    sn = sin[None, :, None]
    out = jnp.stack([x1 * c - x2 * sn, x1 * sn + x2 * c], axis=-2)
    return out.reshape(b, s, h, d).astype(x.dtype)


def axial_gqa_attention(q, k, v, qn_g, kn_g, cos, sin):
    b, s = q.shape[:2]
    grp = ATTN_HEADS // ATTN_KV_HEADS
    q = apply_axial_rope(rms_norm(q, qn_g), cos, sin)
    k = apply_axial_rope(rms_norm(k, kn_g), cos, sin)
    n_blk = s // Q_BLOCK
    qb = jnp.moveaxis(q.reshape(b, n_blk, Q_BLOCK, ATTN_KV_HEADS, grp, HEAD_DIM), 1, 0)
    scale = HEAD_DIM ** -0.5

    def block(qi):
        sc = jnp.einsum('bqhgd,bkhd->bhgqk', qi, k, preferred_element_type=jnp.float32) * scale
        p = jax.nn.softmax(sc, axis=-1).astype(v.dtype)
        return jnp.einsum('bhgqk,bkhd->bqhgd', p, v)

    o = lax.map(block, qb)
    return jnp.moveaxis(o, 0, 1).reshape(b, s, ATTN_W)


def gla_direction(q, k, v, log_a):
    b, h, s, d = q.shape
    n_c = s // GLA_CHUNK
    to_chunks = lambda t: jnp.moveaxis(t.reshape(b, h, n_c, GLA_CHUNK, d), 2, 0)
    mask = jnp.tril(jnp.ones((GLA_CHUNK, GLA_CHUNK), dtype=bool))

    def step(state, inp):
        qc, kc, vc, gc = inp
        cum = jnp.cumsum(gc, axis=2)
        o_inter = jnp.einsum('bhtd,bhde->bhte', qc * jnp.exp(cum), state)
        diff = cum[:, :, :, None, :] - cum[:, :, None, :, :]
        decay = jnp.exp(jnp.where(mask[:, :, None], diff, -jnp.inf))
        att = jnp.einsum('bhtd,bhsd,bhtsd->bhts', qc, kc, decay)
        o_intra = jnp.einsum('bhts,bhse->bhte', att, vc)
        last = cum[:, :, -1:, :]
        new_state = (jnp.exp(last[:, :, 0, :])[..., None] * state
                     + jnp.einsum('bhsd,bhse->bhde', kc * jnp.exp(last - cum), vc))
        return new_state, o_inter + o_intra

    init = jnp.zeros((b, h, d, d), jnp.float32)
    _, o = lax.scan(step, init, (to_chunks(q), to_chunks(k), to_chunks(v), to_chunks(log_a)))
    return jnp.moveaxis(o, 0, 2).reshape(b, h, s, d)


def gla_mixer(q, k, v, g, a_lr, w2, a_b, norm_g):
    b, s, _ = q.shape
    f32 = jnp.float32
    heads = lambda t: t.astype(f32).reshape(b, s, GLA_HEADS, HEAD_DIM).transpose(0, 2, 1, 3)
    qh = heads(q) * HEAD_DIM ** -0.5
    kh = heads(k)
    vh = heads(v)
    a_lr = a_lr.astype(f32).reshape(b, s, 2, GLA_RANK)
    z = jnp.einsum('bsnr,nrk->nbsk', a_lr, w2.astype(f32)) + a_b.astype(f32)[:, None, None, :]
    log_a = jax.nn.log_sigmoid(z) / GLA_TAU
    flip = lambda t: t[:, :, ::-1]
    o_f = gla_direction(qh, kh, vh, heads(log_a[0]))
    o_b = flip(gla_direction(flip(qh), flip(kh), flip(vh), flip(heads(log_a[1]))))
    o = (o_f + o_b).transpose(0, 2, 1, 3)
    gate = jax.nn.silu(g.astype(f32).reshape(b, s, GLA_HEADS, HEAD_DIM))
    return (rms_norm(o, norm_g) * gate).reshape(b, s, GLA_W)


def mlstm_direction(q, k, v, i_pre, logf):
    b, h, s, d = q.shape
    L = MLSTM_CHUNK
    n_c = s // L
    chunk = lambda t: jnp.moveaxis(t.reshape(b, h, n_c, L, *t.shape[3:]), 2, 0)
    mask = jnp.tril(jnp.ones((L, L), dtype=bool))

    def step(carry, inp):
        c_st, n_st, m_st = carry
        qc, kc, vc, ic, fc = inp
        fcum = jnp.cumsum(fc, axis=-1)
        log_d = jnp.where(mask, fcum[..., :, None] - fcum[..., None, :] + ic[..., None, :], -jnp.inf)
        inter = fcum + m_st[..., None]
        m_t = jnp.maximum(inter, jnp.max(log_d, axis=-1))
        w_inter = jnp.exp(inter - m_t)
        qk = jnp.einsum('bhtd,bhsd->bhts', qc, kc) * jnp.exp(log_d - m_t[..., None])
        num = (w_inter[..., None] * jnp.einsum('bhed,bhtd->bhte', c_st, qc)
               + jnp.einsum('bhts,bhse->bhte', qk, vc))
        den = w_inter * jnp.einsum('bhd,bhtd->bht', n_st, qc) + jnp.sum(qk, axis=-1)
        hc = num / jnp.maximum(jnp.abs(den), jnp.exp(-m_t))[..., None]
        f_tot = fcum[..., -1]
        log_w = f_tot[..., None] - fcum + ic
        m_new = jnp.maximum(f_tot + m_st, jnp.max(log_w, axis=-1))
        decay = jnp.exp(f_tot + m_st - m_new)
        wk = kc * jnp.exp(log_w - m_new[..., None])[..., None]
        c_new = decay[..., None, None] * c_st + jnp.einsum('bhse,bhsd->bhed', vc, wk)
        n_new = decay[..., None] * n_st + jnp.sum(wk, axis=2)
        return (c_new, n_new, m_new), hc

    init = (jnp.zeros((b, h, d, d), jnp.float32), jnp.zeros((b, h, d), jnp.float32),
            jnp.zeros((b, h), jnp.float32))
    _, hs = lax.scan(step, init, (chunk(q), chunk(k), chunk(v), chunk(i_pre), chunk(logf)))
    return jnp.moveaxis(hs, 0, 2).reshape(b, h, s, d)


def mlstm_mixer(q, k, v, o_pre, gates, gate_b, norm_g):
    b, s, _ = q.shape
    f32 = jnp.float32
    heads = lambda t: t.astype(f32).reshape(b, s, MLSTM_HEADS, HEAD_DIM).transpose(0, 2, 1, 3)
    qh = heads(q)
    kh = heads(k) * HEAD_DIM ** -0.5
    vh = heads(v)
    gp = (gates.astype(f32).reshape(b, s, 4, MLSTM_HEADS) + gate_b.astype(f32)).transpose(2, 0, 3, 1)
    lf_f = jax.nn.log_sigmoid(gp[2])
    lf_b = jax.nn.log_sigmoid(gp[3])
    flip = lambda t: t[:, :, ::-1]
    h_f = mlstm_direction(qh, kh, vh, gp[0], lf_f)
    h_b = flip(mlstm_direction(flip(qh), flip(kh), flip(vh), flip(gp[1]), flip(lf_b)))
    h = jax.nn.sigmoid(heads(o_pre)) * (h_f + h_b)
    return rms_norm(h.transpose(0, 2, 1, 3), norm_g).reshape(b, s, MLSTM_W)


def centred_dwconv(u, w, bias):
    s = u.shape[1]
    pad = CONV_WIDTH // 2
    up = jnp.pad(u, ((0, 0), (pad, CONV_WIDTH - 1 - pad), (0, 0)))
    out = up[:, 0:s] * w[0]
    for j in range(1, CONV_WIDTH):
        out = out + up[:, j:j + s] * w[j]
    return out + bias


def encoder_layer(x, cos, sin, norm_mix_g, w_in, attn_qn_g, attn_kn_g, gla_gate_w2, gla_gate_b,
                  gla_norm_g, mlstm_gate_b, mlstm_norm_g, w_out, norm_ffn_g, w_up, conv_w, conv_b,
                  w_down):
    b, s, _ = x.shape
    h = rms_norm(x, norm_mix_g)
    proj = h @ w_in
    offsets = np.cumsum(IN_SPLITS)[:-1].tolist()
    aq, ak, av, bq, bk, bv, bg, ba, cq, ck, cv, co, cg = jnp.split(proj, offsets, axis=-1)
    attn = axial_gqa_attention(aq.reshape(b, s, ATTN_HEADS, HEAD_DIM),
                               ak.reshape(b, s, ATTN_KV_HEADS, HEAD_DIM),
                               av.reshape(b, s, ATTN_KV_HEADS, HEAD_DIM),
                               attn_qn_g, attn_kn_g, cos, sin)
    gla = gla_mixer(bq, bk, bv, bg, ba, gla_gate_w2, gla_gate_b, gla_norm_g)
    mls = mlstm_mixer(cq, ck, cv, co, cg, mlstm_gate_b, mlstm_norm_g)
    mixed = jnp.concatenate([attn.astype(x.dtype), gla.astype(x.dtype), mls.astype(x.dtype)], axis=-1)
    x = x + mixed @ w_out
    u = centred_dwconv(rms_norm(x, norm_ffn_g) @ w_up, conv_w, conv_b)
    gate, val = jnp.split(u, 2, axis=-1)
    return x + (jax.nn.silu(gate) * val) @ w_down


def setup_inputs(seed: int = 0) -> dict:
    key = jax.random.key(seed)
    ks = jax.random.split(key, 20)
    f32 = jnp.float32
    nrm = lambda k, shape, scale: scale * jax.random.normal(k, shape, f32)
    gain = lambda k, shape: 1.0 + 0.02 * jax.random.normal(k, shape, f32)
    f_bias = jnp.linspace(3.0, 6.0, MLSTM_HEADS, dtype=f32)
    mlstm_gate_b = jnp.concatenate([nrm(ks[9], (DEPTH, 2, MLSTM_HEADS), 0.1),
                                    f_bias + nrm(ks[10], (DEPTH, 2, MLSTM_HEADS), 0.1)], axis=1)
    return {
        'x_prompt': jax.random.normal(ks[0], (BATCH, SEQ, D_MODEL), f32),
        'x_sample': jax.random.normal(ks[1], (DEC_BATCH, DEC_SEQ, D_MODEL), f32),
        'norm_mix_g': gain(ks[2], (DEPTH, D_MODEL)),
        'w_in': nrm(ks[3], (DEPTH, D_MODEL, IN_DIM), D_MODEL ** -0.5),
        'attn_qn_g': gain(ks[4], (DEPTH, HEAD_DIM)),
        'attn_kn_g': gain(ks[5], (DEPTH, HEAD_DIM)),
        'gla_gate_w2': nrm(ks[6], (DEPTH, 2, GLA_RANK, GLA_W), GLA_RANK ** -0.5),
        'gla_gate_b': nrm(ks[7], (DEPTH, 2, GLA_W), 0.1),
        'gla_norm_g': gain(ks[8], (DEPTH, HEAD_DIM)),
        'mlstm_gate_b': mlstm_gate_b,
        'mlstm_norm_g': gain(ks[11], (DEPTH, HEAD_DIM)),
        'w_out': nrm(ks[12], (DEPTH, MIX_W, D_MODEL), MIX_W ** -0.5),
        'norm_ffn_g': gain(ks[13], (DEPTH, D_MODEL)),
        'w_up': nrm(ks[14], (DEPTH, D_MODEL, 2 * D_FF), D_MODEL ** -0.5),
        'conv_w': nrm(ks[15], (DEPTH, CONV_WIDTH, 2 * D_FF), CONV_WIDTH ** -0.5),
        'conv_b': nrm(ks[16], (DEPTH, 2 * D_FF), 0.02),
        'w_down': nrm(ks[17], (DEPTH, D_FF, D_MODEL), D_FF ** -0.5),
    }


def reference(x_prompt, x_sample, norm_mix_g, w_in, attn_qn_g, attn_kn_g, gla_gate_w2, gla_gate_b,
              gla_norm_g, mlstm_gate_b, mlstm_norm_g, w_out, norm_ffn_g, w_up, conv_w, conv_b,
              w_down):
    def run(x):
        cos, sin = axial_rope_tables(x.shape[1])
        for l in range(DEPTH):
            x = encoder_layer(x, cos, sin, norm_mix_g[l], w_in[l], attn_qn_g[l], attn_kn_g[l],
                              gla_gate_w2[l], gla_gate_b[l], gla_norm_g[l], mlstm_gate_b[l],
                              mlstm_norm_g[l], w_out[l], norm_ffn_g[l], w_up[l], conv_w[l],
                              conv_b[l], w_down[l])
        return x

    y_prompt = run(x_prompt)
    y_sample = run(x_sample)
    return (y_prompt, y_sample)
```

```python
import functools

import jax
import jax.numpy as jnp
import numpy as np
from jax import lax
from jax.experimental import pallas as pl
from jax.experimental.pallas import tpu as pltpu

F32 = jnp.float32
BF16 = jnp.bfloat16

D_MODEL = 1024
GRID_W = 64
HEAD_DIM = 64
EPS = 1e-6
ATTN_HEADS = 8
ATTN_KV_HEADS = 2
ATTN_GROUP = ATTN_HEADS // ATTN_KV_HEADS
ROPE_THETA = 10000.0
GLA_HEADS = 4
GLA_RANK = 16
GLA_TAU = 16.0
MLSTM_HEADS = 4
CHUNK = 64
ATTN_W = ATTN_HEADS * HEAD_DIM
KV_W = ATTN_KV_HEADS * HEAD_DIM
GLA_W = GLA_HEADS * HEAD_DIM
MLSTM_W = MLSTM_HEADS * HEAD_DIM
QK_W = ATTN_W + KV_W
QKV_W = ATTN_W + 2 * KV_W
D_FF = 2816
CONV_WIDTH = 3
GATE_W = 128
Q_SCALE = HEAD_DIM ** -0.5

LANE = 128
SUBLANE = 8
VMEM_LIMIT = 56 * 1024 * 1024

ROW_TILE = 512
REC_BLOCK = 256
ATTN_TQ = 128
ATTN_TK = 512
FF_BLOCK = 1408
HALO = SUBLANE

NT_DIMS = (((1,), (1,)), ((), ()))
TN_DIMS = (((0,), (0,)), ((), ()))


def _params(*sem):
    return pltpu.CompilerParams(dimension_semantics=sem, vmem_limit_bytes=VMEM_LIMIT)


def _log_sigmoid(z):
    return jnp.minimum(z, 0.0) - jnp.log(1.0 + jnp.exp(-jnp.abs(z)))


def _sigmoid(z):
    return 1.0 / (1.0 + jnp.exp(-z))


def _dot(a, b):
    return jnp.dot(a.astype(BF16), b.astype(BF16), preferred_element_type=F32)


def _dot_dims(a, b, dims):
    return lax.dot_general(a.astype(BF16), b.astype(BF16), dims, preferred_element_type=F32)


def _dot_f32(a, b):
    return jnp.dot(a, b, preferred_element_type=F32, precision=lax.Precision.HIGHEST)


def _group_mean(sq, bd):
    hi = sq.astype(BF16)
    lo = (sq - hi.astype(F32)).astype(BF16)
    return (jnp.dot(hi, bd, preferred_element_type=F32)
            + jnp.dot(lo, bd, preferred_element_type=F32))


def _inproj_kernel(x_ref, g_ref, w_ref, qkv_ref, gla_ref, mls_ref, gate_ref):
    x = x_ref[0]
    ms = jnp.mean(x * x, axis=-1, keepdims=True)
    h = (x * lax.rsqrt(ms + EPS) * g_ref[...]).astype(BF16)
    col = 0
    for ref in (qkv_ref, gla_ref, mls_ref, gate_ref):
        width = ref.shape[-1]
        ref[0] = jnp.dot(h, w_ref[:, col:col + width], preferred_element_type=F32)
        col += width


def _inproj(x, g, w):
    b, s, d = x.shape
    widths = (QKV_W, 4 * GLA_W, 4 * MLSTM_W, GATE_W)
    row = lambda n: pl.BlockSpec((1, ROW_TILE, n), lambda bi, i: (bi, i, 0))
    const = lambda shape: pl.BlockSpec(shape, lambda bi, i: (0,) * len(shape))
    return pl.pallas_call(
        _inproj_kernel,
        grid=(b, s // ROW_TILE),
        in_specs=[row(d), const((1, d)), const(w.shape)],
        out_specs=[row(n) for n in widths],
        out_shape=[jax.ShapeDtypeStruct((b, s, n), F32) for n in widths],
        compiler_params=_params("parallel", "parallel"),
        name="inproj",
    )(x, g, w)


def _qkprep_kernel(qkv_ref, cos_ref, sin_ref, gain_ref, bd_ref, q_ref, k_ref, v_ref):
    x = qkv_ref[0]
    qk = x[:, :QK_W]
    ms = _group_mean(qk * qk, bd_ref[...])
    y = qk * lax.rsqrt(ms + EPS) * gain_ref[...]
    lane = lax.broadcasted_iota(jnp.int32, y.shape, 1)
    first_half = (lane & 31) < 16
    quarter = HEAD_DIM // 4
    swapped = jnp.where(first_half, pltpu.roll(y, QK_W - quarter, 1), pltpu.roll(y, quarter, 1))
    r = y * cos_ref[...] + swapped * sin_ref[...]
    q_ref[0] = (r[:, :ATTN_W] * Q_SCALE).astype(BF16)
    for h in range(ATTN_KV_HEADS):
        k_ref[0, h] = r[:, ATTN_W + h * HEAD_DIM:ATTN_W + (h + 1) * HEAD_DIM].astype(BF16)
        v_ref[0, h] = x[:, QK_W + h * HEAD_DIM:QK_W + (h + 1) * HEAD_DIM].astype(BF16)


def _qkprep(qkv, cos_t, sin_t, gain, bd):
    b, s, _ = qkv.shape
    const = lambda shape: pl.BlockSpec(shape, lambda bi, i: (0,) * len(shape))
    tab = pl.BlockSpec((ROW_TILE, QK_W), lambda bi, i: (i, 0))
    kv_spec = pl.BlockSpec((1, ATTN_KV_HEADS, ROW_TILE, HEAD_DIM), lambda bi, i: (bi, 0, i, 0))
    kv_shape = jax.ShapeDtypeStruct((b, ATTN_KV_HEADS, s, HEAD_DIM), BF16)
    return pl.pallas_call(
        _qkprep_kernel,
        grid=(b, s // ROW_TILE),
        in_specs=[pl.BlockSpec((1, ROW_TILE, QKV_W), lambda bi, i: (bi, i, 0)),
                  tab, tab, const((1, QK_W)), const((QK_W, QK_W))],
        out_specs=[pl.BlockSpec((1, ROW_TILE, ATTN_W), lambda bi, i: (bi, i, 0)), kv_spec, kv_spec],
        out_shape=[jax.ShapeDtypeStruct((b, s, ATTN_W), BF16), kv_shape, kv_shape],
        compiler_params=_params("parallel", "parallel"),
        name="qkprep",
    )(qkv, cos_t, sin_t, gain, bd)


def _attn_kernel(q_ref, k_ref, v_ref, o_ref, qs_ref, m_ref, l_ref, acc_ref):
    tq = q_ref.shape[1]
    n_keys = k_ref.shape[2]
    for j in range(ATTN_GROUP):
        qs_ref[j * tq:(j + 1) * tq, :] = q_ref[0, :, j * HEAD_DIM:(j + 1) * HEAD_DIM]
    m_ref[...] = jnp.full(m_ref.shape, -jnp.inf, F32)
    l_ref[...] = jnp.zeros(l_ref.shape, F32)
    acc_ref[...] = jnp.zeros(acc_ref.shape, F32)
    qs = qs_ref[...]

    def step(c, carry):
        start = pl.multiple_of(c * ATTN_TK, ATTN_TK)
        k = k_ref[0, 0, pl.ds(start, ATTN_TK), :]
        v = v_ref[0, 0, pl.ds(start, ATTN_TK), :]
        s = lax.dot_general(qs, k, NT_DIMS, preferred_element_type=F32)
        m_old = m_ref[...]
        m_new = jnp.maximum(m_old, jnp.max(s, axis=-1, keepdims=True))
        p = jnp.exp(s - m_new)
        alpha = jnp.exp(m_old - m_new)
        l_ref[...] = alpha * l_ref[...] + jnp.sum(p, axis=-1, keepdims=True)
        acc_ref[...] = alpha * acc_ref[...] + jnp.dot(p.astype(BF16), v, preferred_element_type=F32)
        m_ref[...] = m_new
        return carry

    lax.fori_loop(0, n_keys // ATTN_TK, step, 0)
    out = acc_ref[...] / l_ref[...]
    for j in range(ATTN_GROUP):
        o_ref[0, :, j * HEAD_DIM:(j + 1) * HEAD_DIM] = out[j * tq:(j + 1) * tq, :].astype(o_ref.dtype)


def _attention(q, k, v):
    b, s, _ = q.shape
    gw = ATTN_GROUP * HEAD_DIM
    rows = ATTN_GROUP * ATTN_TQ
    q_spec = pl.BlockSpec((1, ATTN_TQ, gw), lambda bi, h, i: (bi, i, h))
    kv_spec = pl.BlockSpec((1, 1, s, HEAD_DIM), lambda bi, h, i: (bi, h, 0, 0))
    return pl.pallas_call(
        _attn_kernel,
        grid=(b, ATTN_KV_HEADS, s // ATTN_TQ),
        in_specs=[q_spec, kv_spec, kv_spec],
        out_specs=q_spec,
        out_shape=jax.ShapeDtypeStruct((b, s, ATTN_W), BF16),
        scratch_shapes=[pltpu.VMEM((rows, HEAD_DIM), BF16),
                        pltpu.VMEM((rows, 1), F32),
                        pltpu.VMEM((rows, 1), F32),
                        pltpu.VMEM((rows, HEAD_DIM), F32)],
        compiler_params=_params("parallel", "parallel", "arbitrary"),
        name="attention",
    )(q, k, v)


def _chunk_masks():
    row = lax.broadcasted_iota(jnp.int32, (CHUNK, CHUNK), 0)
    col = lax.broadcasted_iota(jnp.int32, (CHUNK, CHUNK), 1)
    return col <= row, col >= row


def _gla_chunk(q, k, v, a, w2, gb, tri, last, mask, st_ref, d):
    z = _dot_f32(a, w2) + gb
    g = _log_sigmoid(z) * (1.0 / GLA_TAU)
    cum = _dot_f32(tri, g)
    total = cum[last:last + 1]
    ref = cum[CHUNK // 2:CHUNK // 2 + 1]
    qt = q * Q_SCALE * jnp.exp(cum - ref)
    kt = k * jnp.exp(ref - cum)
    qe = qt * jnp.exp(ref)
    kd = kt * jnp.exp(total - ref)
    dec = jnp.exp(total)
    outs = []
    for h in range(GLA_HEADS):
        sl = slice(h * HEAD_DIM, (h + 1) * HEAD_DIM)
        att = jnp.where(mask, _dot_dims(qt[:, sl], kt[:, sl], NT_DIMS), 0.0)
        st = st_ref[d, h]
        outs.append(_dot_dims(qe[:, sl], st, NT_DIMS) + _dot(att, v[:, sl]))
        st_ref[d, h] = st * dec[:, sl] + _dot_dims(v[:, sl], kd[:, sl], TN_DIMS)
    return jnp.concatenate(outs, axis=1)


def _gla_kernel(qf_ref, kf_ref, vf_ref, af_ref, qb_ref, kb_ref, vb_ref, ab_ref,
                w2_ref, gb_ref, tri_ref, of_ref, ob_ref, st_ref):
    @pl.when(pl.program_id(1) == 0)
    def _():
        st_ref[...] = jnp.zeros(st_ref.shape, F32)

    mask_f, mask_b = _chunk_masks()
    n_chunks = REC_BLOCK // CHUNK
    for i in range(n_chunks):
        rf = slice(i * CHUNK, (i + 1) * CHUNK)
        of_ref[0, rf, :] = _gla_chunk(qf_ref[0, rf, :], kf_ref[0, rf, :], vf_ref[0, rf, :],
                                      af_ref[0, rf, :], w2_ref[0], gb_ref[0], tri_ref[0],
                                      CHUNK - 1, mask_f, st_ref, 0)
        rb = slice((n_chunks - 1 - i) * CHUNK, (n_chunks - i) * CHUNK)
        ob_ref[0, rb, :] = _gla_chunk(qb_ref[0, rb, :], kb_ref[0, rb, :], vb_ref[0, rb, :],
                                      ab_ref[0, rb, :], w2_ref[1], gb_ref[1], tri_ref[1],
                                      0, mask_b, st_ref, 1)


def _gla(gla_proj, gate_proj, w2ext, gb, tri):
    b, s, _ = gla_proj.shape
    nb = s // REC_BLOCK
    fwd = lambda col: pl.BlockSpec((1, REC_BLOCK, GLA_W), lambda bi, j: (bi, j, col))
    bwd = lambda col: pl.BlockSpec((1, REC_BLOCK, GLA_W), lambda bi, j: (bi, nb - 1 - j, col))
    const = lambda shape: pl.BlockSpec(shape, lambda bi, j: (0,) * len(shape))
    gate_f = pl.BlockSpec((1, REC_BLOCK, GATE_W), lambda bi, j: (bi, j, 0))
    gate_b = pl.BlockSpec((1, REC_BLOCK, GATE_W), lambda bi, j: (bi, nb - 1 - j, 0))
    out_shape = jax.ShapeDtypeStruct((b, s, GLA_W), F32)
    return pl.pallas_call(
        _gla_kernel,
        grid=(b, nb),
        in_specs=[fwd(0), fwd(1), fwd(2), gate_f, bwd(0), bwd(1), bwd(2), gate_b,
                  const(w2ext.shape), const(gb.shape), const(tri.shape)],
        out_specs=[pl.BlockSpec((1, REC_BLOCK, GLA_W), lambda bi, j: (bi, j, 0)),
                   pl.BlockSpec((1, REC_BLOCK, GLA_W), lambda bi, j: (bi, nb - 1 - j, 0))],
        out_shape=[out_shape, out_shape],
        scratch_shapes=[pltpu.VMEM((2, GLA_HEADS, HEAD_DIM, HEAD_DIM), F32)],
        compiler_params=_params("parallel", "arbitrary"),
        name="gla",
    )(gla_proj, gla_proj, gla_proj, gate_proj, gla_proj, gla_proj, gla_proj, gate_proj,
      w2ext, gb, tri)


def _mlstm_chunk(q, k, v, gcol, grow, tri, tri_t, last, mask, ct_ref, n_ref, m_ref, d):
    fcum_col = _dot_f32(tri, _log_sigmoid(gcol))
    fcum_row = _dot_f32(_log_sigmoid(grow), tri_t)
    outs = []
    for h in range(MLSTM_HEADS):
        sl = slice(h * HEAD_DIM, (h + 1) * HEAD_DIM)
        ci = d * MLSTM_HEADS + h
        cf = 2 * MLSTM_HEADS + ci
        idx = d * MLSTM_HEADS + h
        fc, fr = fcum_col[:, cf:cf + 1], fcum_row[cf:cf + 1, :]
        ic, ir = gcol[:, ci:ci + 1], grow[ci:ci + 1, :]
        m_st = m_ref[idx][:, 0:1]
        qh, kh, vh = q[:, sl], k[:, sl] * Q_SCALE, v[:, sl]
        log_d = jnp.where(mask, fc - fr + ir, -jnp.inf)
        inter = fc + m_st
        m_t = jnp.maximum(inter, jnp.max(log_d, axis=-1, keepdims=True))
        w_inter = jnp.exp(inter - m_t)
        qk = _dot_dims(qh, kh, NT_DIMS) * jnp.exp(log_d - m_t)
        ct = ct_ref[idx]
        n_st = n_ref[idx]
        num = w_inter * _dot(qh, ct) + _dot(qk, vh)
        den = (w_inter * jnp.sum(qh * n_st, axis=-1, keepdims=True)
               + jnp.sum(qk, axis=-1, keepdims=True))
        outs.append(num / jnp.maximum(jnp.abs(den), jnp.exp(-m_t)))
        f_tot = fc[last:last + 1]
        m_new = jnp.maximum(f_tot + m_st, jnp.max(f_tot - fr + ir, axis=-1, keepdims=True))
        decay = jnp.exp(f_tot + m_st - m_new)
        wk = kh * jnp.exp(f_tot - fc + ic - m_new)
        ct_ref[idx] = decay * ct + _dot_dims(wk, vh, TN_DIMS)
        n_ref[idx] = decay * n_st + jnp.sum(wk, axis=0, keepdims=True)
        m_ref[idx] = jnp.broadcast_to(m_new, m_ref.shape[1:])
    return jnp.concatenate(outs, axis=1)


def _mlstm_kernel(qf_ref, kf_ref, vf_ref, gcf_ref, grf_ref, qb_ref, kb_ref, vb_ref, gcb_ref, grb_ref,
                  bcol_ref, brow_ref, tri_ref, trit_ref, hf_ref, hb_ref, ct_ref, n_ref, m_ref):
    @pl.when(pl.program_id(1) == 0)
    def _():
        ct_ref[...] = jnp.zeros(ct_ref.shape, F32)
        n_ref[...] = jnp.zeros(n_ref.shape, F32)
        m_ref[...] = jnp.zeros(m_ref.shape, F32)

    mask_f, mask_b = _chunk_masks()
    n_chunks = REC_BLOCK // CHUNK
    n_gates = 4 * MLSTM_HEADS
    for i in range(n_chunks):
        rf = slice(i * CHUNK, (i + 1) * CHUNK)
        gcol = gcf_ref[0, rf, 2 * GLA_RANK:2 * GLA_RANK + n_gates] + bcol_ref[...]
        grow = grf_ref[0, i] + brow_ref[...]
        hf_ref[0, rf, :] = _mlstm_chunk(qf_ref[0, rf, :], kf_ref[0, rf, :], vf_ref[0, rf, :],
                                        gcol, grow, tri_ref[0], trit_ref[0], CHUNK - 1, mask_f,
                                        ct_ref, n_ref, m_ref, 0)
        ib = n_chunks - 1 - i
        rb = slice(ib * CHUNK, (ib + 1) * CHUNK)
        gcol = gcb_ref[0, rb, 2 * GLA_RANK:2 * GLA_RANK + n_gates] + bcol_ref[...]
        grow = grb_ref[0, ib] + brow_ref[...]
        hb_ref[0, rb, :] = _mlstm_chunk(qb_ref[0, rb, :], kb_ref[0, rb, :], vb_ref[0, rb, :],
                                        gcol, grow, tri_ref[1], trit_ref[1], 0, mask_b,
                                        ct_ref, n_ref, m_ref, 1)


def _mlstm(mls_proj, gate_proj, gate_rows, bcol, brow, tri, tri_t):
    b, s, _ = mls_proj.shape
    nb = s // REC_BLOCK
    cpb = REC_BLOCK // CHUNK
    n_gates = 4 * MLSTM_HEADS
    fwd = lambda col: pl.BlockSpec((1, REC_BLOCK, MLSTM_W), lambda bi, j: (bi, j, col))
    bwd = lambda col: pl.BlockSpec((1, REC_BLOCK, MLSTM_W), lambda bi, j: (bi, nb - 1 - j, col))
    const = lambda shape: pl.BlockSpec(shape, lambda bi, j: (0,) * len(shape))
    gcol_f = pl.BlockSpec((1, REC_BLOCK, GATE_W), lambda bi, j: (bi, j, 0))
    gcol_b = pl.BlockSpec((1, REC_BLOCK, GATE_W), lambda bi, j: (bi, nb - 1 - j, 0))
    grow_f = pl.BlockSpec((1, cpb, n_gates, CHUNK), lambda bi, j: (bi, j, 0, 0))
    grow_b = pl.BlockSpec((1, cpb, n_gates, CHUNK), lambda bi, j: (bi, nb - 1 - j, 0, 0))
    out_shape = jax.ShapeDtypeStruct((b, s, MLSTM_W), F32)
    n_state = 2 * MLSTM_HEADS
    return pl.pallas_call(
        _mlstm_kernel,
        grid=(b, nb),
        in_specs=[fwd(0), fwd(1), fwd(2), gcol_f, grow_f, bwd(0), bwd(1), bwd(2), gcol_b, grow_b,
                  const(bcol.shape), const(brow.shape), const(tri.shape), const(tri_t.shape)],
        out_specs=[pl.BlockSpec((1, REC_BLOCK, MLSTM_W), lambda bi, j: (bi, j, 0)),
                   pl.BlockSpec((1, REC_BLOCK, MLSTM_W), lambda bi, j: (bi, nb - 1 - j, 0))],
        out_shape=[out_shape, out_shape],
        scratch_shapes=[pltpu.VMEM((n_state, HEAD_DIM, HEAD_DIM), F32),
                        pltpu.VMEM((n_state, 1, HEAD_DIM), F32),
                        pltpu.VMEM((n_state, 1, LANE), F32)],
        compiler_params=_params("parallel", "arbitrary"),
        name="mlstm",
    )(mls_proj, mls_proj, mls_proj, gate_proj, gate_rows,
      mls_proj, mls_proj, mls_proj, gate_proj, gate_rows, bcol, brow, tri, tri_t)


def _outproj_kernel(x_ref, attn_ref, gof_ref, gob_ref, gg_ref, mhf_ref, mhb_ref, mo_ref,
                    gng_ref, mng_ref, bd_ref, wo_ref, out_ref):
    bd = bd_ref[...]

    def head_norm(t, gain):
        return t * lax.rsqrt(_group_mean(t * t, bd) + EPS) * gain

    gg = gg_ref[0]
    gla = head_norm(gof_ref[0] + gob_ref[0], gng_ref[...]) * (gg * _sigmoid(gg))
    mls = head_norm(_sigmoid(mo_ref[0]) * (mhf_ref[0] + mhb_ref[0]), mng_ref[...])
    acc = jnp.dot(attn_ref[0], wo_ref[:ATTN_W, :], preferred_element_type=F32)
    acc += jnp.dot(gla.astype(BF16), wo_ref[ATTN_W:ATTN_W + GLA_W, :], preferred_element_type=F32)
    acc += jnp.dot(mls.astype(BF16), wo_ref[ATTN_W + GLA_W:, :], preferred_element_type=F32)
    out_ref[0] = x_ref[0] + acc


def _outproj(x, attn, gof, gob, gla_proj, mhf, mhb, mls_proj, gng, mng, bd, wo):
    b, s, d = x.shape
    row = lambda n, col=0: pl.BlockSpec((1, ROW_TILE, n), lambda bi, i: (bi, i, col))
    const = lambda shape: pl.BlockSpec(shape, lambda bi, i: (0,) * len(shape))
    return pl.pallas_call(
        _outproj_kernel,
        grid=(b, s // ROW_TILE),
        in_specs=[row(d), row(ATTN_W), row(GLA_W), row(GLA_W), row(GLA_W, 3),
                  row(MLSTM_W), row(MLSTM_W), row(MLSTM_W, 3),
                  const((1, GLA_W)), const((1, MLSTM_W)), const(bd.shape), const(wo.shape)],
        out_specs=row(d),
        out_shape=jax.ShapeDtypeStruct((b, s, d), F32),
        compiler_params=_params("parallel", "parallel"),
        name="outproj",
    )(x, attn, gof, gob, gla_proj, mhf, mhb, mls_proj, gng, mng, bd, wo)


def _ffn_kernel(x_ref, prev_ref, next_ref, g_ref, wg_ref, wv_ref, cwg_ref, cwv_ref, cbg_ref, cbv_ref,
                wd_ref, out_ref, h_ref, ug_ref, uv_ref, acc_ref):
    i, j = pl.program_id(1), pl.program_id(2)
    tm = x_ref.shape[1]

    def normed(t):
        ms = jnp.mean(t * t, axis=-1, keepdims=True)
        return t * lax.rsqrt(ms + EPS) * g_ref[...]

    @pl.when(j == 0)
    def _():
        keep_prev = (i > 0).astype(F32)
        keep_next = (i < pl.num_programs(1) - 1).astype(F32)
        h_ref[0:HALO, :] = (normed(prev_ref[0]) * keep_prev).astype(BF16)
        h_ref[HALO:HALO + tm, :] = normed(x_ref[0]).astype(BF16)
        h_ref[HALO + tm:, :] = (normed(next_ref[0]) * keep_next).astype(BF16)
        acc_ref[...] = jnp.zeros(acc_ref.shape, F32)

    h = h_ref[...]
    ug_ref[...] = jnp.dot(h, wg_ref[...], preferred_element_type=F32)
    uv_ref[...] = jnp.dot(h, wv_ref[...], preferred_element_type=F32)

    def conv(u_ref, cw_ref, cb_ref):
        out = cb_ref[...] + u_ref[HALO - 1:HALO - 1 + tm, :] * cw_ref[0:1, :]
        for t in range(1, CONV_WIDTH):
            out = out + u_ref[HALO - 1 + t:HALO - 1 + t + tm, :] * cw_ref[t:t + 1, :]
        return out

    gate = conv(ug_ref, cwg_ref, cbg_ref)
    val = conv(uv_ref, cwv_ref, cbv_ref)
    act = (gate * _sigmoid(gate) * val).astype(BF16)
    acc_ref[...] += jnp.dot(act, wd_ref[...], preferred_element_type=F32)

    @pl.when(j == pl.num_programs(2) - 1)
    def _():
        out_ref[0] = x_ref[0] + acc_ref[...]


def _ffn(x, g, w_up, conv_w, conv_b, w_down):
    b, s, d = x.shape
    nj = D_FF // FF_BLOCK
    tiles_per_halo = ROW_TILE // HALO
    last_halo = s // HALO - 1
    row = pl.BlockSpec((1, ROW_TILE, d), lambda bi, i, j: (bi, i, 0))
    prev = pl.BlockSpec((1, HALO, d), lambda bi, i, j: (bi, jnp.maximum(i * tiles_per_halo - 1, 0), 0))
    nxt = pl.BlockSpec((1, HALO, d),
                       lambda bi, i, j: (bi, jnp.minimum((i + 1) * tiles_per_halo, last_halo), 0))
    gate_cols = lambda rows: pl.BlockSpec((rows, FF_BLOCK), lambda bi, i, j: (0, j))
    val_cols = lambda rows: pl.BlockSpec((rows, FF_BLOCK), lambda bi, i, j: (0, nj + j))
    return pl.pallas_call(
        _ffn_kernel,
        grid=(b, s // ROW_TILE, nj),
        in_specs=[row, prev, nxt, pl.BlockSpec((1, d), lambda bi, i, j: (0, 0)),
                  gate_cols(d), val_cols(d), gate_cols(CONV_WIDTH), val_cols(CONV_WIDTH),
                  gate_cols(1), val_cols(1),
                  pl.BlockSpec((FF_BLOCK, d), lambda bi, i, j: (j, 0))],
        out_specs=row,
        out_shape=jax.ShapeDtypeStruct((b, s, d), F32),
        scratch_shapes=[pltpu.VMEM((ROW_TILE + 2 * HALO, d), BF16),
                        pltpu.VMEM((ROW_TILE + 2 * HALO, FF_BLOCK), F32),
                        pltpu.VMEM((ROW_TILE + 2 * HALO, FF_BLOCK), F32),
                        pltpu.VMEM((ROW_TILE, d), F32)],
        compiler_params=_params("parallel", "parallel", "arbitrary"),
        name="ffn",
    )(x, x, x, g, w_up, w_up, conv_w, conv_w, conv_b, conv_b, w_down)


def _rope_tables(seq):
    n_rows = seq // GRID_W
    row = jnp.repeat(jnp.arange(n_rows, dtype=F32), GRID_W)
    col = jnp.tile(jnp.arange(GRID_W, dtype=F32), n_rows)
    n_freq = HEAD_DIM // 4
    inv_freq = jnp.power(ROPE_THETA, -jnp.arange(n_freq, dtype=F32) / n_freq)
    ang_r, ang_c = row[:, None] * inv_freq, col[:, None] * inv_freq
    cos_h = jnp.concatenate([jnp.cos(ang_r), jnp.cos(ang_r), jnp.cos(ang_c), jnp.cos(ang_c)], axis=1)
    sin_h = jnp.concatenate([-jnp.sin(ang_r), jnp.sin(ang_r), -jnp.sin(ang_c), jnp.sin(ang_c)], axis=1)
    reps = QK_W // HEAD_DIM
    return jnp.tile(cos_h, (1, reps)), jnp.tile(sin_h, (1, reps))


def _block_diag_mean(width):
    group = np.arange(width) // HEAD_DIM
    return jnp.asarray((group[:, None] == group[None, :]).astype(np.float32) / HEAD_DIM, dtype=BF16)


def _tri_constants():
    t = np.arange(CHUNK)
    tril = (t[None, :] <= t[:, None]).astype(np.float32)
    triu = (t[None, :] >= t[:, None]).astype(np.float32)
    tri = jnp.asarray(np.stack([tril, triu]))
    tri_t = jnp.asarray(np.stack([tril.T, triu.T]))
    return tri, tri_t


def _layer(x, consts, norm_mix_g, w_in, attn_qn_g, attn_kn_g, gla_gate_w2, gla_gate_b, gla_norm_g,
           mlstm_gate_b, mlstm_norm_g, w_out, norm_ffn_g, w_up, conv_w, conv_b, w_down):
    cos_t, sin_t, bd_qk, bd_head, tri, tri_t = consts
    b, s, _ = x.shape
    n_gates = 4 * MLSTM_HEADS
    main_w = QKV_W + 4 * GLA_W
    gla_a = w_in[:, main_w:main_w + 2 * GLA_RANK]
    mls_main = w_in[:, main_w + 2 * GLA_RANK:main_w + 2 * GLA_RANK + 4 * MLSTM_W]
    mls_g = w_in[:, main_w + 2 * GLA_RANK + 4 * MLSTM_W:]
    pad = jnp.zeros((D_MODEL, GATE_W - 2 * GLA_RANK - n_gates), F32)
    w_in_r = jnp.concatenate([w_in[:, :main_w], mls_main, gla_a, mls_g, pad], axis=1).astype(BF16)

    qkv, gla_proj, mls_proj, gate_proj = _inproj(x, norm_mix_g[None, :], w_in_r)

    qk_gain = jnp.concatenate([jnp.tile(attn_qn_g, ATTN_HEADS), jnp.tile(attn_kn_g, ATTN_KV_HEADS)])[None, :]
    q, k, v = _qkprep(qkv, cos_t, sin_t, qk_gain, bd_qk)
    attn = _attention(q, k, v)

    w2ext = jnp.zeros((2, GATE_W, GLA_W), F32)
    w2ext = w2ext.at[0, :GLA_RANK].set(gla_gate_w2[0]).at[1, GLA_RANK:2 * GLA_RANK].set(gla_gate_w2[1])
    gof, gob = _gla(gla_proj, gate_proj, w2ext, gla_gate_b[:, None, :], tri)

    gate_rows = gate_proj[:, :, 2 * GLA_RANK:2 * GLA_RANK + n_gates]
    gate_rows = gate_rows.reshape(b, s // CHUNK, CHUNK, n_gates).transpose(0, 1, 3, 2)
    bflat = mlstm_gate_b.reshape(n_gates)
    mhf, mhb = _mlstm(mls_proj, gate_proj, gate_rows, bflat[None, :], bflat[:, None], tri, tri_t)

    x = _outproj(x, attn, gof, gob, gla_proj, mhf, mhb, mls_proj,
                 jnp.tile(gla_norm_g, GLA_HEADS)[None, :], jnp.tile(mlstm_norm_g, MLSTM_HEADS)[None, :],
                 bd_head, w_out.astype(BF16))
    return _ffn(x, norm_ffn_g[None, :], w_up.astype(BF16), conv_w, conv_b[None, :], w_down.astype(BF16))


def _trunk(x, weights):
    seq = x.shape[1]
    cos_t, sin_t = _rope_tables(seq)
    tri, tri_t = _tri_constants()
    consts = (cos_t, sin_t, _block_diag_mean(QK_W), _block_diag_mean(GLA_W), tri, tri_t)
    depth = weights[0].shape[0]
    for l in range(depth):
        x = _layer(x, consts, *[w[l] for w in weights])
    return x


def kernel(x_prompt, x_sample, norm_mix_g, w_in, attn_qn_g, attn_kn_g, gla_gate_w2, gla_gate_b,
           gla_norm_g, mlstm_gate_b, mlstm_norm_g, w_out, norm_ffn_g, w_up, conv_w, conv_b, w_down):
    weights = (norm_mix_g, w_in, attn_qn_g, attn_kn_g, gla_gate_w2, gla_gate_b, gla_norm_g,
               mlstm_gate_b, mlstm_norm_g, w_out, norm_ffn_g, w_up, conv_w, conv_b, w_down)
    assert x_prompt.shape[1:] == x_sample.shape[1:]
    n_prompt = x_prompt.shape[0]
    y = _trunk(jnp.concatenate([x_prompt, x_sample], axis=0), weights)
    return y[:n_prompt], y[n_prompt:]
```

```python
import functools

import jax
import jax.numpy as jnp
import numpy as np
from jax import lax
from jax.experimental import pallas as pl
from jax.experimental.pallas import tpu as pltpu

F32 = jnp.float32
BF16 = jnp.bfloat16

D_MODEL = 1024
GRID_W = 64
HEAD_DIM = 64
EPS = 1e-6
ATTN_HEADS = 8
ATTN_KV_HEADS = 2
ATTN_GROUP = ATTN_HEADS // ATTN_KV_HEADS
ROPE_THETA = 10000.0
GLA_HEADS = 4
GLA_RANK = 16
GLA_TAU = 16.0
MLSTM_HEADS = 4
CHUNK = 64
ATTN_W = ATTN_HEADS * HEAD_DIM
KV_W = ATTN_KV_HEADS * HEAD_DIM
GLA_W = GLA_HEADS * HEAD_DIM
MLSTM_W = MLSTM_HEADS * HEAD_DIM
QK_W = ATTN_W + KV_W
QKV_W = ATTN_W + 2 * KV_W
D_FF = 2816
CONV_WIDTH = 3
GATE_W = 128
Q_SCALE = HEAD_DIM ** -0.5
LOG2_E = 1.4426950408889634

LANE = 128
SUBLANE = 8
VMEM_LIMIT = 56 * 1024 * 1024

ROW_TILE = 512
REC_BLOCK = 256
ATTN_TQ = 128
ATTN_TK = 512
FF_BLOCK = 1408
HALO = SUBLANE

NT_DIMS = (((1,), (1,)), ((), ()))
TN_DIMS = (((0,), (0,)), ((), ()))


def _params(*sem):
    return pltpu.CompilerParams(dimension_semantics=sem, vmem_limit_bytes=VMEM_LIMIT)


def _log_sigmoid(z):
    return jnp.minimum(z, 0.0) - jnp.log(1.0 + jnp.exp(-jnp.abs(z)))


def _sigmoid(z):
    return 1.0 / (1.0 + jnp.exp(-z))


def _dot(a, b):
    return jnp.dot(a.astype(BF16), b.astype(BF16), preferred_element_type=F32)


def _dot_dims(a, b, dims):
    return lax.dot_general(a.astype(BF16), b.astype(BF16), dims, preferred_element_type=F32)


def _dot_f32(a, b):
    return jnp.dot(a, b, preferred_element_type=F32, precision=lax.Precision.HIGHEST)


def _group_mean(sq, bd):
    hi = sq.astype(BF16)
    lo = (sq - hi.astype(F32)).astype(BF16)
    return (jnp.dot(hi, bd, preferred_element_type=F32)
            + jnp.dot(lo, bd, preferred_element_type=F32))


def _inproj_kernel(x_ref, g_ref, w_ref, qkv_ref, gla_ref, mls_ref, gate_ref):
    x = x_ref[0]
    ms = jnp.mean(x * x, axis=-1, keepdims=True)
    h = (x * lax.rsqrt(ms + EPS) * g_ref[...]).astype(BF16)
    col = 0
    for ref in (qkv_ref, gla_ref, mls_ref, gate_ref):
        width = ref.shape[-1]
        ref[0] = jnp.dot(h, w_ref[:, col:col + width], preferred_element_type=F32)
        col += width


def _inproj(x, g, w):
    b, s, d = x.shape
    widths = (QKV_W, 4 * GLA_W, 4 * MLSTM_W, GATE_W)
    row = lambda n: pl.BlockSpec((1, ROW_TILE, n), lambda bi, i: (bi, i, 0))
    const = lambda shape: pl.BlockSpec(shape, lambda bi, i: (0,) * len(shape))
    return pl.pallas_call(
        _inproj_kernel,
        grid=(b, s // ROW_TILE),
        in_specs=[row(d), const((1, d)), const(w.shape)],
        out_specs=[row(n) for n in widths],
        out_shape=[jax.ShapeDtypeStruct((b, s, n), F32) for n in widths],
        compiler_params=_params("parallel", "parallel"),
        name="inproj",
    )(x, g, w)


def _qkprep_kernel(qkv_ref, cos_ref, sin_ref, gain_ref, bd_ref, qt_ref, k_ref, vt_ref):
    x = qkv_ref[0]
    qk = x[:, :QK_W]
    ms = _group_mean(qk * qk, bd_ref[...])
    y = qk * lax.rsqrt(ms + EPS) * gain_ref[...]
    lane = lax.broadcasted_iota(jnp.int32, y.shape, 1)
    first_half = (lane & 31) < 16
    quarter = HEAD_DIM // 4
    swapped = jnp.where(first_half, pltpu.roll(y, QK_W - quarter, 1), pltpu.roll(y, quarter, 1))
    r = y * cos_ref[...] + swapped * sin_ref[...]
    q_t = (r[:, :ATTN_W] * (Q_SCALE * LOG2_E)).T.astype(BF16)
    qt_ref[0] = q_t.reshape(ATTN_HEADS, HEAD_DIM, q_t.shape[-1])
    v_t = x[:, QK_W:].T.astype(BF16)
    vt_ref[0] = v_t.reshape(ATTN_KV_HEADS, HEAD_DIM, v_t.shape[-1])
    for h in range(ATTN_KV_HEADS):
        k_ref[0, h] = r[:, ATTN_W + h * HEAD_DIM:ATTN_W + (h + 1) * HEAD_DIM].astype(BF16)


def _qkprep(qkv, cos_t, sin_t, gain, bd):
    b, s, _ = qkv.shape
    const = lambda shape: pl.BlockSpec(shape, lambda bi, i: (0,) * len(shape))
    tab = pl.BlockSpec((ROW_TILE, QK_W), lambda bi, i: (i, 0))
    t_spec = lambda heads: pl.BlockSpec((1, heads, HEAD_DIM, ROW_TILE), lambda bi, i: (bi, 0, 0, i))
    t_shape = lambda heads: jax.ShapeDtypeStruct((b, heads, HEAD_DIM, s), BF16)
    return pl.pallas_call(
        _qkprep_kernel,
        grid=(b, s // ROW_TILE),
        in_specs=[pl.BlockSpec((1, ROW_TILE, QKV_W), lambda bi, i: (bi, i, 0)),
                  tab, tab, const((1, QK_W)), const((QK_W, QK_W))],
        out_specs=[t_spec(ATTN_HEADS),
                   pl.BlockSpec((1, ATTN_KV_HEADS, ROW_TILE, HEAD_DIM), lambda bi, i: (bi, 0, i, 0)),
                   t_spec(ATTN_KV_HEADS)],
        out_shape=[t_shape(ATTN_HEADS),
                   jax.ShapeDtypeStruct((b, ATTN_KV_HEADS, s, HEAD_DIM), BF16),
                   t_shape(ATTN_KV_HEADS)],
        compiler_params=_params("parallel", "parallel"),
        name="qkprep",
    )(qkv, cos_t, sin_t, gain, bd)


def _attn_kernel(qt_ref, k_ref, vt_ref, o_ref):
    tq = qt_ref.shape[-1]
    n_keys = k_ref.shape[2]
    q_t = jnp.concatenate([qt_ref[0, j] for j in range(ATTN_GROUP)], axis=1)
    rows = q_t.shape[1]
    m = jnp.full((1, rows), -jnp.inf, F32)
    l = jnp.zeros((1, rows), F32)
    acc = jnp.zeros((HEAD_DIM, rows), F32)
    for c in range(n_keys // ATTN_TK):
        keys = slice(c * ATTN_TK, (c + 1) * ATTN_TK)
        s_t = jnp.dot(k_ref[0, 0, keys, :], q_t, preferred_element_type=F32)
        m_new = jnp.maximum(m, jnp.max(s_t, axis=0, keepdims=True))
        p_t = jnp.exp2(s_t - m_new)
        alpha = jnp.exp2(m - m_new)
        l = alpha * l + jnp.sum(p_t, axis=0, keepdims=True)
        acc = alpha * acc + jnp.dot(vt_ref[0, 0, :, keys], p_t.astype(BF16), preferred_element_type=F32)
        m = m_new
    out = (acc / l).T
    for j in range(ATTN_GROUP):
        o_ref[0, :, j * HEAD_DIM:(j + 1) * HEAD_DIM] = out[j * tq:(j + 1) * tq, :].astype(o_ref.dtype)


def _attention(q_t, k, v_t):
    b, _, _, s = q_t.shape
    gw = ATTN_GROUP * HEAD_DIM
    return pl.pallas_call(
        _attn_kernel,
        grid=(b, ATTN_KV_HEADS, s // ATTN_TQ),
        in_specs=[pl.BlockSpec((1, ATTN_GROUP, HEAD_DIM, ATTN_TQ), lambda bi, h, i: (bi, h, 0, i)),
                  pl.BlockSpec((1, 1, s, HEAD_DIM), lambda bi, h, i: (bi, h, 0, 0)),
                  pl.BlockSpec((1, 1, HEAD_DIM, s), lambda bi, h, i: (bi, h, 0, 0))],
        out_specs=pl.BlockSpec((1, ATTN_TQ, gw), lambda bi, h, i: (bi, i, h)),
        out_shape=jax.ShapeDtypeStruct((b, s, ATTN_W), BF16),
        compiler_params=_params("parallel", "parallel", "arbitrary"),
        name="attention",
    )(q_t, k, v_t)


def _chunk_masks():
    row = lax.broadcasted_iota(jnp.int32, (CHUNK, CHUNK), 0)
    col = lax.broadcasted_iota(jnp.int32, (CHUNK, CHUNK), 1)
    return col <= row, col >= row


def _gla_chunk(q, k, v, a, w2, gb, tri, last, mask, st_ref, d):
    z = _dot_f32(a, w2) + gb
    g = _log_sigmoid(z) * (1.0 / GLA_TAU)
    cum = _dot_f32(tri, g)
    total = cum[last:last + 1]
    ref = cum[CHUNK // 2:CHUNK // 2 + 1]
    qt = q * Q_SCALE * jnp.exp(cum - ref)
    kt = k * jnp.exp(ref - cum)
    qe = qt * jnp.exp(ref)
    kd = kt * jnp.exp(total - ref)
    dec = jnp.exp(total)
    outs = []
    for h in range(GLA_HEADS):
        sl = slice(h * HEAD_DIM, (h + 1) * HEAD_DIM)
        att = jnp.where(mask, _dot_dims(qt[:, sl], kt[:, sl], NT_DIMS), 0.0)
        st = st_ref[d, h]
        outs.append(_dot_dims(qe[:, sl], st, NT_DIMS) + _dot(att, v[:, sl]))
        st_ref[d, h] = st * dec[:, sl] + _dot_dims(v[:, sl], kd[:, sl], TN_DIMS)
    return jnp.concatenate(outs, axis=1)


def _gla_kernel(qf_ref, kf_ref, vf_ref, af_ref, qb_ref, kb_ref, vb_ref, ab_ref,
                w2_ref, gb_ref, tri_ref, of_ref, ob_ref, st_ref):
    @pl.when(pl.program_id(1) == 0)
    def _():
        st_ref[...] = jnp.zeros(st_ref.shape, F32)

    mask_f, mask_b = _chunk_masks()
    n_chunks = REC_BLOCK // CHUNK
    for i in range(n_chunks):
        rf = slice(i * CHUNK, (i + 1) * CHUNK)
        of_ref[0, rf, :] = _gla_chunk(qf_ref[0, rf, :], kf_ref[0, rf, :], vf_ref[0, rf, :],
                                      af_ref[0, rf, :], w2_ref[0], gb_ref[0], tri_ref[0],
                                      CHUNK - 1, mask_f, st_ref, 0)
        rb = slice((n_chunks - 1 - i) * CHUNK, (n_chunks - i) * CHUNK)
        ob_ref[0, rb, :] = _gla_chunk(qb_ref[0, rb, :], kb_ref[0, rb, :], vb_ref[0, rb, :],
                                      ab_ref[0, rb, :], w2_ref[1], gb_ref[1], tri_ref[1],
                                      0, mask_b, st_ref, 1)


def _gla(gla_proj, gate_proj, w2ext, gb, tri):
    b, s, _ = gla_proj.shape
    nb = s // REC_BLOCK
    fwd = lambda col: pl.BlockSpec((1, REC_BLOCK, GLA_W), lambda bi, j: (bi, j, col))
    bwd = lambda col: pl.BlockSpec((1, REC_BLOCK, GLA_W), lambda bi, j: (bi, nb - 1 - j, col))
    const = lambda shape: pl.BlockSpec(shape, lambda bi, j: (0,) * len(shape))
    gate_f = pl.BlockSpec((1, REC_BLOCK, GATE_W), lambda bi, j: (bi, j, 0))
    gate_b = pl.BlockSpec((1, REC_BLOCK, GATE_W), lambda bi, j: (bi, nb - 1 - j, 0))
    out_shape = jax.ShapeDtypeStruct((b, s, GLA_W), F32)
    return pl.pallas_call(
        _gla_kernel,
        grid=(b, nb),
        in_specs=[fwd(0), fwd(1), fwd(2), gate_f, bwd(0), bwd(1), bwd(2), gate_b,
                  const(w2ext.shape), const(gb.shape), const(tri.shape)],
        out_specs=[pl.BlockSpec((1, REC_BLOCK, GLA_W), lambda bi, j: (bi, j, 0)),
                   pl.BlockSpec((1, REC_BLOCK, GLA_W), lambda bi, j: (bi, nb - 1 - j, 0))],
        out_shape=[out_shape, out_shape],
        scratch_shapes=[pltpu.VMEM((2, GLA_HEADS, HEAD_DIM, HEAD_DIM), F32)],
        compiler_params=_params("parallel", "arbitrary"),
        name="gla",
    )(gla_proj, gla_proj, gla_proj, gate_proj, gla_proj, gla_proj, gla_proj, gate_proj,
      w2ext, gb, tri)


def _mlstm_chunk(q, k, v, gcol, grow, tri, tri_t, last, mask, ct_ref, n_ref, m_ref, d):
    fcum_col = _dot_f32(tri, _log_sigmoid(gcol))
    fcum_row = _dot_f32(_log_sigmoid(grow), tri_t)
    outs = []
    for h in range(MLSTM_HEADS):
        sl = slice(h * HEAD_DIM, (h + 1) * HEAD_DIM)
        ci = d * MLSTM_HEADS + h
        cf = 2 * MLSTM_HEADS + ci
        idx = d * MLSTM_HEADS + h
        fc, fr = fcum_col[:, cf:cf + 1], fcum_row[cf:cf + 1, :]
        ic, ir = gcol[:, ci:ci + 1], grow[ci:ci + 1, :]
        m_st = m_ref[idx][:, 0:1]
        qh, kh, vh = q[:, sl], k[:, sl] * Q_SCALE, v[:, sl]
        log_d = jnp.where(mask, fc - fr + ir, -jnp.inf)
        inter = fc + m_st
        m_t = jnp.maximum(inter, jnp.max(log_d, axis=-1, keepdims=True))
        w_inter = jnp.exp(inter - m_t)
        qk = _dot_dims(qh, kh, NT_DIMS) * jnp.exp(log_d - m_t)
        ct = ct_ref[idx]
        n_st = n_ref[idx]
        num = w_inter * _dot(qh, ct) + _dot(qk, vh)
        den = (w_inter * jnp.sum(qh * n_st, axis=-1, keepdims=True)
               + jnp.sum(qk, axis=-1, keepdims=True))
        outs.append(num / jnp.maximum(jnp.abs(den), jnp.exp(-m_t)))
        f_tot = fc[last:last + 1]
        m_new = jnp.maximum(f_tot + m_st, jnp.max(f_tot - fr + ir, axis=-1, keepdims=True))
        decay = jnp.exp(f_tot + m_st - m_new)
        wk = kh * jnp.exp(f_tot - fc + ic - m_new)
        ct_ref[idx] = decay * ct + _dot_dims(wk, vh, TN_DIMS)
        n_ref[idx] = decay * n_st + jnp.sum(wk, axis=0, keepdims=True)
        m_ref[idx] = jnp.broadcast_to(m_new, m_ref.shape[1:])
    return jnp.concatenate(outs, axis=1)


def _mlstm_kernel(qf_ref, kf_ref, vf_ref, gcf_ref, grf_ref, qb_ref, kb_ref, vb_ref, gcb_ref, grb_ref,
                  bcol_ref, brow_ref, tri_ref, trit_ref, hf_ref, hb_ref, ct_ref, n_ref, m_ref):
    @pl.when(pl.program_id(1) == 0)
    def _():
        ct_ref[...] = jnp.zeros(ct_ref.shape, F32)
        n_ref[...] = jnp.zeros(n_ref.shape, F32)
        m_ref[...] = jnp.zeros(m_ref.shape, F32)

    mask_f, mask_b = _chunk_masks()
    n_chunks = REC_BLOCK // CHUNK
    n_gates = 4 * MLSTM_HEADS
    for i in range(n_chunks):
        rf = slice(i * CHUNK, (i + 1) * CHUNK)
        gcol = gcf_ref[0, rf, 2 * GLA_RANK:2 * GLA_RANK + n_gates] + bcol_ref[...]
        grow = grf_ref[0, i] + brow_ref[...]
        hf_ref[0, rf, :] = _mlstm_chunk(qf_ref[0, rf, :], kf_ref[0, rf, :], vf_ref[0, rf, :],
                                        gcol, grow, tri_ref[0], trit_ref[0], CHUNK - 1, mask_f,
                                        ct_ref, n_ref, m_ref, 0)
        ib = n_chunks - 1 - i
        rb = slice(ib * CHUNK, (ib + 1) * CHUNK)
        gcol = gcb_ref[0, rb, 2 * GLA_RANK:2 * GLA_RANK + n_gates] + bcol_ref[...]
        grow = grb_ref[0, ib] + brow_ref[...]
        hb_ref[0, rb, :] = _mlstm_chunk(qb_ref[0, rb, :], kb_ref[0, rb, :], vb_ref[0, rb, :],
                                        gcol, grow, tri_ref[1], trit_ref[1], 0, mask_b,
                                        ct_ref, n_ref, m_ref, 1)


def _mlstm(mls_proj, gate_proj, gate_rows, bcol, brow, tri, tri_t):
    b, s, _ = mls_proj.shape
    nb = s // REC_BLOCK
    cpb = REC_BLOCK // CHUNK
    n_gates = 4 * MLSTM_HEADS
    fwd = lambda col: pl.BlockSpec((1, REC_BLOCK, MLSTM_W), lambda bi, j: (bi, j, col))
    bwd = lambda col: pl.BlockSpec((1, REC_BLOCK, MLSTM_W), lambda bi, j: (bi, nb - 1 - j, col))
    const = lambda shape: pl.BlockSpec(shape, lambda bi, j: (0,) * len(shape))
    gcol_f = pl.BlockSpec((1, REC_BLOCK, GATE_W), lambda bi, j: (bi, j, 0))
    gcol_b = pl.BlockSpec((1, REC_BLOCK, GATE_W), lambda bi, j: (bi, nb - 1 - j, 0))
    grow_f = pl.BlockSpec((1, cpb, n_gates, CHUNK), lambda bi, j: (bi, j, 0, 0))
    grow_b = pl.BlockSpec((1, cpb, n_gates, CHUNK), lambda bi, j: (bi, nb - 1 - j, 0, 0))
    out_shape = jax.ShapeDtypeStruct((b, s, MLSTM_W), F32)
    n_state = 2 * MLSTM_HEADS
    return pl.pallas_call(
        _mlstm_kernel,
        grid=(b, nb),
        in_specs=[fwd(0), fwd(1), fwd(2), gcol_f, grow_f, bwd(0), bwd(1), bwd(2), gcol_b, grow_b,
                  const(bcol.shape), const(brow.shape), const(tri.shape), const(tri_t.shape)],
        out_specs=[pl.BlockSpec((1, REC_BLOCK, MLSTM_W), lambda bi, j: (bi, j, 0)),
                   pl.BlockSpec((1, REC_BLOCK, MLSTM_W), lambda bi, j: (bi, nb - 1 - j, 0))],
        out_shape=[out_shape, out_shape],
        scratch_shapes=[pltpu.VMEM((n_state, HEAD_DIM, HEAD_DIM), F32),
                        pltpu.VMEM((n_state, 1, HEAD_DIM), F32),
                        pltpu.VMEM((n_state, 1, LANE), F32)],
        compiler_params=_params("parallel", "arbitrary"),
        name="mlstm",
    )(mls_proj, mls_proj, mls_proj, gate_proj, gate_rows,
      mls_proj, mls_proj, mls_proj, gate_proj, gate_rows, bcol, brow, tri, tri_t)


def _outproj_kernel(x_ref, attn_ref, gof_ref, gob_ref, gg_ref, mhf_ref, mhb_ref, mo_ref,
                    gng_ref, mng_ref, bd_ref, wo_ref, out_ref):
    bd = bd_ref[...]

    def head_norm(t, gain):
        return t * lax.rsqrt(_group_mean(t * t, bd) + EPS) * gain

    gg = gg_ref[0]
    gla = head_norm(gof_ref[0] + gob_ref[0], gng_ref[...]) * (gg * _sigmoid(gg))
    mls = head_norm(_sigmoid(mo_ref[0]) * (mhf_ref[0] + mhb_ref[0]), mng_ref[...])
    acc = jnp.dot(attn_ref[0], wo_ref[:ATTN_W, :], preferred_element_type=F32)
    acc += jnp.dot(gla.astype(BF16), wo_ref[ATTN_W:ATTN_W + GLA_W, :], preferred_element_type=F32)
    acc += jnp.dot(mls.astype(BF16), wo_ref[ATTN_W + GLA_W:, :], preferred_element_type=F32)
    out_ref[0] = x_ref[0] + acc


def _outproj(x, attn, gof, gob, gla_proj, mhf, mhb, mls_proj, gng, mng, bd, wo):
    b, s, d = x.shape
    row = lambda n, col=0: pl.BlockSpec((1, ROW_TILE, n), lambda bi, i: (bi, i, col))
    const = lambda shape: pl.BlockSpec(shape, lambda bi, i: (0,) * len(shape))
    return pl.pallas_call(
        _outproj_kernel,
        grid=(b, s // ROW_TILE),
        in_specs=[row(d), row(ATTN_W), row(GLA_W), row(GLA_W), row(GLA_W, 3),
                  row(MLSTM_W), row(MLSTM_W), row(MLSTM_W, 3),
                  const((1, GLA_W)), const((1, MLSTM_W)), const(bd.shape), const(wo.shape)],
        out_specs=row(d),
        out_shape=jax.ShapeDtypeStruct((b, s, d), F32),
        compiler_params=_params("parallel", "parallel"),
        name="outproj",
    )(x, attn, gof, gob, gla_proj, mhf, mhb, mls_proj, gng, mng, bd, wo)


def _ffn_kernel(x_ref, prev_ref, next_ref, g_ref, wg_ref, wv_ref, cwg_ref, cwv_ref, cbg_ref, cbv_ref,
                wd_ref, out_ref, h_ref, ug_ref, uv_ref, acc_ref):
    i, j = pl.program_id(1), pl.program_id(2)
    tm = x_ref.shape[1]

    def normed(t):
        ms = jnp.mean(t * t, axis=-1, keepdims=True)
        return t * lax.rsqrt(ms + EPS) * g_ref[...]

    @pl.when(j == 0)
    def _():
        keep_prev = (i > 0).astype(F32)
        keep_next = (i < pl.num_programs(1) - 1).astype(F32)
        h_ref[0:HALO, :] = (normed(prev_ref[0]) * keep_prev).astype(BF16)
        h_ref[HALO:HALO + tm, :] = normed(x_ref[0]).astype(BF16)
        h_ref[HALO + tm:, :] = (normed(next_ref[0]) * keep_next).astype(BF16)
        acc_ref[...] = jnp.zeros(acc_ref.shape, F32)

    h = h_ref[...]
    ug_ref[...] = jnp.dot(h, wg_ref[...], preferred_element_type=F32)
    uv_ref[...] = jnp.dot(h, wv_ref[...], preferred_element_type=F32)

    def conv(u_ref, cw_ref, cb_ref):
        out = cb_ref[...] + u_ref[HALO - 1:HALO - 1 + tm, :] * cw_ref[0:1, :]
        for t in range(1, CONV_WIDTH):
            out = out + u_ref[HALO - 1 + t:HALO - 1 + t + tm, :] * cw_ref[t:t + 1, :]
        return out

    gate = conv(ug_ref, cwg_ref, cbg_ref)
    val = conv(uv_ref, cwv_ref, cbv_ref)
    act = (gate * _sigmoid(gate) * val).astype(BF16)
    acc_ref[...] += jnp.dot(act, wd_ref[...], preferred_element_type=F32)

    @pl.when(j == pl.num_programs(2) - 1)
    def _():
        out_ref[0] = x_ref[0] + acc_ref[...]


def _ffn(x, g, w_up, conv_w, conv_b, w_down):
    b, s, d = x.shape
    nj = D_FF // FF_BLOCK
    tiles_per_halo = ROW_TILE // HALO
    last_halo = s // HALO - 1
    row = pl.BlockSpec((1, ROW_TILE, d), lambda bi, i, j: (bi, i, 0))
    prev = pl.BlockSpec((1, HALO, d), lambda bi, i, j: (bi, jnp.maximum(i * tiles_per_halo - 1, 0), 0))
    nxt = pl.BlockSpec((1, HALO, d),
                       lambda bi, i, j: (bi, jnp.minimum((i + 1) * tiles_per_halo, last_halo), 0))
    gate_cols = lambda rows: pl.BlockSpec((rows, FF_BLOCK), lambda bi, i, j: (0, j))
    val_cols = lambda rows: pl.BlockSpec((rows, FF_BLOCK), lambda bi, i, j: (0, nj + j))
    return pl.pallas_call(
        _ffn_kernel,
        grid=(b, s // ROW_TILE, nj),
        in_specs=[row, prev, nxt, pl.BlockSpec((1, d), lambda bi, i, j: (0, 0)),
                  gate_cols(d), val_cols(d), gate_cols(CONV_WIDTH), val_cols(CONV_WIDTH),
                  gate_cols(1), val_cols(1),
                  pl.BlockSpec((FF_BLOCK, d), lambda bi, i, j: (j, 0))],
        out_specs=row,
        out_shape=jax.ShapeDtypeStruct((b, s, d), F32),
        scratch_shapes=[pltpu.VMEM((ROW_TILE + 2 * HALO, d), BF16),
                        pltpu.VMEM((ROW_TILE + 2 * HALO, FF_BLOCK), F32),
                        pltpu.VMEM((ROW_TILE + 2 * HALO, FF_BLOCK), F32),
                        pltpu.VMEM((ROW_TILE, d), F32)],
        compiler_params=_params("parallel", "parallel", "arbitrary"),
        name="ffn",
    )(x, x, x, g, w_up, w_up, conv_w, conv_w, conv_b, conv_b, w_down)


def _rope_tables(seq):
    n_rows = seq // GRID_W
    row = jnp.repeat(jnp.arange(n_rows, dtype=F32), GRID_W)
    col = jnp.tile(jnp.arange(GRID_W, dtype=F32), n_rows)
    n_freq = HEAD_DIM // 4
    inv_freq = jnp.power(ROPE_THETA, -jnp.arange(n_freq, dtype=F32) / n_freq)
    ang_r, ang_c = row[:, None] * inv_freq, col[:, None] * inv_freq
    cos_h = jnp.concatenate([jnp.cos(ang_r), jnp.cos(ang_r), jnp.cos(ang_c), jnp.cos(ang_c)], axis=1)
    sin_h = jnp.concatenate([-jnp.sin(ang_r), jnp.sin(ang_r), -jnp.sin(ang_c), jnp.sin(ang_c)], axis=1)
    reps = QK_W // HEAD_DIM
    return jnp.tile(cos_h, (1, reps)), jnp.tile(sin_h, (1, reps))


def _block_diag_mean(width):
    group = np.arange(width) // HEAD_DIM
    return jnp.asarray((group[:, None] == group[None, :]).astype(np.float32) / HEAD_DIM, dtype=BF16)


def _tri_constants():
    t = np.arange(CHUNK)
    tril = (t[None, :] <= t[:, None]).astype(np.float32)
    triu = (t[None, :] >= t[:, None]).astype(np.float32)
    tri = jnp.asarray(np.stack([tril, triu]))
    tri_t = jnp.asarray(np.stack([tril.T, triu.T]))
    return tri, tri_t


def _layer(x, consts, norm_mix_g, w_in, attn_qn_g, attn_kn_g, gla_gate_w2, gla_gate_b, gla_norm_g,
           mlstm_gate_b, mlstm_norm_g, w_out, norm_ffn_g, w_up, conv_w, conv_b, w_down):
    cos_t, sin_t, bd_qk, bd_head, tri, tri_t = consts
    b, s, _ = x.shape
    n_gates = 4 * MLSTM_HEADS
    main_w = QKV_W + 4 * GLA_W
    gla_a = w_in[:, main_w:main_w + 2 * GLA_RANK]
    mls_main = w_in[:, main_w + 2 * GLA_RANK:main_w + 2 * GLA_RANK + 4 * MLSTM_W]
    mls_g = w_in[:, main_w + 2 * GLA_RANK + 4 * MLSTM_W:]
    pad = jnp.zeros((D_MODEL, GATE_W - 2 * GLA_RANK - n_gates), F32)
    w_in_r = jnp.concatenate([w_in[:, :main_w], mls_main, gla_a, mls_g, pad], axis=1).astype(BF16)

    qkv, gla_proj, mls_proj, gate_proj = _inproj(x, norm_mix_g[None, :], w_in_r)

    qk_gain = jnp.concatenate([jnp.tile(attn_qn_g, ATTN_HEADS), jnp.tile(attn_kn_g, ATTN_KV_HEADS)])[None, :]
    q_t, k, v_t = _qkprep(qkv, cos_t, sin_t, qk_gain, bd_qk)
    attn = _attention(q_t, k, v_t)

    w2ext = jnp.zeros((2, GATE_W, GLA_W), F32)
    w2ext = w2ext.at[0, :GLA_RANK].set(gla_gate_w2[0]).at[1, GLA_RANK:2 * GLA_RANK].set(gla_gate_w2[1])
    gof, gob = _gla(gla_proj, gate_proj, w2ext, gla_gate_b[:, None, :], tri)

    gate_rows = gate_proj[:, :, 2 * GLA_RANK:2 * GLA_RANK + n_gates]
    gate_rows = gate_rows.reshape(b, s // CHUNK, CHUNK, n_gates).transpose(0, 1, 3, 2)
    bflat = mlstm_gate_b.reshape(n_gates)
    mhf, mhb = _mlstm(mls_proj, gate_proj, gate_rows, bflat[None, :], bflat[:, None], tri, tri_t)

    x = _outproj(x, attn, gof, gob, gla_proj, mhf, mhb, mls_proj,
                 jnp.tile(gla_norm_g, GLA_HEADS)[None, :], jnp.tile(mlstm_norm_g, MLSTM_HEADS)[None, :],
                 bd_head, w_out.astype(BF16))
    return _ffn(x, norm_ffn_g[None, :], w_up.astype(BF16), conv_w, conv_b[None, :], w_down.astype(BF16))


def _trunk(x, weights):
    seq = x.shape[1]
    cos_t, sin_t = _rope_tables(seq)
    tri, tri_t = _tri_constants()
    consts = (cos_t, sin_t, _block_diag_mean(QK_W), _block_diag_mean(GLA_W), tri, tri_t)
    depth = weights[0].shape[0]
    for l in range(depth):
        x = _layer(x, consts, *[w[l] for w in weights])
    return x


def kernel(x_prompt, x_sample, norm_mix_g, w_in, attn_qn_g, attn_kn_g, gla_gate_w2, gla_gate_b,
           gla_norm_g, mlstm_gate_b, mlstm_norm_g, w_out, norm_ffn_g, w_up, conv_w, conv_b, w_down):
    weights = (norm_mix_g, w_in, attn_qn_g, attn_kn_g, gla_gate_w2, gla_gate_b, gla_norm_g,
               mlstm_gate_b, mlstm_norm_g, w_out, norm_ffn_g, w_up, conv_w, conv_b, w_down)
    assert x_prompt.shape[1:] == x_sample.shape[1:]
    n_prompt = x_prompt.shape[0]
    y = _trunk(jnp.concatenate([x_prompt, x_sample], axis=0), weights)
    return y[:n_prompt], y[n_prompt:]
```

```python
import functools

import jax
import jax.numpy as jnp
import numpy as np
from jax import lax
from jax.experimental import pallas as pl
from jax.experimental.pallas import tpu as pltpu

F32 = jnp.float32
BF16 = jnp.bfloat16

D_MODEL = 1024
GRID_W = 64
HEAD_DIM = 64
EPS = 1e-6
ATTN_HEADS = 8
ATTN_KV_HEADS = 2
ATTN_GROUP = ATTN_HEADS // ATTN_KV_HEADS
ROPE_THETA = 10000.0
GLA_HEADS = 4
GLA_RANK = 16
GLA_TAU = 16.0
MLSTM_HEADS = 4
CHUNK = 64
ATTN_W = ATTN_HEADS * HEAD_DIM
KV_W = ATTN_KV_HEADS * HEAD_DIM
GLA_W = GLA_HEADS * HEAD_DIM
MLSTM_W = MLSTM_HEADS * HEAD_DIM
QK_W = ATTN_W + KV_W
QKV_W = ATTN_W + 2 * KV_W
D_FF = 2816
CONV_WIDTH = 3
GATE_W = 128
Q_SCALE = HEAD_DIM ** -0.5
LOG2_E = 1.4426950408889634

LANE = 128
SUBLANE = 8
VMEM_LIMIT = 56 * 1024 * 1024

ROW_TILE = 512
REC_BLOCK = 256
ATTN_TQ = 128
ATTN_TK = 512
FF_BLOCK = 1408
HALO = SUBLANE

NT_DIMS = (((1,), (1,)), ((), ()))
TN_DIMS = (((0,), (0,)), ((), ()))


def _params(*sem):
    return pltpu.CompilerParams(dimension_semantics=sem, vmem_limit_bytes=VMEM_LIMIT)


def _log_sigmoid(z):
    return jnp.minimum(z, 0.0) - jnp.log(1.0 + jnp.exp(-jnp.abs(z)))


def _sigmoid(z):
    return 1.0 / (1.0 + jnp.exp(-z))


def _dot(a, b):
    return jnp.dot(a.astype(BF16), b.astype(BF16), preferred_element_type=F32)


def _dot_dims(a, b, dims):
    return lax.dot_general(a.astype(BF16), b.astype(BF16), dims, preferred_element_type=F32)


def _split3(x):
    p1 = x.astype(BF16)
    rest = x - p1.astype(F32)
    p2 = rest.astype(BF16)
    return p1, p2, (rest - p2.astype(F32)).astype(BF16)


def _cumsum_cols(tbd, x):
    return sum(jnp.dot(tbd, p, preferred_element_type=F32) for p in _split3(x))


def _cumsum_rows(x, tbd_t):
    return sum(jnp.dot(p, tbd_t, preferred_element_type=F32) for p in _split3(x))


def _dot_split(a, b_hi, b_lo):
    a_hi = a.astype(BF16)
    a_lo = (a - a_hi.astype(F32)).astype(BF16)
    return (jnp.dot(a_hi, b_hi, preferred_element_type=F32)
            + jnp.dot(a_lo, b_hi, preferred_element_type=F32)
            + jnp.dot(a_hi, b_lo, preferred_element_type=F32))


def _group_mean(sq, bd):
    hi = sq.astype(BF16)
    lo = (sq - hi.astype(F32)).astype(BF16)
    return (jnp.dot(hi, bd, preferred_element_type=F32)
            + jnp.dot(lo, bd, preferred_element_type=F32))


def _inproj_kernel(x_ref, g_ref, w_ref, qkv_ref, gla_ref, mls_ref, gate_ref):
    x = x_ref[0]
    ms = jnp.mean(x * x, axis=-1, keepdims=True)
    h = (x * lax.rsqrt(ms + EPS) * g_ref[...]).astype(BF16)
    col = 0
    for ref in (qkv_ref, gla_ref, mls_ref, gate_ref):
        width = ref.shape[-1]
        ref[0] = jnp.dot(h, w_ref[:, col:col + width], preferred_element_type=F32)
        col += width


def _inproj(x, g, w):
    b, s, d = x.shape
    widths = (QKV_W, 4 * GLA_W, 4 * MLSTM_W, GATE_W)
    row = lambda n: pl.BlockSpec((1, ROW_TILE, n), lambda bi, i: (bi, i, 0))
    const = lambda shape: pl.BlockSpec(shape, lambda bi, i: (0,) * len(shape))
    return pl.pallas_call(
        _inproj_kernel,
        grid=(b, s // ROW_TILE),
        in_specs=[row(d), const((1, d)), const(w.shape)],
        out_specs=[row(n) for n in widths],
        out_shape=[jax.ShapeDtypeStruct((b, s, n), F32) for n in widths],
        compiler_params=_params("parallel", "parallel"),
        name="inproj",
    )(x, g, w)


def _qkprep_kernel(qkv_ref, cos_ref, sin_ref, gain_ref, bd_ref, qt_ref, k_ref, vt_ref):
    x = qkv_ref[0]
    qk = x[:, :QK_W]
    ms = _group_mean(qk * qk, bd_ref[...])
    y = qk * lax.rsqrt(ms + EPS) * gain_ref[...]
    lane = lax.broadcasted_iota(jnp.int32, y.shape, 1)
    first_half = (lane & 31) < 16
    quarter = HEAD_DIM // 4
    swapped = jnp.where(first_half, pltpu.roll(y, QK_W - quarter, 1), pltpu.roll(y, quarter, 1))
    r = y * cos_ref[...] + swapped * sin_ref[...]
    q_t = (r[:, :ATTN_W] * (Q_SCALE * LOG2_E)).T.astype(BF16)
    qt_ref[0] = q_t.reshape(ATTN_HEADS, HEAD_DIM, q_t.shape[-1])
    v_t = x[:, QK_W:].T.astype(BF16)
    vt_ref[0] = v_t.reshape(ATTN_KV_HEADS, HEAD_DIM, v_t.shape[-1])
    for h in range(ATTN_KV_HEADS):
        k_ref[0, h] = r[:, ATTN_W + h * HEAD_DIM:ATTN_W + (h + 1) * HEAD_DIM].astype(BF16)


def _qkprep(qkv, cos_t, sin_t, gain, bd):
    b, s, _ = qkv.shape
    const = lambda shape: pl.BlockSpec(shape, lambda bi, i: (0,) * len(shape))
    tab = pl.BlockSpec((ROW_TILE, QK_W), lambda bi, i: (i, 0))
    t_spec = lambda heads: pl.BlockSpec((1, heads, HEAD_DIM, ROW_TILE), lambda bi, i: (bi, 0, 0, i))
    t_shape = lambda heads: jax.ShapeDtypeStruct((b, heads, HEAD_DIM, s), BF16)
    return pl.pallas_call(
        _qkprep_kernel,
        grid=(b, s // ROW_TILE),
        in_specs=[pl.BlockSpec((1, ROW_TILE, QKV_W), lambda bi, i: (bi, i, 0)),
                  tab, tab, const((1, QK_W)), const((QK_W, QK_W))],
        out_specs=[t_spec(ATTN_HEADS),
                   pl.BlockSpec((1, ATTN_KV_HEADS, ROW_TILE, HEAD_DIM), lambda bi, i: (bi, 0, i, 0)),
                   t_spec(ATTN_KV_HEADS)],
        out_shape=[t_shape(ATTN_HEADS),
                   jax.ShapeDtypeStruct((b, ATTN_KV_HEADS, s, HEAD_DIM), BF16),
                   t_shape(ATTN_KV_HEADS)],
        compiler_params=_params("parallel", "parallel"),
        name="qkprep",
    )(qkv, cos_t, sin_t, gain, bd)


def _attn_kernel(qt_ref, k_ref, vt_ref, o_ref):
    tq = qt_ref.shape[-1]
    n_keys = k_ref.shape[2]
    q_t = jnp.concatenate([qt_ref[0, j] for j in range(ATTN_GROUP)], axis=1)
    rows = q_t.shape[1]
    m = jnp.full((1, rows), -jnp.inf, F32)
    l = jnp.zeros((1, rows), F32)
    acc = jnp.zeros((HEAD_DIM, rows), F32)
    for c in range(n_keys // ATTN_TK):
        keys = slice(c * ATTN_TK, (c + 1) * ATTN_TK)
        s_t = jnp.dot(k_ref[0, 0, keys, :], q_t, preferred_element_type=F32)
        m_new = jnp.maximum(m, jnp.max(s_t, axis=0, keepdims=True))
        p_t = jnp.exp2(s_t - m_new)
        alpha = jnp.exp2(m - m_new)
        l = alpha * l + jnp.sum(p_t, axis=0, keepdims=True)
        acc = alpha * acc + jnp.dot(vt_ref[0, 0, :, keys], p_t.astype(BF16), preferred_element_type=F32)
        m = m_new
    out = (acc / l).T
    for j in range(ATTN_GROUP):
        o_ref[0, :, j * HEAD_DIM:(j + 1) * HEAD_DIM] = out[j * tq:(j + 1) * tq, :].astype(o_ref.dtype)


def _attention(q_t, k, v_t):
    b, _, _, s = q_t.shape
    gw = ATTN_GROUP * HEAD_DIM
    return pl.pallas_call(
        _attn_kernel,
        grid=(b, ATTN_KV_HEADS, s // ATTN_TQ),
        in_specs=[pl.BlockSpec((1, ATTN_GROUP, HEAD_DIM, ATTN_TQ), lambda bi, h, i: (bi, h, 0, i)),
                  pl.BlockSpec((1, 1, s, HEAD_DIM), lambda bi, h, i: (bi, h, 0, 0)),
                  pl.BlockSpec((1, 1, HEAD_DIM, s), lambda bi, h, i: (bi, h, 0, 0))],
        out_specs=pl.BlockSpec((1, ATTN_TQ, gw), lambda bi, h, i: (bi, i, h)),
        out_shape=jax.ShapeDtypeStruct((b, s, ATTN_W), BF16),
        compiler_params=_params("parallel", "parallel", "arbitrary"),
        name="attention",
    )(q_t, k, v_t)


def _chunk_order(d):
    n_chunks = REC_BLOCK // CHUNK
    return range(n_chunks) if d == 0 else range(n_chunks - 1, -1, -1)


def _glagate_kernel(a_ref, w2hi_ref, w2lo_ref, gb_ref, tbd_ref, cf_ref, cb_ref):
    a = a_ref[0]
    for d, out_ref in enumerate((cf_ref, cb_ref)):
        z = _dot_split(a, w2hi_ref[d], w2lo_ref[d]) + gb_ref[d]
        g = _log_sigmoid(z) * (1.0 / GLA_TAU)
        for sb in range(a.shape[0] // REC_BLOCK):
            rows = slice(sb * REC_BLOCK, (sb + 1) * REC_BLOCK)
            out_ref[0, rows, :] = _cumsum_cols(tbd_ref[d], g[rows])


def _glagate(gate_proj, w2_hi, w2_lo, gb, tbd):
    b, s, _ = gate_proj.shape
    const = lambda shape: pl.BlockSpec(shape, lambda bi, i: (0,) * len(shape))
    out_spec = pl.BlockSpec((1, ROW_TILE, GLA_W), lambda bi, i: (bi, i, 0))
    out_shape = jax.ShapeDtypeStruct((b, s, GLA_W), F32)
    return pl.pallas_call(
        _glagate_kernel,
        grid=(b, s // ROW_TILE),
        in_specs=[pl.BlockSpec((1, ROW_TILE, GATE_W), lambda bi, i: (bi, i, 0)),
                  const(w2_hi.shape), const(w2_lo.shape), const(gb.shape), const(tbd.shape)],
        out_specs=[out_spec, out_spec],
        out_shape=[out_shape, out_shape],
        compiler_params=_params("parallel", "parallel"),
        name="glagate",
    )(gate_proj, w2_hi, w2_lo, gb, tbd)


def _gla_direction(q_ref, k_ref, v_ref, cum_ref, o_ref, st_ref, d):
    causal_t, _, head_lane, block_diag = _head_layout_masks(d)
    last = CHUNK - 1 if d == 0 else 0
    st = st_ref[d]
    for i in _chunk_order(d):
        rows = slice(i * CHUNK, (i + 1) * CHUNK)
        cum = cum_ref[0, rows, :]
        total = cum[last:last + 1]
        ref = cum[CHUNK // 2:CHUNK // 2 + 1]
        qt = q_ref[0, rows, :] * Q_SCALE * jnp.exp(cum - ref)
        kt = k_ref[0, rows, :] * jnp.exp(ref - cum)
        qe = qt * jnp.exp(ref)
        kd = kt * jnp.exp(total - ref)
        v = v_ref[0, rows, :]
        q4 = jnp.concatenate([jnp.where(head_lane[h], qt, 0.0) for h in range(GLA_HEADS)], axis=0)
        att_t = jnp.where(causal_t, _dot_dims(kt, q4, NT_DIMS), 0.0)
        yield
        full = _dot_dims(att_t, v, TN_DIMS)
        out = _dot_dims(qe, st, NT_DIMS)
        yield
        for h in range(GLA_HEADS):
            out = out + jnp.where(head_lane[h], full[h * CHUNK:(h + 1) * CHUNK], 0.0)
        o_ref[0, rows, :] = out
        st = st * jnp.exp(total) + jnp.where(block_diag, _dot_dims(v, kd, TN_DIMS), 0.0)
        yield
    st_ref[d] = st


def _interleave(*stages):
    stages = list(stages)
    while stages:
        for g in list(stages):
            if next(g, StopIteration) is StopIteration:
                stages.remove(g)


def _gla_kernel(qf_ref, kf_ref, vf_ref, cf_ref, qb_ref, kb_ref, vb_ref, cb_ref, of_ref, ob_ref, st_ref):
    @pl.when(pl.program_id(1) == 0)
    def _():
        st_ref[...] = jnp.zeros(st_ref.shape, F32)

    _interleave(_gla_direction(qf_ref, kf_ref, vf_ref, cf_ref, of_ref, st_ref, 0),
                _gla_direction(qb_ref, kb_ref, vb_ref, cb_ref, ob_ref, st_ref, 1))


def _gla(gla_proj, cum_f, cum_b):
    b, s, _ = gla_proj.shape
    nb = s // REC_BLOCK
    fwd = lambda col: pl.BlockSpec((1, REC_BLOCK, GLA_W), lambda bi, j: (bi, j, col))
    bwd = lambda col: pl.BlockSpec((1, REC_BLOCK, GLA_W), lambda bi, j: (bi, nb - 1 - j, col))
    out_shape = jax.ShapeDtypeStruct((b, s, GLA_W), F32)
    return pl.pallas_call(
        _gla_kernel,
        grid=(b, nb),
        in_specs=[fwd(0), fwd(1), fwd(2), fwd(0), bwd(0), bwd(1), bwd(2), bwd(0)],
        out_specs=[fwd(0), bwd(0)],
        out_shape=[out_shape, out_shape],
        scratch_shapes=[pltpu.VMEM((2, GLA_W, GLA_W), F32)],
        compiler_params=_params("parallel", "arbitrary"),
        name="gla",
    )(gla_proj, gla_proj, gla_proj, cum_f, gla_proj, gla_proj, gla_proj, cum_b)


def _head_layout_masks(d):
    sub = lax.broadcasted_iota(jnp.int32, (CHUNK, MLSTM_W), 0)
    lane = lax.broadcasted_iota(jnp.int32, (CHUNK, MLSTM_W), 1)
    t_lane = lane & (HEAD_DIM - 1)
    causal_t = (sub <= t_lane) if d == 0 else (sub >= t_lane)
    eye_t = sub == t_lane
    head_lane = [(lane >> 6) == h for h in range(MLSTM_HEADS)]
    brow = lax.broadcasted_iota(jnp.int32, (MLSTM_W, MLSTM_W), 0) >> 6
    bcol = lax.broadcasted_iota(jnp.int32, (MLSTM_W, MLSTM_W), 1) >> 6
    return causal_t, eye_t, head_lane, brow == bcol


def _expand3(x, e):
    return sum(jnp.dot(p, e, preferred_element_type=F32) for p in _split3(x))


def _mlstm_direction(q_ref, k_ref, v_ref, gates, frow, h_ref, tbd, tbd_t, e_r, e_f, e_h,
                     ct_ref, n_ref, m_ref, d):
    causal_t, eye_t, head_lane, block_diag = _head_layout_masks(d)
    last = CHUNK - 1 if d == 0 else 0
    n_chunks = REC_BLOCK // CHUNK
    lane = lax.broadcasted_iota(jnp.int32, gates.shape, 1)
    i_lo = 2 * GLA_RANK + d * MLSTM_HEADS
    is_i_lane = (lane >= i_lo) & (lane < i_lo + MLSTM_HEADS)
    fcum = _cumsum_cols(tbd, _log_sigmoid(gates))
    r_exp = _expand3(jnp.where(is_i_lane, gates, fcum), e_r)
    f_last = jnp.concatenate([fcum[c * CHUNK + last:c * CHUNK + last + 1] for c in range(n_chunks)]
                             + [jnp.zeros((SUBLANE - n_chunks, fcum.shape[1]), F32)], axis=0)
    f_tot_rows = _expand3(f_last, e_f)
    fc_rows = _cumsum_rows(_log_sigmoid(frow), tbd_t)

    ct, n_row, m_row = ct_ref[d], n_ref[d], m_ref[d]
    for c in _chunk_order(d):
        rows = slice(c * CHUNK, (c + 1) * CHUNK)
        q, ks, v = q_ref[0, rows, :], k_ref[0, rows, :] * Q_SCALE, v_ref[0, rows, :]
        q4 = jnp.concatenate([jnp.where(head_lane[h], q, 0.0) for h in range(MLSTM_HEADS)], axis=0)
        kn = jnp.concatenate([ks, jnp.broadcast_to(n_row, (2 * SUBLANE, MLSTM_W))], axis=0)
        z = _dot_dims(kn, q4, NT_DIMS)
        s_t, qn = z[:CHUNK], z[CHUNK:CHUNK + 1]
        yield
        rc = r_exp[rows]
        rmax_t = jnp.max(jnp.where(causal_t, rc, -jnp.inf), axis=0, keepdims=True)
        rmax = jnp.max(rc, axis=0, keepdims=True)
        p_t = s_t * jnp.exp(jnp.where(causal_t, rc - rmax_t, -jnp.inf))
        row_sum = jnp.sum(p_t, axis=0, keepdims=True)
        kw = ks * jnp.exp(rc - rmax)
        fc_row = fc_rows[c:c + 1]
        inter = fc_row + m_row
        intra = fc_row + rmax_t
        m_t = jnp.maximum(inter, intra)
        w_inter, w_intra = jnp.exp(inter - m_t), jnp.exp(intra - m_t)
        den = w_inter * qn + w_intra * row_sum
        inv = 1.0 / jnp.maximum(jnp.abs(den), jnp.exp(-m_t))
        full = _dot_dims(p_t * (w_intra * inv), v, TN_DIMS)
        yield
        out = _expand3(jnp.where(eye_t, w_inter * inv, 0.0), e_h) * _dot(q, ct)
        for h in range(MLSTM_HEADS):
            out = out + jnp.where(head_lane[h], full[h * CHUNK:(h + 1) * CHUNK], 0.0)
        h_ref[0, rows, :] = out
        m_max = jnp.maximum(m_row, rmax)
        keep, add = jnp.exp(m_row - m_max), jnp.exp(rmax - m_max)
        ct = keep * ct + add * jnp.where(block_diag, _dot_dims(kw, v, TN_DIMS), 0.0)
        n_row = keep * n_row + add * jnp.sum(kw, axis=0, keepdims=True)
        m_row = f_tot_rows[c:c + 1] + m_max
        yield
    ct_ref[d], n_ref[d], m_ref[d] = ct, n_row, m_row


def _mlstm_kernel(qf_ref, kf_ref, vf_ref, gcf_ref, grf_ref, qb_ref, kb_ref, vb_ref, gcb_ref, grb_ref,
                  bcol_ref, brow_ref, tbd_ref, tbdt_ref, er_ref, ef_ref, eh_ref,
                  hf_ref, hb_ref, ct_ref, n_ref, m_ref):
    @pl.when(pl.program_id(1) == 0)
    def _():
        ct_ref[...] = jnp.zeros(ct_ref.shape, F32)
        n_ref[...] = jnp.zeros(n_ref.shape, F32)
        m_ref[...] = jnp.zeros(m_ref.shape, F32)

    _interleave(
        _mlstm_direction(qf_ref, kf_ref, vf_ref, gcf_ref[0] + bcol_ref[...], grf_ref[0, 0, 0] + brow_ref[0],
                         hf_ref, tbd_ref[0], tbdt_ref[0], er_ref[0], ef_ref[0], eh_ref[...],
                         ct_ref, n_ref, m_ref, 0),
        _mlstm_direction(qb_ref, kb_ref, vb_ref, gcb_ref[0] + bcol_ref[...], grb_ref[0, 0, 0] + brow_ref[1],
                         hb_ref, tbd_ref[1], tbdt_ref[1], er_ref[1], ef_ref[1], eh_ref[...],
                         ct_ref, n_ref, m_ref, 1))


def _mlstm(mls_proj, gate_proj, f_rows, bcol, brow, tbd, tbd_t, e_r, e_f, e_h):
    b, s, _ = mls_proj.shape
    nb = s // REC_BLOCK
    fwd = lambda col: pl.BlockSpec((1, REC_BLOCK, MLSTM_W), lambda bi, j: (bi, j, col))
    bwd = lambda col: pl.BlockSpec((1, REC_BLOCK, MLSTM_W), lambda bi, j: (bi, nb - 1 - j, col))
    const = lambda shape: pl.BlockSpec(shape, lambda bi, j: (0,) * len(shape))
    gcol_f = pl.BlockSpec((1, REC_BLOCK, GATE_W), lambda bi, j: (bi, j, 0))
    gcol_b = pl.BlockSpec((1, REC_BLOCK, GATE_W), lambda bi, j: (bi, nb - 1 - j, 0))
    frow_shape = (1, 1, 1) + f_rows.shape[3:]
    frow_f = pl.BlockSpec(frow_shape, lambda bi, j: (bi, 0, j, 0, 0))
    frow_b = pl.BlockSpec(frow_shape, lambda bi, j: (bi, 1, nb - 1 - j, 0, 0))
    out_shape = jax.ShapeDtypeStruct((b, s, MLSTM_W), F32)
    return pl.pallas_call(
        _mlstm_kernel,
        grid=(b, nb),
        in_specs=[fwd(0), fwd(1), fwd(2), gcol_f, frow_f, bwd(0), bwd(1), bwd(2), gcol_b, frow_b,
                  const(bcol.shape), const(brow.shape), const(tbd.shape), const(tbd_t.shape),
                  const(e_r.shape), const(e_f.shape), const(e_h.shape)],
        out_specs=[pl.BlockSpec((1, REC_BLOCK, MLSTM_W), lambda bi, j: (bi, j, 0)),
                   pl.BlockSpec((1, REC_BLOCK, MLSTM_W), lambda bi, j: (bi, nb - 1 - j, 0))],
        out_shape=[out_shape, out_shape],
        scratch_shapes=[pltpu.VMEM((2, MLSTM_W, MLSTM_W), F32),
                        pltpu.VMEM((2, 1, MLSTM_W), F32),
                        pltpu.VMEM((2, 1, MLSTM_W), F32)],
        compiler_params=_params("parallel", "arbitrary"),
        name="mlstm",
    )(mls_proj, mls_proj, mls_proj, gate_proj, f_rows,
      mls_proj, mls_proj, mls_proj, gate_proj, f_rows, bcol, brow, tbd, tbd_t, e_r, e_f, e_h)


def _outproj_kernel(x_ref, attn_ref, gof_ref, gob_ref, gg_ref, mhf_ref, mhb_ref, mo_ref,
                    gng_ref, mng_ref, bd_ref, wo_ref, out_ref):
    bd = bd_ref[...]

    def head_norm(t, gain):
        return t * lax.rsqrt(_group_mean(t * t, bd) + EPS) * gain

    gg = gg_ref[0]
    gla = head_norm(gof_ref[0] + gob_ref[0], gng_ref[...]) * (gg * _sigmoid(gg))
    mls = head_norm(_sigmoid(mo_ref[0]) * (mhf_ref[0] + mhb_ref[0]), mng_ref[...])
    acc = jnp.dot(attn_ref[0], wo_ref[:ATTN_W, :], preferred_element_type=F32)
    acc += jnp.dot(gla.astype(BF16), wo_ref[ATTN_W:ATTN_W + GLA_W, :], preferred_element_type=F32)
    acc += jnp.dot(mls.astype(BF16), wo_ref[ATTN_W + GLA_W:, :], preferred_element_type=F32)
    out_ref[0] = x_ref[0] + acc


def _outproj(x, attn, gof, gob, gla_proj, mhf, mhb, mls_proj, gng, mng, bd, wo):
    b, s, d = x.shape
    row = lambda n, col=0: pl.BlockSpec((1, ROW_TILE, n), lambda bi, i: (bi, i, col))
    const = lambda shape: pl.BlockSpec(shape, lambda bi, i: (0,) * len(shape))
    return pl.pallas_call(
        _outproj_kernel,
        grid=(b, s // ROW_TILE),
        in_specs=[row(d), row(ATTN_W), row(GLA_W), row(GLA_W), row(GLA_W, 3),
                  row(MLSTM_W), row(MLSTM_W), row(MLSTM_W, 3),
                  const((1, GLA_W)), const((1, MLSTM_W)), const(bd.shape), const(wo.shape)],
        out_specs=row(d),
        out_shape=jax.ShapeDtypeStruct((b, s, d), F32),
        compiler_params=_params("parallel", "parallel"),
        name="outproj",
    )(x, attn, gof, gob, gla_proj, mhf, mhb, mls_proj, gng, mng, bd, wo)


def _ffn_kernel(x_ref, prev_ref, next_ref, g_ref, wg_ref, wv_ref, cwg_ref, cwv_ref, cbg_ref, cbv_ref,
                wd_ref, out_ref, h_ref, ug_ref, uv_ref, acc_ref):
    i, j = pl.program_id(1), pl.program_id(2)
    tm = x_ref.shape[1]

    def normed(t):
        ms = jnp.mean(t * t, axis=-1, keepdims=True)
        return t * lax.rsqrt(ms + EPS) * g_ref[...]

    @pl.when(j == 0)
    def _():
        keep_prev = (i > 0).astype(F32)
        keep_next = (i < pl.num_programs(1) - 1).astype(F32)
        h_ref[0:HALO, :] = (normed(prev_ref[0]) * keep_prev).astype(BF16)
        h_ref[HALO:HALO + tm, :] = normed(x_ref[0]).astype(BF16)
        h_ref[HALO + tm:, :] = (normed(next_ref[0]) * keep_next).astype(BF16)
        acc_ref[...] = jnp.zeros(acc_ref.shape, F32)

    h = h_ref[...]
    ug_ref[...] = jnp.dot(h, wg_ref[...], preferred_element_type=F32)
    uv_ref[...] = jnp.dot(h, wv_ref[...], preferred_element_type=F32)

    def conv(u_ref, cw_ref, cb_ref):
        out = cb_ref[...] + u_ref[HALO - 1:HALO - 1 + tm, :] * cw_ref[0:1, :]
        for t in range(1, CONV_WIDTH):
            out = out + u_ref[HALO - 1 + t:HALO - 1 + t + tm, :] * cw_ref[t:t + 1, :]
        return out

    gate = conv(ug_ref, cwg_ref, cbg_ref)
    val = conv(uv_ref, cwv_ref, cbv_ref)
    act = (gate * _sigmoid(gate) * val).astype(BF16)
    acc_ref[...] += jnp.dot(act, wd_ref[...], preferred_element_type=F32)

    @pl.when(j == pl.num_programs(2) - 1)
    def _():
        out_ref[0] = x_ref[0] + acc_ref[...]


def _ffn(x, g, w_up, conv_w, conv_b, w_down):
    b, s, d = x.shape
    nj = D_FF // FF_BLOCK
    tiles_per_halo = ROW_TILE // HALO
    last_halo = s // HALO - 1
    row = pl.BlockSpec((1, ROW_TILE, d), lambda bi, i, j: (bi, i, 0))
    prev = pl.BlockSpec((1, HALO, d), lambda bi, i, j: (bi, jnp.maximum(i * tiles_per_halo - 1, 0), 0))
    nxt = pl.BlockSpec((1, HALO, d),
                       lambda bi, i, j: (bi, jnp.minimum((i + 1) * tiles_per_halo, last_halo), 0))
    gate_cols = lambda rows: pl.BlockSpec((rows, FF_BLOCK), lambda bi, i, j: (0, j))
    val_cols = lambda rows: pl.BlockSpec((rows, FF_BLOCK), lambda bi, i, j: (0, nj + j))
    return pl.pallas_call(
        _ffn_kernel,
        grid=(b, s // ROW_TILE, nj),
        in_specs=[row, prev, nxt, pl.BlockSpec((1, d), lambda bi, i, j: (0, 0)),
                  gate_cols(d), val_cols(d), gate_cols(CONV_WIDTH), val_cols(CONV_WIDTH),
                  gate_cols(1), val_cols(1),
                  pl.BlockSpec((FF_BLOCK, d), lambda bi, i, j: (j, 0))],
        out_specs=row,
        out_shape=jax.ShapeDtypeStruct((b, s, d), F32),
        scratch_shapes=[pltpu.VMEM((ROW_TILE + 2 * HALO, d), BF16),
                        pltpu.VMEM((ROW_TILE + 2 * HALO, FF_BLOCK), F32),
                        pltpu.VMEM((ROW_TILE + 2 * HALO, FF_BLOCK), F32),
                        pltpu.VMEM((ROW_TILE, d), F32)],
        compiler_params=_params("parallel", "parallel", "arbitrary"),
        name="ffn",
    )(x, x, x, g, w_up, w_up, conv_w, conv_w, conv_b, conv_b, w_down)


def _rope_tables(seq):
    n_rows = seq // GRID_W
    row = jnp.repeat(jnp.arange(n_rows, dtype=F32), GRID_W)
    col = jnp.tile(jnp.arange(GRID_W, dtype=F32), n_rows)
    n_freq = HEAD_DIM // 4
    inv_freq = jnp.power(ROPE_THETA, -jnp.arange(n_freq, dtype=F32) / n_freq)
    ang_r, ang_c = row[:, None] * inv_freq, col[:, None] * inv_freq
    cos_h = jnp.concatenate([jnp.cos(ang_r), jnp.cos(ang_r), jnp.cos(ang_c), jnp.cos(ang_c)], axis=1)
    sin_h = jnp.concatenate([-jnp.sin(ang_r), jnp.sin(ang_r), -jnp.sin(ang_c), jnp.sin(ang_c)], axis=1)
    reps = QK_W // HEAD_DIM
    return jnp.tile(cos_h, (1, reps)), jnp.tile(sin_h, (1, reps))


def _block_diag_mean(width):
    group = np.arange(width) // HEAD_DIM
    return jnp.asarray((group[:, None] == group[None, :]).astype(np.float32) / HEAD_DIM, dtype=BF16)


def _tri_constants():
    t = np.arange(REC_BLOCK)
    same_chunk = (t[:, None] // CHUNK) == (t[None, :] // CHUNK)
    prefix = (same_chunk & (t[None, :] <= t[:, None])).astype(np.float32)
    suffix = (same_chunk & (t[None, :] >= t[:, None])).astype(np.float32)
    tbd = jnp.asarray(np.stack([prefix, suffix]), dtype=BF16)
    tbd_t = jnp.asarray(np.stack([prefix.T, suffix.T]), dtype=BF16)
    return tbd, tbd_t


def _mlstm_expanders():
    e_r = np.zeros((2, GATE_W, MLSTM_W), np.float32)
    e_f = np.zeros((2, GATE_W, MLSTM_W), np.float32)
    for d in range(2):
        for h in range(MLSTM_HEADS):
            lanes = slice(h * HEAD_DIM, (h + 1) * HEAD_DIM)
            i_lane = 2 * GLA_RANK + d * MLSTM_HEADS + h
            f_lane = i_lane + 2 * MLSTM_HEADS
            e_r[d, i_lane, lanes] = 1.0
            e_r[d, f_lane, lanes] = -1.0
            e_f[d, f_lane, lanes] = 1.0
    group = np.arange(MLSTM_W) // HEAD_DIM
    e_h = (group[:, None] == group[None, :]).astype(np.float32)
    return jnp.asarray(e_r, dtype=BF16), jnp.asarray(e_f, dtype=BF16), jnp.asarray(e_h, dtype=BF16)


def _layer(x, consts, norm_mix_g, w_in, attn_qn_g, attn_kn_g, gla_gate_w2, gla_gate_b, gla_norm_g,
           mlstm_gate_b, mlstm_norm_g, w_out, norm_ffn_g, w_up, conv_w, conv_b, w_down):
    cos_t, sin_t, bd_qk, bd_head, tbd, tbd_t, e_r, e_f, e_h = consts
    b, s, _ = x.shape
    n_gates = 4 * MLSTM_HEADS
    main_w = QKV_W + 4 * GLA_W
    gla_a = w_in[:, main_w:main_w + 2 * GLA_RANK]
    mls_main = w_in[:, main_w + 2 * GLA_RANK:main_w + 2 * GLA_RANK + 4 * MLSTM_W]
    mls_g = w_in[:, main_w + 2 * GLA_RANK + 4 * MLSTM_W:]
    pad = jnp.zeros((D_MODEL, GATE_W - 2 * GLA_RANK - n_gates), F32)
    w_in_r = jnp.concatenate([w_in[:, :main_w], mls_main, gla_a, mls_g, pad], axis=1).astype(BF16)

    qkv, gla_proj, mls_proj, gate_proj = _inproj(x, norm_mix_g[None, :], w_in_r)

    qk_gain = jnp.concatenate([jnp.tile(attn_qn_g, ATTN_HEADS), jnp.tile(attn_kn_g, ATTN_KV_HEADS)])[None, :]
    q_t, k, v_t = _qkprep(qkv, cos_t, sin_t, qk_gain, bd_qk)
    attn = _attention(q_t, k, v_t)

    w2ext = jnp.zeros((2, GATE_W, GLA_W), F32)
    w2ext = w2ext.at[0, :GLA_RANK].set(gla_gate_w2[0]).at[1, GLA_RANK:2 * GLA_RANK].set(gla_gate_w2[1])
    w2_hi = w2ext.astype(BF16)
    w2_lo = (w2ext - w2_hi.astype(F32)).astype(BF16)
    cum_f, cum_b = _glagate(gate_proj, w2_hi, w2_lo, gla_gate_b[:, None, :], tbd)
    gof, gob = _gla(gla_proj, cum_f, cum_b)

    nb, cpb = s // REC_BLOCK, REC_BLOCK // CHUNK
    f_lo = 2 * GLA_RANK + 2 * MLSTM_HEADS
    f_rows = gate_proj[:, :, f_lo:f_lo + 2 * MLSTM_HEADS].reshape(b, nb, cpb, CHUNK, 2, MLSTM_HEADS)
    f_rows = f_rows.transpose(0, 4, 1, 2, 5, 3).reshape(b, 2, nb, cpb, MLSTM_W)
    f_rows = jnp.pad(f_rows, ((0, 0), (0, 0), (0, 0), (0, SUBLANE - cpb), (0, 0)))
    bcol = jnp.zeros((1, GATE_W), F32).at[0, 2 * GLA_RANK:2 * GLA_RANK + n_gates].set(
        mlstm_gate_b.reshape(n_gates))
    brow = jnp.repeat(mlstm_gate_b[2:], HEAD_DIM, axis=1)[:, None, :]
    mhf, mhb = _mlstm(mls_proj, gate_proj, f_rows, bcol, brow, tbd, tbd_t, e_r, e_f, e_h)

    x = _outproj(x, attn, gof, gob, gla_proj, mhf, mhb, mls_proj,
                 jnp.tile(gla_norm_g, GLA_HEADS)[None, :], jnp.tile(mlstm_norm_g, MLSTM_HEADS)[None, :],
                 bd_head, w_out.astype(BF16))
    return _ffn(x, norm_ffn_g[None, :], w_up.astype(BF16), conv_w, conv_b[None, :], w_down.astype(BF16))


def _trunk(x, weights):
    seq = x.shape[1]
    cos_t, sin_t = _rope_tables(seq)
    consts = ((cos_t, sin_t, _block_diag_mean(QK_W), _block_diag_mean(GLA_W))
              + _tri_constants() + _mlstm_expanders())
    depth = weights[0].shape[0]
    for l in range(depth):
        x = _layer(x, consts, *[w[l] for w in weights])
    return x


def kernel(x_prompt, x_sample, norm_mix_g, w_in, attn_qn_g, attn_kn_g, gla_gate_w2, gla_gate_b,
           gla_norm_g, mlstm_gate_b, mlstm_norm_g, w_out, norm_ffn_g, w_up, conv_w, conv_b, w_down):
    weights = (norm_mix_g, w_in, attn_qn_g, attn_kn_g, gla_gate_w2, gla_gate_b, gla_norm_g,
               mlstm_gate_b, mlstm_norm_g, w_out, norm_ffn_g, w_up, conv_w, conv_b, w_down)
    assert x_prompt.shape[1:] == x_sample.shape[1:]
    n_prompt = x_prompt.shape[0]
    y = _trunk(jnp.concatenate([x_prompt, x_sample], axis=0), weights)
    return y[:n_prompt], y[n_prompt:]
```

```python
import functools

import jax
import jax.numpy as jnp
import numpy as np
from jax import lax
from jax.experimental import pallas as pl
from jax.experimental.pallas import tpu as pltpu

F32 = jnp.float32
BF16 = jnp.bfloat16

D_MODEL = 1024
GRID_W = 64
HEAD_DIM = 64
EPS = 1e-6
ATTN_HEADS = 8
ATTN_KV_HEADS = 2
ATTN_GROUP = ATTN_HEADS // ATTN_KV_HEADS
ROPE_THETA = 10000.0
GLA_HEADS = 4
GLA_RANK = 16
GLA_TAU = 16.0
MLSTM_HEADS = 4
CHUNK = 64
ATTN_W = ATTN_HEADS * HEAD_DIM
KV_W = ATTN_KV_HEADS * HEAD_DIM
GLA_W = GLA_HEADS * HEAD_DIM
MLSTM_W = MLSTM_HEADS * HEAD_DIM
QK_W = ATTN_W + KV_W
QKV_W = ATTN_W + 2 * KV_W
D_FF = 2816
CONV_WIDTH = 3
GATE_W = 128
Q_SCALE = HEAD_DIM ** -0.5
LOG2_E = 1.4426950408889634

LANE = 128
SUBLANE = 8
VMEM_LIMIT = 56 * 1024 * 1024

ROW_TILE = 512
REC_BLOCK = 256
ATTN_TQ = 256
ATTN_STREAM_LANES = 256
ATTN_TK = 512
VT_ROWS = HEAD_DIM + 16
FF_BLOCK = 1408
HALO = SUBLANE

NT_DIMS = (((1,), (1,)), ((), ()))
TN_DIMS = (((0,), (0,)), ((), ()))


def _params(*sem):
    return pltpu.CompilerParams(dimension_semantics=sem, vmem_limit_bytes=VMEM_LIMIT)


def _log_sigmoid(z):
    return jnp.minimum(z, 0.0) - jnp.log(1.0 + jnp.exp(-jnp.abs(z)))


def _sigmoid(z):
    return 1.0 / (1.0 + jnp.exp(-z))


def _dot(a, b):
    return jnp.dot(a.astype(BF16), b.astype(BF16), preferred_element_type=F32)


def _dot_dims(a, b, dims):
    return lax.dot_general(a.astype(BF16), b.astype(BF16), dims, preferred_element_type=F32)


def _split3(x):
    p1 = x.astype(BF16)
    rest = x - p1.astype(F32)
    p2 = rest.astype(BF16)
    return p1, p2, (rest - p2.astype(F32)).astype(BF16)


def _cumsum_cols(tbd, x):
    return sum(jnp.dot(tbd, p, preferred_element_type=F32) for p in _split3(x))


def _cumsum_rows(x, tbd_t):
    return sum(jnp.dot(p, tbd_t, preferred_element_type=F32) for p in _split3(x))


def _dot_split(a, b_hi, b_lo):
    a_hi = a.astype(BF16)
    a_lo = (a - a_hi.astype(F32)).astype(BF16)
    return (jnp.dot(a_hi, b_hi, preferred_element_type=F32)
            + jnp.dot(a_lo, b_hi, preferred_element_type=F32)
            + jnp.dot(a_hi, b_lo, preferred_element_type=F32))


def _group_mean(sq, bd):
    hi = sq.astype(BF16)
    lo = (sq - hi.astype(F32)).astype(BF16)
    return (jnp.dot(hi, bd, preferred_element_type=F32)
            + jnp.dot(lo, bd, preferred_element_type=F32))


def _inproj_kernel(x_ref, g_ref, w_ref, qkv_ref, gla_ref, mls_ref, gate_ref):
    x = x_ref[0]
    ms = jnp.mean(x * x, axis=-1, keepdims=True)
    h = (x * lax.rsqrt(ms + EPS) * g_ref[...]).astype(BF16)
    col = 0
    for ref in (qkv_ref, gla_ref, mls_ref, gate_ref):
        width = ref.shape[-1]
        ref[0] = jnp.dot(h, w_ref[:, col:col + width], preferred_element_type=F32)
        col += width


def _inproj(x, g, w):
    b, s, d = x.shape
    widths = (QKV_W, 4 * GLA_W, 4 * MLSTM_W, GATE_W)
    row = lambda n: pl.BlockSpec((1, ROW_TILE, n), lambda bi, i: (bi, i, 0))
    const = lambda shape: pl.BlockSpec(shape, lambda bi, i: (0,) * len(shape))
    return pl.pallas_call(
        _inproj_kernel,
        grid=(b, s // ROW_TILE),
        in_specs=[row(d), const((1, d)), const(w.shape)],
        out_specs=[row(n) for n in widths],
        out_shape=[jax.ShapeDtypeStruct((b, s, n), F32) for n in widths],
        compiler_params=_params("parallel", "parallel"),
        name="inproj",
    )(x, g, w)


def _qkprep_kernel(qkv_ref, cos_ref, sin_ref, gain_ref, bd_ref, qt_ref, k_ref, vt_ref):
    x = qkv_ref[0]
    qk = x[:, :QK_W]
    ms = _group_mean(qk * qk, bd_ref[...])
    y = qk * lax.rsqrt(ms + EPS) * gain_ref[...]
    lane = lax.broadcasted_iota(jnp.int32, y.shape, 1)
    first_half = (lane & 31) < 16
    quarter = HEAD_DIM // 4
    swapped = jnp.where(first_half, pltpu.roll(y, QK_W - quarter, 1), pltpu.roll(y, quarter, 1))
    r = y * cos_ref[...] + swapped * sin_ref[...]
    q_t = (r[:, :ATTN_W] * (Q_SCALE * LOG2_E)).T.astype(BF16)
    qt_ref[0] = q_t.reshape(ATTN_HEADS, HEAD_DIM, q_t.shape[-1])
    v_t = x[:, QK_W:].T.astype(BF16)
    v_t = v_t.reshape(ATTN_KV_HEADS, HEAD_DIM, v_t.shape[-1])
    row = lax.broadcasted_iota(jnp.int32, (ATTN_KV_HEADS, VT_ROWS - HEAD_DIM, v_t.shape[-1]), 1)
    vt_ref[0] = jnp.concatenate([v_t, jnp.where(row == 0, 1.0, 0.0).astype(BF16)], axis=1)
    for h in range(ATTN_KV_HEADS):
        k_ref[0, h] = r[:, ATTN_W + h * HEAD_DIM:ATTN_W + (h + 1) * HEAD_DIM].astype(BF16)


def _qkprep(qkv, cos_t, sin_t, gain, bd):
    b, s, _ = qkv.shape
    const = lambda shape: pl.BlockSpec(shape, lambda bi, i: (0,) * len(shape))
    tab = pl.BlockSpec((ROW_TILE, QK_W), lambda bi, i: (i, 0))
    t_spec = lambda heads, rows: pl.BlockSpec((1, heads, rows, ROW_TILE), lambda bi, i: (bi, 0, 0, i))
    t_shape = lambda heads, rows: jax.ShapeDtypeStruct((b, heads, rows, s), BF16)
    return pl.pallas_call(
        _qkprep_kernel,
        grid=(b, s // ROW_TILE),
        in_specs=[pl.BlockSpec((1, ROW_TILE, QKV_W), lambda bi, i: (bi, i, 0)),
                  tab, tab, const((1, QK_W)), const((QK_W, QK_W))],
        out_specs=[t_spec(ATTN_HEADS, HEAD_DIM),
                   pl.BlockSpec((1, ATTN_KV_HEADS, ROW_TILE, HEAD_DIM), lambda bi, i: (bi, 0, i, 0)),
                   t_spec(ATTN_KV_HEADS, VT_ROWS)],
        out_shape=[t_shape(ATTN_HEADS, HEAD_DIM),
                   jax.ShapeDtypeStruct((b, ATTN_KV_HEADS, s, HEAD_DIM), BF16),
                   t_shape(ATTN_KV_HEADS, VT_ROWS)],
        compiler_params=_params("parallel", "parallel"),
        name="qkprep",
    )(qkv, cos_t, sin_t, gain, bd)


def _attn_kernel(qt_ref, k_ref, vt_ref, o_ref):
    tq = qt_ref.shape[-1]
    n_chunks = k_ref.shape[2] // ATTN_TK
    keys = lambda c: slice(c * ATTN_TK, (c + 1) * ATTN_TK)

    def stream(heads):
        q_t = jnp.concatenate([qt_ref[0, j] for j in heads], axis=1)
        m = jnp.full((1, q_t.shape[1]), -jnp.inf, F32)
        acc = jnp.zeros((VT_ROWS, q_t.shape[1]), F32)
        s_next = jnp.dot(k_ref[0, 0, keys(0), :], q_t, preferred_element_type=F32)
        yield
        for c in range(n_chunks):
            s_t = s_next
            if c + 1 < n_chunks:
                s_next = jnp.dot(k_ref[0, 0, keys(c + 1), :], q_t, preferred_element_type=F32)
            m_new = jnp.maximum(m, jnp.max(s_t, axis=0, keepdims=True))
            yield
            p_t = jnp.exp2((s_t - m_new).astype(BF16))
            yield
            acc = (jnp.exp2(m - m_new) * acc
                   + jnp.dot(vt_ref[0, 0, :, keys(c)], p_t, preferred_element_type=F32))
            m = m_new
            yield
        out = (acc[:HEAD_DIM] / acc[HEAD_DIM:HEAD_DIM + 1]).T
        for n, j in enumerate(heads):
            o_ref[0, :, j * HEAD_DIM:(j + 1) * HEAD_DIM] = out[n * tq:(n + 1) * tq, :].astype(o_ref.dtype)

    per_stream = ATTN_STREAM_LANES // tq
    _interleave(*[stream(range(j, j + per_stream)) for j in range(0, ATTN_GROUP, per_stream)])


def _attention(q_t, k, v_t):
    b, _, _, s = q_t.shape
    gw = ATTN_GROUP * HEAD_DIM
    return pl.pallas_call(
        _attn_kernel,
        grid=(b, ATTN_KV_HEADS, s // ATTN_TQ),
        in_specs=[pl.BlockSpec((1, ATTN_GROUP, HEAD_DIM, ATTN_TQ), lambda bi, h, i: (bi, h, 0, i)),
                  pl.BlockSpec((1, 1, s, HEAD_DIM), lambda bi, h, i: (bi, h, 0, 0)),
                  pl.BlockSpec((1, 1, VT_ROWS, s), lambda bi, h, i: (bi, h, 0, 0))],
        out_specs=pl.BlockSpec((1, ATTN_TQ, gw), lambda bi, h, i: (bi, i, h)),
        out_shape=jax.ShapeDtypeStruct((b, s, ATTN_W), BF16),
        compiler_params=_params("parallel", "parallel", "arbitrary"),
        name="attention",
    )(q_t, k, v_t)


def _chunk_order(d):
    n_chunks = REC_BLOCK // CHUNK
    return range(n_chunks) if d == 0 else range(n_chunks - 1, -1, -1)


def _glagate_kernel(a_ref, w2hi_ref, w2lo_ref, gb_ref, tbd_ref, cf_ref, cb_ref):
    a = a_ref[0]
    for d, out_ref in enumerate((cf_ref, cb_ref)):
        z = _dot_split(a, w2hi_ref[d], w2lo_ref[d]) + gb_ref[d]
        g = _log_sigmoid(z) * (1.0 / GLA_TAU)
        for sb in range(a.shape[0] // REC_BLOCK):
            rows = slice(sb * REC_BLOCK, (sb + 1) * REC_BLOCK)
            out_ref[0, rows, :] = _cumsum_cols(tbd_ref[d], g[rows])


def _glagate(gate_proj, w2_hi, w2_lo, gb, tbd):
    b, s, _ = gate_proj.shape
    const = lambda shape: pl.BlockSpec(shape, lambda bi, i: (0,) * len(shape))
    out_spec = pl.BlockSpec((1, ROW_TILE, GLA_W), lambda bi, i: (bi, i, 0))
    out_shape = jax.ShapeDtypeStruct((b, s, GLA_W), F32)
    return pl.pallas_call(
        _glagate_kernel,
        grid=(b, s // ROW_TILE),
        in_specs=[pl.BlockSpec((1, ROW_TILE, GATE_W), lambda bi, i: (bi, i, 0)),
                  const(w2_hi.shape), const(w2_lo.shape), const(gb.shape), const(tbd.shape)],
        out_specs=[out_spec, out_spec],
        out_shape=[out_shape, out_shape],
        compiler_params=_params("parallel", "parallel"),
        name="glagate",
    )(gate_proj, w2_hi, w2_lo, gb, tbd)


def _gla_direction(q_ref, k_ref, v_ref, cum_ref, o_ref, st_ref, d):
    causal_t, _, head_lane, block_diag = _head_layout_masks(d)
    last = CHUNK - 1 if d == 0 else 0
    st = st_ref[d]
    for i in _chunk_order(d):
        rows = slice(i * CHUNK, (i + 1) * CHUNK)
        cum = cum_ref[0, rows, :]
        total = cum[last:last + 1]
        ref = cum[CHUNK // 2:CHUNK // 2 + 1]
        qt = q_ref[0, rows, :] * Q_SCALE * jnp.exp(cum - ref)
        kt = k_ref[0, rows, :] * jnp.exp(ref - cum)
        qe = qt * jnp.exp(ref)
        kd = kt * jnp.exp(total - ref)
        v = v_ref[0, rows, :]
        q4 = jnp.concatenate([jnp.where(head_lane[h], qt, 0.0) for h in range(GLA_HEADS)], axis=0)
        att_t = jnp.where(causal_t, _dot_dims(kt, q4, NT_DIMS), 0.0)
        yield
        full = _dot_dims(att_t, v, TN_DIMS)
        out = _dot_dims(qe, st, NT_DIMS)
        yield
        for h in range(GLA_HEADS):
            out = out + jnp.where(head_lane[h], full[h * CHUNK:(h + 1) * CHUNK], 0.0)
        o_ref[0, rows, :] = out
        st = st * jnp.exp(total) + jnp.where(block_diag, _dot_dims(v, kd, TN_DIMS), 0.0)
        yield
    st_ref[d] = st


def _interleave(*stages):
    stages = list(stages)
    while stages:
        for g in list(stages):
            if next(g, StopIteration) is StopIteration:
                stages.remove(g)


def _gla_kernel(qf_ref, kf_ref, vf_ref, cf_ref, qb_ref, kb_ref, vb_ref, cb_ref, of_ref, ob_ref, st_ref):
    @pl.when(pl.program_id(1) == 0)
    def _():
        st_ref[...] = jnp.zeros(st_ref.shape, F32)

    _interleave(_gla_direction(qf_ref, kf_ref, vf_ref, cf_ref, of_ref, st_ref, 0),
                _gla_direction(qb_ref, kb_ref, vb_ref, cb_ref, ob_ref, st_ref, 1))


def _gla(gla_proj, cum_f, cum_b):
    b, s, _ = gla_proj.shape
    nb = s // REC_BLOCK
    fwd = lambda col: pl.BlockSpec((1, REC_BLOCK, GLA_W), lambda bi, j: (bi, j, col))
    bwd = lambda col: pl.BlockSpec((1, REC_BLOCK, GLA_W), lambda bi, j: (bi, nb - 1 - j, col))
    out_shape = jax.ShapeDtypeStruct((b, s, GLA_W), F32)
    return pl.pallas_call(
        _gla_kernel,
        grid=(b, nb),
        in_specs=[fwd(0), fwd(1), fwd(2), fwd(0), bwd(0), bwd(1), bwd(2), bwd(0)],
        out_specs=[fwd(0), bwd(0)],
        out_shape=[out_shape, out_shape],
        scratch_shapes=[pltpu.VMEM((2, GLA_W, GLA_W), F32)],
        compiler_params=_params("parallel", "arbitrary"),
        name="gla",
    )(gla_proj, gla_proj, gla_proj, cum_f, gla_proj, gla_proj, gla_proj, cum_b)


def _head_layout_masks(d):
    sub = lax.broadcasted_iota(jnp.int32, (CHUNK, MLSTM_W), 0)
    lane = lax.broadcasted_iota(jnp.int32, (CHUNK, MLSTM_W), 1)
    t_lane = lane & (HEAD_DIM - 1)
    causal_t = (sub <= t_lane) if d == 0 else (sub >= t_lane)
    eye_t = sub == t_lane
    head_lane = [(lane >> 6) == h for h in range(MLSTM_HEADS)]
    brow = lax.broadcasted_iota(jnp.int32, (MLSTM_W, MLSTM_W), 0) >> 6
    bcol = lax.broadcasted_iota(jnp.int32, (MLSTM_W, MLSTM_W), 1) >> 6
    return causal_t, eye_t, head_lane, brow == bcol


def _expand3(x, e):
    return sum(jnp.dot(p, e, preferred_element_type=F32) for p in _split3(x))


def _mlstm_direction(q_ref, k_ref, v_ref, gates, frow, h_ref, tbd, tbd_t, e_r, e_f, e_h,
                     ct_ref, n_ref, m_ref, d):
    causal_t, eye_t, head_lane, block_diag = _head_layout_masks(d)
    last = CHUNK - 1 if d == 0 else 0
    n_chunks = REC_BLOCK // CHUNK
    lane = lax.broadcasted_iota(jnp.int32, gates.shape, 1)
    i_lo = 2 * GLA_RANK + d * MLSTM_HEADS
    is_i_lane = (lane >= i_lo) & (lane < i_lo + MLSTM_HEADS)
    fcum = _cumsum_cols(tbd, _log_sigmoid(gates))
    r_exp = _expand3(jnp.where(is_i_lane, gates, fcum), e_r)
    f_last = jnp.concatenate([fcum[c * CHUNK + last:c * CHUNK + last + 1] for c in range(n_chunks)]
                             + [jnp.zeros((SUBLANE - n_chunks, fcum.shape[1]), F32)], axis=0)
    f_tot_rows = _expand3(f_last, e_f)
    fc_rows = _cumsum_rows(_log_sigmoid(frow), tbd_t)

    ct, n_row, m_row = ct_ref[d], n_ref[d], m_ref[d]
    for c in _chunk_order(d):
        rows = slice(c * CHUNK, (c + 1) * CHUNK)
        q, ks, v = q_ref[0, rows, :], k_ref[0, rows, :] * Q_SCALE, v_ref[0, rows, :]
        q4 = jnp.concatenate([jnp.where(head_lane[h], q, 0.0) for h in range(MLSTM_HEADS)], axis=0)
        kn = jnp.concatenate([ks, jnp.broadcast_to(n_row, (2 * SUBLANE, MLSTM_W))], axis=0)
        z = _dot_dims(kn, q4, NT_DIMS)
        s_t, qn = z[:CHUNK], z[CHUNK:CHUNK + 1]
        yield
        rc = r_exp[rows]
        rmax_t = jnp.max(jnp.where(causal_t, rc, -jnp.inf), axis=0, keepdims=True)
        rmax = jnp.max(rc, axis=0, keepdims=True)
        p_t = s_t * jnp.exp(jnp.where(causal_t, rc - rmax_t, -jnp.inf))
        row_sum = jnp.sum(p_t, axis=0, keepdims=True)
        kw = ks * jnp.exp(rc - rmax)
        fc_row = fc_rows[c:c + 1]
        inter = fc_row + m_row
        intra = fc_row + rmax_t
        m_t = jnp.maximum(inter, intra)
        w_inter, w_intra = jnp.exp(inter - m_t), jnp.exp(intra - m_t)
        den = w_inter * qn + w_intra * row_sum
        inv = 1.0 / jnp.maximum(jnp.abs(den), jnp.exp(-m_t))
        full = _dot_dims(p_t * (w_intra * inv), v, TN_DIMS)
        yield
        out = _expand3(jnp.where(eye_t, w_inter * inv, 0.0), e_h) * _dot(q, ct)
        for h in range(MLSTM_HEADS):
            out = out + jnp.where(head_lane[h], full[h * CHUNK:(h + 1) * CHUNK], 0.0)
        h_ref[0, rows, :] = out
        m_max = jnp.maximum(m_row, rmax)
        keep, add = jnp.exp(m_row - m_max), jnp.exp(rmax - m_max)
        ct = keep * ct + add * jnp.where(block_diag, _dot_dims(kw, v, TN_DIMS), 0.0)
        n_row = keep * n_row + add * jnp.sum(kw, axis=0, keepdims=True)
        m_row = f_tot_rows[c:c + 1] + m_max
        yield
    ct_ref[d], n_ref[d], m_ref[d] = ct, n_row, m_row


def _mlstm_kernel(qf_ref, kf_ref, vf_ref, gcf_ref, grf_ref, qb_ref, kb_ref, vb_ref, gcb_ref, grb_ref,
                  bcol_ref, brow_ref, tbd_ref, tbdt_ref, er_ref, ef_ref, eh_ref,
                  hf_ref, hb_ref, ct_ref, n_ref, m_ref):
    @pl.when(pl.program_id(1) == 0)
    def _():
        ct_ref[...] = jnp.zeros(ct_ref.shape, F32)
        n_ref[...] = jnp.zeros(n_ref.shape, F32)
        m_ref[...] = jnp.zeros(m_ref.shape, F32)

    _interleave(
        _mlstm_direction(qf_ref, kf_ref, vf_ref, gcf_ref[0] + bcol_ref[...], grf_ref[0, 0, 0] + brow_ref[0],
                         hf_ref, tbd_ref[0], tbdt_ref[0], er_ref[0], ef_ref[0], eh_ref[...],
                         ct_ref, n_ref, m_ref, 0),
        _mlstm_direction(qb_ref, kb_ref, vb_ref, gcb_ref[0] + bcol_ref[...], grb_ref[0, 0, 0] + brow_ref[1],
                         hb_ref, tbd_ref[1], tbdt_ref[1], er_ref[1], ef_ref[1], eh_ref[...],
                         ct_ref, n_ref, m_ref, 1))


def _mlstm(mls_proj, gate_proj, f_rows, bcol, brow, tbd, tbd_t, e_r, e_f, e_h):
    b, s, _ = mls_proj.shape
    nb = s // REC_BLOCK
    fwd = lambda col: pl.BlockSpec((1, REC_BLOCK, MLSTM_W), lambda bi, j: (bi, j, col))
    bwd = lambda col: pl.BlockSpec((1, REC_BLOCK, MLSTM_W), lambda bi, j: (bi, nb - 1 - j, col))
    const = lambda shape: pl.BlockSpec(shape, lambda bi, j: (0,) * len(shape))
    gcol_f = pl.BlockSpec((1, REC_BLOCK, GATE_W), lambda bi, j: (bi, j, 0))
    gcol_b = pl.BlockSpec((1, REC_BLOCK, GATE_W), lambda bi, j: (bi, nb - 1 - j, 0))
    frow_shape = (1, 1, 1) + f_rows.shape[3:]
    frow_f = pl.BlockSpec(frow_shape, lambda bi, j: (bi, 0, j, 0, 0))
    frow_b = pl.BlockSpec(frow_shape, lambda bi, j: (bi, 1, nb - 1 - j, 0, 0))
    out_shape = jax.ShapeDtypeStruct((b, s, MLSTM_W), F32)
    return pl.pallas_call(
        _mlstm_kernel,
        grid=(b, nb),
        in_specs=[fwd(0), fwd(1), fwd(2), gcol_f, frow_f, bwd(0), bwd(1), bwd(2), gcol_b, frow_b,
                  const(bcol.shape), const(brow.shape), const(tbd.shape), const(tbd_t.shape),
                  const(e_r.shape), const(e_f.shape), const(e_h.shape)],
        out_specs=[pl.BlockSpec((1, REC_BLOCK, MLSTM_W), lambda bi, j: (bi, j, 0)),
                   pl.BlockSpec((1, REC_BLOCK, MLSTM_W), lambda bi, j: (bi, nb - 1 - j, 0))],
        out_shape=[out_shape, out_shape],
        scratch_shapes=[pltpu.VMEM((2, MLSTM_W, MLSTM_W), F32),
                        pltpu.VMEM((2, 1, MLSTM_W), F32),
                        pltpu.VMEM((2, 1, MLSTM_W), F32)],
        compiler_params=_params("parallel", "arbitrary"),
        name="mlstm",
    )(mls_proj, mls_proj, mls_proj, gate_proj, f_rows,
      mls_proj, mls_proj, mls_proj, gate_proj, f_rows, bcol, brow, tbd, tbd_t, e_r, e_f, e_h)


def _outproj_kernel(x_ref, attn_ref, gof_ref, gob_ref, gg_ref, mhf_ref, mhb_ref, mo_ref,
                    gng_ref, mng_ref, bd_ref, wo_ref, out_ref):
    bd = bd_ref[...]

    def head_norm(t, gain):
        return t * lax.rsqrt(_group_mean(t * t, bd) + EPS) * gain

    gg = gg_ref[0]
    gla = head_norm(gof_ref[0] + gob_ref[0], gng_ref[...]) * (gg * _sigmoid(gg))
    mls = head_norm(_sigmoid(mo_ref[0]) * (mhf_ref[0] + mhb_ref[0]), mng_ref[...])
    acc = jnp.dot(attn_ref[0], wo_ref[:ATTN_W, :], preferred_element_type=F32)
    acc += jnp.dot(gla.astype(BF16), wo_ref[ATTN_W:ATTN_W + GLA_W, :], preferred_element_type=F32)
    acc += jnp.dot(mls.astype(BF16), wo_ref[ATTN_W + GLA_W:, :], preferred_element_type=F32)
    out_ref[0] = x_ref[0] + acc


def _outproj(x, attn, gof, gob, gla_proj, mhf, mhb, mls_proj, gng, mng, bd, wo):
    b, s, d = x.shape
    row = lambda n, col=0: pl.BlockSpec((1, ROW_TILE, n), lambda bi, i: (bi, i, col))
    const = lambda shape: pl.BlockSpec(shape, lambda bi, i: (0,) * len(shape))
    return pl.pallas_call(
        _outproj_kernel,
        grid=(b, s // ROW_TILE),
        in_specs=[row(d), row(ATTN_W), row(GLA_W), row(GLA_W), row(GLA_W, 3),
                  row(MLSTM_W), row(MLSTM_W), row(MLSTM_W, 3),
                  const((1, GLA_W)), const((1, MLSTM_W)), const(bd.shape), const(wo.shape)],
        out_specs=row(d),
        out_shape=jax.ShapeDtypeStruct((b, s, d), F32),
        compiler_params=_params("parallel", "parallel"),
        name="outproj",
    )(x, attn, gof, gob, gla_proj, mhf, mhb, mls_proj, gng, mng, bd, wo)


def _ffn_kernel(x_ref, prev_ref, next_ref, g_ref, wg_ref, wv_ref, cwg_ref, cwv_ref, cbg_ref, cbv_ref,
                wd_ref, out_ref, h_ref, ug_ref, uv_ref, acc_ref):
    i, j = pl.program_id(1), pl.program_id(2)
    tm = x_ref.shape[1]

    def normed(t):
        ms = jnp.mean(t * t, axis=-1, keepdims=True)
        return t * lax.rsqrt(ms + EPS) * g_ref[...]

    @pl.when(j == 0)
    def _():
        keep_prev = (i > 0).astype(F32)
        keep_next = (i < pl.num_programs(1) - 1).astype(F32)
        h_ref[0:HALO, :] = (normed(prev_ref[0]) * keep_prev).astype(BF16)
        h_ref[HALO:HALO + tm, :] = normed(x_ref[0]).astype(BF16)
        h_ref[HALO + tm:, :] = (normed(next_ref[0]) * keep_next).astype(BF16)
        acc_ref[...] = jnp.zeros(acc_ref.shape, F32)

    h = h_ref[...]
    ug_ref[...] = jnp.dot(h, wg_ref[...], preferred_element_type=F32)
    uv_ref[...] = jnp.dot(h, wv_ref[...], preferred_element_type=F32)

    def conv(u_ref, cw_ref, cb_ref):
        out = cb_ref[...] + u_ref[HALO - 1:HALO - 1 + tm, :] * cw_ref[0:1, :]
        for t in range(1, CONV_WIDTH):
            out = out + u_ref[HALO - 1 + t:HALO - 1 + t + tm, :] * cw_ref[t:t + 1, :]
        return out

    gate = conv(ug_ref, cwg_ref, cbg_ref)
    val = conv(uv_ref, cwv_ref, cbv_ref)
    act = (gate * _sigmoid(gate) * val).astype(BF16)
    acc_ref[...] += jnp.dot(act, wd_ref[...], preferred_element_type=F32)

    @pl.when(j == pl.num_programs(2) - 1)
    def _():
        out_ref[0] = x_ref[0] + acc_ref[...]


def _ffn(x, g, w_up, conv_w, conv_b, w_down):
    b, s, d = x.shape
    nj = D_FF // FF_BLOCK
    tiles_per_halo = ROW_TILE // HALO
    last_halo = s // HALO - 1
    row = pl.BlockSpec((1, ROW_TILE, d), lambda bi, i, j: (bi, i, 0))
    prev = pl.BlockSpec((1, HALO, d), lambda bi, i, j: (bi, jnp.maximum(i * tiles_per_halo - 1, 0), 0))
    nxt = pl.BlockSpec((1, HALO, d),
                       lambda bi, i, j: (bi, jnp.minimum((i + 1) * tiles_per_halo, last_halo), 0))
    gate_cols = lambda rows: pl.BlockSpec((rows, FF_BLOCK), lambda bi, i, j: (0, j))
    val_cols = lambda rows: pl.BlockSpec((rows, FF_BLOCK), lambda bi, i, j: (0, nj + j))
    return pl.pallas_call(
        _ffn_kernel,
        grid=(b, s // ROW_TILE, nj),
        in_specs=[row, prev, nxt, pl.BlockSpec((1, d), lambda bi, i, j: (0, 0)),
                  gate_cols(d), val_cols(d), gate_cols(CONV_WIDTH), val_cols(CONV_WIDTH),
                  gate_cols(1), val_cols(1),
                  pl.BlockSpec((FF_BLOCK, d), lambda bi, i, j: (j, 0))],
        out_specs=row,
        out_shape=jax.ShapeDtypeStruct((b, s, d), F32),
        scratch_shapes=[pltpu.VMEM((ROW_TILE + 2 * HALO, d), BF16),
                        pltpu.VMEM((ROW_TILE + 2 * HALO, FF_BLOCK), F32),
                        pltpu.VMEM((ROW_TILE + 2 * HALO, FF_BLOCK), F32),
                        pltpu.VMEM((ROW_TILE, d), F32)],
        compiler_params=_params("parallel", "parallel", "arbitrary"),
        name="ffn",
    )(x, x, x, g, w_up, w_up, conv_w, conv_w, conv_b, conv_b, w_down)


def _rope_tables(seq):
    n_rows = seq // GRID_W
    row = jnp.repeat(jnp.arange(n_rows, dtype=F32), GRID_W)
    col = jnp.tile(jnp.arange(GRID_W, dtype=F32), n_rows)
    n_freq = HEAD_DIM // 4
    inv_freq = jnp.power(ROPE_THETA, -jnp.arange(n_freq, dtype=F32) / n_freq)
    ang_r, ang_c = row[:, None] * inv_freq, col[:, None] * inv_freq
    cos_h = jnp.concatenate([jnp.cos(ang_r), jnp.cos(ang_r), jnp.cos(ang_c), jnp.cos(ang_c)], axis=1)
    sin_h = jnp.concatenate([-jnp.sin(ang_r), jnp.sin(ang_r), -jnp.sin(ang_c), jnp.sin(ang_c)], axis=1)
    reps = QK_W // HEAD_DIM
    return jnp.tile(cos_h, (1, reps)), jnp.tile(sin_h, (1, reps))


def _block_diag_mean(width):
    group = np.arange(width) // HEAD_DIM
    return jnp.asarray((group[:, None] == group[None, :]).astype(np.float32) / HEAD_DIM, dtype=BF16)


def _tri_constants():
    t = np.arange(REC_BLOCK)
    same_chunk = (t[:, None] // CHUNK) == (t[None, :] // CHUNK)
    prefix = (same_chunk & (t[None, :] <= t[:, None])).astype(np.float32)
    suffix = (same_chunk & (t[None, :] >= t[:, None])).astype(np.float32)
    tbd = jnp.asarray(np.stack([prefix, suffix]), dtype=BF16)
    tbd_t = jnp.asarray(np.stack([prefix.T, suffix.T]), dtype=BF16)
    return tbd, tbd_t


def _mlstm_expanders():
    e_r = np.zeros((2, GATE_W, MLSTM_W), np.float32)
    e_f = np.zeros((2, GATE_W, MLSTM_W), np.float32)
    for d in range(2):
        for h in range(MLSTM_HEADS):
            lanes = slice(h * HEAD_DIM, (h + 1) * HEAD_DIM)
            i_lane = 2 * GLA_RANK + d * MLSTM_HEADS + h
            f_lane = i_lane + 2 * MLSTM_HEADS
            e_r[d, i_lane, lanes] = 1.0
            e_r[d, f_lane, lanes] = -1.0
            e_f[d, f_lane, lanes] = 1.0
    group = np.arange(MLSTM_W) // HEAD_DIM
    e_h = (group[:, None] == group[None, :]).astype(np.float32)
    return jnp.asarray(e_r, dtype=BF16), jnp.asarray(e_f, dtype=BF16), jnp.asarray(e_h, dtype=BF16)


def _layer(x, consts, norm_mix_g, w_in, attn_qn_g, attn_kn_g, gla_gate_w2, gla_gate_b, gla_norm_g,
           mlstm_gate_b, mlstm_norm_g, w_out, norm_ffn_g, w_up, conv_w, conv_b, w_down):
    cos_t, sin_t, bd_qk, bd_head, tbd, tbd_t, e_r, e_f, e_h = consts
    b, s, _ = x.shape
    n_gates = 4 * MLSTM_HEADS
    main_w = QKV_W + 4 * GLA_W
    gla_a = w_in[:, main_w:main_w + 2 * GLA_RANK]
    mls_main = w_in[:, main_w + 2 * GLA_RANK:main_w + 2 * GLA_RANK + 4 * MLSTM_W]
    mls_g = w_in[:, main_w + 2 * GLA_RANK + 4 * MLSTM_W:]
    pad = jnp.zeros((D_MODEL, GATE_W - 2 * GLA_RANK - n_gates), F32)
    w_in_r = jnp.concatenate([w_in[:, :main_w], mls_main, gla_a, mls_g, pad], axis=1).astype(BF16)

    qkv, gla_proj, mls_proj, gate_proj = _inproj(x, norm_mix_g[None, :], w_in_r)

    qk_gain = jnp.concatenate([jnp.tile(attn_qn_g, ATTN_HEADS), jnp.tile(attn_kn_g, ATTN_KV_HEADS)])[None, :]
    q_t, k, v_t = _qkprep(qkv, cos_t, sin_t, qk_gain, bd_qk)
    attn = _attention(q_t, k, v_t)

    w2ext = jnp.zeros((2, GATE_W, GLA_W), F32)
    w2ext = w2ext.at[0, :GLA_RANK].set(gla_gate_w2[0]).at[1, GLA_RANK:2 * GLA_RANK].set(gla_gate_w2[1])
    w2_hi = w2ext.astype(BF16)
    w2_lo = (w2ext - w2_hi.astype(F32)).astype(BF16)
    cum_f, cum_b = _glagate(gate_proj, w2_hi, w2_lo, gla_gate_b[:, None, :], tbd)
    gof, gob = _gla(gla_proj, cum_f, cum_b)

    nb, cpb = s // REC_BLOCK, REC_BLOCK // CHUNK
    f_lo = 2 * GLA_RANK + 2 * MLSTM_HEADS
    f_rows = gate_proj[:, :, f_lo:f_lo + 2 * MLSTM_HEADS].reshape(b, nb, cpb, CHUNK, 2, MLSTM_HEADS)
    f_rows = f_rows.transpose(0, 4, 1, 2, 5, 3).reshape(b, 2, nb, cpb, MLSTM_W)
    f_rows = jnp.pad(f_rows, ((0, 0), (0, 0), (0, 0), (0, SUBLANE - cpb), (0, 0)))
    bcol = jnp.zeros((1, GATE_W), F32).at[0, 2 * GLA_RANK:2 * GLA_RANK + n_gates].set(
        mlstm_gate_b.reshape(n_gates))
    brow = jnp.repeat(mlstm_gate_b[2:], HEAD_DIM, axis=1)[:, None, :]
    mhf, mhb = _mlstm(mls_proj, gate_proj, f_rows, bcol, brow, tbd, tbd_t, e_r, e_f, e_h)

    x = _outproj(x, attn, gof, gob, gla_proj, mhf, mhb, mls_proj,
                 jnp.tile(gla_norm_g, GLA_HEADS)[None, :], jnp.tile(mlstm_norm_g, MLSTM_HEADS)[None, :],
                 bd_head, w_out.astype(BF16))
    return _ffn(x, norm_ffn_g[None, :], w_up.astype(BF16), conv_w, conv_b[None, :], w_down.astype(BF16))


def _trunk(x, weights):
    seq = x.shape[1]
    cos_t, sin_t = _rope_tables(seq)
    consts = ((cos_t, sin_t, _block_diag_mean(QK_W), _block_diag_mean(GLA_W))
              + _tri_constants() + _mlstm_expanders())
    depth = weights[0].shape[0]
    for l in range(depth):
        x = _layer(x, consts, *[w[l] for w in weights])
    return x


def kernel(x_prompt, x_sample, norm_mix_g, w_in, attn_qn_g, attn_kn_g, gla_gate_w2, gla_gate_b,
           gla_norm_g, mlstm_gate_b, mlstm_norm_g, w_out, norm_ffn_g, w_up, conv_w, conv_b, w_down):
    weights = (norm_mix_g, w_in, attn_qn_g, attn_kn_g, gla_gate_w2, gla_gate_b, gla_norm_g,
               mlstm_gate_b, mlstm_norm_g, w_out, norm_ffn_g, w_up, conv_w, conv_b, w_down)
    assert x_prompt.shape[1:] == x_sample.shape[1:]
    n_prompt = x_prompt.shape[0]
    y = _trunk(jnp.concatenate([x_prompt, x_sample], axis=0), weights)
    return y[:n_prompt], y[n_prompt:]
```

```python
import functools

import jax
import jax.numpy as jnp
import numpy as np
from jax import lax
from jax.experimental import pallas as pl
from jax.experimental.pallas import tpu as pltpu

F32 = jnp.float32
BF16 = jnp.bfloat16

D_MODEL = 1024
GRID_W = 64
HEAD_DIM = 64
EPS = 1e-6
ATTN_HEADS = 8
ATTN_KV_HEADS = 2
ATTN_GROUP = ATTN_HEADS // ATTN_KV_HEADS
ROPE_THETA = 10000.0
GLA_HEADS = 4
GLA_RANK = 16
GLA_TAU = 16.0
MLSTM_HEADS = 4
CHUNK = 64
ATTN_W = ATTN_HEADS * HEAD_DIM
KV_W = ATTN_KV_HEADS * HEAD_DIM
GLA_W = GLA_HEADS * HEAD_DIM
MLSTM_W = MLSTM_HEADS * HEAD_DIM
QK_W = ATTN_W + KV_W
QKV_W = ATTN_W + 2 * KV_W
D_FF = 2816
CONV_WIDTH = 3
GATE_W = 128
Q_SCALE = HEAD_DIM ** -0.5
LOG2_E = 1.4426950408889634

LANE = 128
SUBLANE = 8
VMEM_LIMIT = 56 * 1024 * 1024

ROW_TILE = 512
REC_BLOCK = 256
ATTN_TQ = 512
ATTN_STREAM_LANES = 256
ATTN_TK = 256
VT_ROWS = HEAD_DIM + 16
FF_BLOCK = 1408
HALO = SUBLANE

NT_DIMS = (((1,), (1,)), ((), ()))
TN_DIMS = (((0,), (0,)), ((), ()))


def _params(*sem):
    return pltpu.CompilerParams(dimension_semantics=sem, vmem_limit_bytes=VMEM_LIMIT)


def _log_sigmoid(z):
    return jnp.minimum(z, 0.0) - jnp.log(1.0 + jnp.exp(-jnp.abs(z)))


def _sigmoid(z):
    return 1.0 / (1.0 + jnp.exp(-z))


def _dot(a, b):
    return jnp.dot(a.astype(BF16), b.astype(BF16), preferred_element_type=F32)


def _dot_dims(a, b, dims):
    return lax.dot_general(a.astype(BF16), b.astype(BF16), dims, preferred_element_type=F32)


def _split3(x):
    p1 = x.astype(BF16)
    rest = x - p1.astype(F32)
    p2 = rest.astype(BF16)
    return p1, p2, (rest - p2.astype(F32)).astype(BF16)


def _cumsum_cols(tbd, x):
    return sum(jnp.dot(tbd, p, preferred_element_type=F32) for p in _split3(x))


def _cumsum_rows(x, tbd_t):
    return sum(jnp.dot(p, tbd_t, preferred_element_type=F32) for p in _split3(x))


def _dot_split(a, b_hi, b_lo):
    a_hi = a.astype(BF16)
    a_lo = (a - a_hi.astype(F32)).astype(BF16)
    return (jnp.dot(a_hi, b_hi, preferred_element_type=F32)
            + jnp.dot(a_lo, b_hi, preferred_element_type=F32)
            + jnp.dot(a_hi, b_lo, preferred_element_type=F32))


def _group_mean(sq, bd):
    hi = sq.astype(BF16)
    lo = (sq - hi.astype(F32)).astype(BF16)
    return (jnp.dot(hi, bd, preferred_element_type=F32)
            + jnp.dot(lo, bd, preferred_element_type=F32))


def _inproj_kernel(x_ref, g_ref, w_ref, qkv_ref, gla_ref, mls_ref, gate_ref):
    x = x_ref[0]
    ms = jnp.mean(x * x, axis=-1, keepdims=True)
    h = (x * lax.rsqrt(ms + EPS) * g_ref[...]).astype(BF16)
    col = 0
    for ref in (qkv_ref, gla_ref, mls_ref, gate_ref):
        width = ref.shape[-1]
        ref[0] = jnp.dot(h, w_ref[:, col:col + width], preferred_element_type=F32)
        col += width


def _inproj(x, g, w):
    b, s, d = x.shape
    widths = (QKV_W, 4 * GLA_W, 4 * MLSTM_W, GATE_W)
    row = lambda n: pl.BlockSpec((1, ROW_TILE, n), lambda bi, i: (bi, i, 0))
    const = lambda shape: pl.BlockSpec(shape, lambda bi, i: (0,) * len(shape))
    return pl.pallas_call(
        _inproj_kernel,
        grid=(b, s // ROW_TILE),
        in_specs=[row(d), const((1, d)), const(w.shape)],
        out_specs=[row(n) for n in widths],
        out_shape=[jax.ShapeDtypeStruct((b, s, n), F32) for n in widths],
        compiler_params=_params("parallel", "parallel"),
        name="inproj",
    )(x, g, w)


def _qkprep_kernel(qkv_ref, cos_ref, sin_ref, gain_ref, bd_ref, qt_ref, k_ref, vt_ref):
    x = qkv_ref[0]
    qk = x[:, :QK_W]
    ms = _group_mean(qk * qk, bd_ref[...])
    y = qk * lax.rsqrt(ms + EPS) * gain_ref[...]
    lane = lax.broadcasted_iota(jnp.int32, y.shape, 1)
    first_half = (lane & 31) < 16
    quarter = HEAD_DIM // 4
    swapped = jnp.where(first_half, pltpu.roll(y, QK_W - quarter, 1), pltpu.roll(y, quarter, 1))
    r = y * cos_ref[...] + swapped * sin_ref[...]
    q_t = (r[:, :ATTN_W] * (Q_SCALE * LOG2_E)).T.astype(BF16)
    qt_ref[0] = q_t.reshape(ATTN_HEADS, HEAD_DIM, q_t.shape[-1])
    v_t = x[:, QK_W:].T.astype(BF16)
    v_t = v_t.reshape(ATTN_KV_HEADS, HEAD_DIM, v_t.shape[-1])
    row = lax.broadcasted_iota(jnp.int32, (ATTN_KV_HEADS, VT_ROWS - HEAD_DIM, v_t.shape[-1]), 1)
    vt_ref[0] = jnp.concatenate([v_t, jnp.where(row == 0, 1.0, 0.0).astype(BF16)], axis=1)
    for h in range(ATTN_KV_HEADS):
        k_ref[0, h] = r[:, ATTN_W + h * HEAD_DIM:ATTN_W + (h + 1) * HEAD_DIM].astype(BF16)


def _qkprep(qkv, cos_t, sin_t, gain, bd):
    b, s, _ = qkv.shape
    const = lambda shape: pl.BlockSpec(shape, lambda bi, i: (0,) * len(shape))
    tab = pl.BlockSpec((ROW_TILE, QK_W), lambda bi, i: (i, 0))
    t_spec = lambda heads, rows: pl.BlockSpec((1, heads, rows, ROW_TILE), lambda bi, i: (bi, 0, 0, i))
    t_shape = lambda heads, rows: jax.ShapeDtypeStruct((b, heads, rows, s), BF16)
    return pl.pallas_call(
        _qkprep_kernel,
        grid=(b, s // ROW_TILE),
        in_specs=[pl.BlockSpec((1, ROW_TILE, QKV_W), lambda bi, i: (bi, i, 0)),
                  tab, tab, const((1, QK_W)), const((QK_W, QK_W))],
        out_specs=[t_spec(ATTN_HEADS, HEAD_DIM),
                   pl.BlockSpec((1, ATTN_KV_HEADS, ROW_TILE, HEAD_DIM), lambda bi, i: (bi, 0, i, 0)),
                   t_spec(ATTN_KV_HEADS, VT_ROWS)],
        out_shape=[t_shape(ATTN_HEADS, HEAD_DIM),
                   jax.ShapeDtypeStruct((b, ATTN_KV_HEADS, s, HEAD_DIM), BF16),
                   t_shape(ATTN_KV_HEADS, VT_ROWS)],
        compiler_params=_params("parallel", "parallel"),
        name="qkprep",
    )(qkv, cos_t, sin_t, gain, bd)


def _attn_kernel(qt_ref, k_ref, vt_ref, o_ref):
    tq = qt_ref.shape[-1]
    n_chunks = k_ref.shape[2] // ATTN_TK
    keys = lambda c: slice(c * ATTN_TK, (c + 1) * ATTN_TK)

    def stream(head, lo):
        q_t = qt_ref[0, head, :, lo:lo + ATTN_STREAM_LANES]
        m = jnp.full((1, ATTN_STREAM_LANES), -jnp.inf, F32)
        acc = jnp.zeros((VT_ROWS, ATTN_STREAM_LANES), F32)

        def scores(c):
            s = jnp.dot(k_ref[0, 0, keys(c), :], q_t, preferred_element_type=F32)
            return s, jnp.max(s, axis=0, keepdims=True)

        nxt = scores(0)
        yield
        for c in range(n_chunks):
            s_t, s_max = nxt
            if c + 1 < n_chunks:
                nxt = scores(c + 1)
            m_new = jnp.maximum(m, s_max)
            yield
            p_t = jnp.exp2((s_t - m_new).astype(BF16))
            yield
            acc = (jnp.exp2(m - m_new) * acc
                   + jnp.dot(vt_ref[0, 0, :, keys(c)], p_t, preferred_element_type=F32))
            m = m_new
            yield
        out = (acc[:HEAD_DIM] / acc[HEAD_DIM:HEAD_DIM + 1]).T
        o_ref[0, lo:lo + ATTN_STREAM_LANES, head * HEAD_DIM:(head + 1) * HEAD_DIM] = out.astype(o_ref.dtype)

    _interleave(*[stream(j, lo) for j in range(ATTN_GROUP) for lo in range(0, tq, ATTN_STREAM_LANES)])


def _attention(q_t, k, v_t):
    b, _, _, s = q_t.shape
    gw = ATTN_GROUP * HEAD_DIM
    return pl.pallas_call(
        _attn_kernel,
        grid=(b, ATTN_KV_HEADS, s // ATTN_TQ),
        in_specs=[pl.BlockSpec((1, ATTN_GROUP, HEAD_DIM, ATTN_TQ), lambda bi, h, i: (bi, h, 0, i)),
                  pl.BlockSpec((1, 1, s, HEAD_DIM), lambda bi, h, i: (bi, h, 0, 0)),
                  pl.BlockSpec((1, 1, VT_ROWS, s), lambda bi, h, i: (bi, h, 0, 0))],
        out_specs=pl.BlockSpec((1, ATTN_TQ, gw), lambda bi, h, i: (bi, i, h)),
        out_shape=jax.ShapeDtypeStruct((b, s, ATTN_W), BF16),
        compiler_params=_params("parallel", "parallel", "arbitrary"),
        name="attention",
    )(q_t, k, v_t)


def _chunk_order(d):
    n_chunks = REC_BLOCK // CHUNK
    return range(n_chunks) if d == 0 else range(n_chunks - 1, -1, -1)


def _glagate_kernel(a_ref, w2hi_ref, w2lo_ref, gb_ref, tbd_ref, cf_ref, cb_ref):
    a = a_ref[0]
    for d, out_ref in enumerate((cf_ref, cb_ref)):
        z = _dot_split(a, w2hi_ref[d], w2lo_ref[d]) + gb_ref[d]
        g = _log_sigmoid(z) * (1.0 / GLA_TAU)
        for sb in range(a.shape[0] // REC_BLOCK):
            rows = slice(sb * REC_BLOCK, (sb + 1) * REC_BLOCK)
            out_ref[0, rows, :] = _cumsum_cols(tbd_ref[d], g[rows])


def _glagate(gate_proj, w2_hi, w2_lo, gb, tbd):
    b, s, _ = gate_proj.shape
    const = lambda shape: pl.BlockSpec(shape, lambda bi, i: (0,) * len(shape))
    out_spec = pl.BlockSpec((1, ROW_TILE, GLA_W), lambda bi, i: (bi, i, 0))
    out_shape = jax.ShapeDtypeStruct((b, s, GLA_W), F32)
    return pl.pallas_call(
        _glagate_kernel,
        grid=(b, s // ROW_TILE),
        in_specs=[pl.BlockSpec((1, ROW_TILE, GATE_W), lambda bi, i: (bi, i, 0)),
                  const(w2_hi.shape), const(w2_lo.shape), const(gb.shape), const(tbd.shape)],
        out_specs=[out_spec, out_spec],
        out_shape=[out_shape, out_shape],
        compiler_params=_params("parallel", "parallel"),
        name="glagate",
    )(gate_proj, w2_hi, w2_lo, gb, tbd)


def _gla_direction(q_ref, k_ref, v_ref, cum_ref, o_ref, st_ref, d):
    causal_t, _, head_lane, block_diag = _head_layout_masks(d)
    last = CHUNK - 1 if d == 0 else 0
    st = st_ref[d]
    for i in _chunk_order(d):
        rows = slice(i * CHUNK, (i + 1) * CHUNK)
        cum = cum_ref[0, rows, :]
        total = cum[last:last + 1]
        ref = cum[CHUNK // 2:CHUNK // 2 + 1]
        qt = q_ref[0, rows, :] * Q_SCALE * jnp.exp(cum - ref)
        kt = k_ref[0, rows, :] * jnp.exp(ref - cum)
        qe = qt * jnp.exp(ref)
        kd = kt * jnp.exp(total - ref)
        v = v_ref[0, rows, :]
        q4 = jnp.concatenate([jnp.where(head_lane[h], qt, 0.0) for h in range(GLA_HEADS)], axis=0)
        att_t = jnp.where(causal_t, _dot_dims(kt, q4, NT_DIMS), 0.0)
        yield
        full = _dot_dims(att_t, v, TN_DIMS)
        out = _dot_dims(qe, st, NT_DIMS)
        yield
        for h in range(GLA_HEADS):
            out = out + jnp.where(head_lane[h], full[h * CHUNK:(h + 1) * CHUNK], 0.0)
        o_ref[0, rows, :] = out
        st = st * jnp.exp(total) + jnp.where(block_diag, _dot_dims(v, kd, TN_DIMS), 0.0)
        yield
    st_ref[d] = st


def _interleave(*stages):
    stages = list(stages)
    while stages:
        for g in list(stages):
            if next(g, StopIteration) is StopIteration:
                stages.remove(g)


def _gla_kernel(qf_ref, kf_ref, vf_ref, cf_ref, qb_ref, kb_ref, vb_ref, cb_ref, of_ref, ob_ref, st_ref):
    @pl.when(pl.program_id(1) == 0)
    def _():
        st_ref[...] = jnp.zeros(st_ref.shape, F32)

    _interleave(_gla_direction(qf_ref, kf_ref, vf_ref, cf_ref, of_ref, st_ref, 0),
                _gla_direction(qb_ref, kb_ref, vb_ref, cb_ref, ob_ref, st_ref, 1))


def _gla(gla_proj, cum_f, cum_b):
    b, s, _ = gla_proj.shape
    nb = s // REC_BLOCK
    fwd = lambda col: pl.BlockSpec((1, REC_BLOCK, GLA_W), lambda bi, j: (bi, j, col))
    bwd = lambda col: pl.BlockSpec((1, REC_BLOCK, GLA_W), lambda bi, j: (bi, nb - 1 - j, col))
    out_shape = jax.ShapeDtypeStruct((b, s, GLA_W), F32)
    return pl.pallas_call(
        _gla_kernel,
        grid=(b, nb),
        in_specs=[fwd(0), fwd(1), fwd(2), fwd(0), bwd(0), bwd(1), bwd(2), bwd(0)],
        out_specs=[fwd(0), bwd(0)],
        out_shape=[out_shape, out_shape],
        scratch_shapes=[pltpu.VMEM((2, GLA_W, GLA_W), F32)],
        compiler_params=_params("parallel", "arbitrary"),
        name="gla",
    )(gla_proj, gla_proj, gla_proj, cum_f, gla_proj, gla_proj, gla_proj, cum_b)


def _head_layout_masks(d):
    sub = lax.broadcasted_iota(jnp.int32, (CHUNK, MLSTM_W), 0)
    lane = lax.broadcasted_iota(jnp.int32, (CHUNK, MLSTM_W), 1)
    t_lane = lane & (HEAD_DIM - 1)
    causal_t = (sub <= t_lane) if d == 0 else (sub >= t_lane)
    eye_t = sub == t_lane
    head_lane = [(lane >> 6) == h for h in range(MLSTM_HEADS)]
    brow = lax.broadcasted_iota(jnp.int32, (MLSTM_W, MLSTM_W), 0) >> 6
    bcol = lax.broadcasted_iota(jnp.int32, (MLSTM_W, MLSTM_W), 1) >> 6
    return causal_t, eye_t, head_lane, brow == bcol


def _expand3(x, e):
    return sum(jnp.dot(p, e, preferred_element_type=F32) for p in _split3(x))


def _mlstm_direction(q_ref, k_ref, v_ref, gates, frow, h_ref, tbd, tbd_t, e_r, e_f, e_h,
                     ct_ref, n_ref, m_ref, d):
    causal_t, eye_t, head_lane, block_diag = _head_layout_masks(d)
    last = CHUNK - 1 if d == 0 else 0
    n_chunks = REC_BLOCK // CHUNK
    lane = lax.broadcasted_iota(jnp.int32, gates.shape, 1)
    i_lo = 2 * GLA_RANK + d * MLSTM_HEADS
    is_i_lane = (lane >= i_lo) & (lane < i_lo + MLSTM_HEADS)
    fcum = _cumsum_cols(tbd, _log_sigmoid(gates))
    r_exp = _expand3(jnp.where(is_i_lane, gates, fcum), e_r)
    f_last = jnp.concatenate([fcum[c * CHUNK + last:c * CHUNK + last + 1] for c in range(n_chunks)]
                             + [jnp.zeros((SUBLANE - n_chunks, fcum.shape[1]), F32)], axis=0)
    f_tot_rows = _expand3(f_last, e_f)
    fc_rows = _cumsum_rows(_log_sigmoid(frow), tbd_t)

    ct, n_row, m_row = ct_ref[d], n_ref[d], m_ref[d]
    for c in _chunk_order(d):
        rows = slice(c * CHUNK, (c + 1) * CHUNK)
        q, ks, v = q_ref[0, rows, :], k_ref[0, rows, :] * Q_SCALE, v_ref[0, rows, :]
        q4 = jnp.concatenate([jnp.where(head_lane[h], q, 0.0) for h in range(MLSTM_HEADS)], axis=0)
        kn = jnp.concatenate([ks, jnp.broadcast_to(n_row, (2 * SUBLANE, MLSTM_W))], axis=0)
        z = _dot_dims(kn, q4, NT_DIMS)
        s_t, qn = z[:CHUNK], z[CHUNK:CHUNK + 1]
        yield
        rc = r_exp[rows]
        rmax_t = jnp.max(jnp.where(causal_t, rc, -jnp.inf), axis=0, keepdims=True)
        rmax = jnp.max(rc, axis=0, keepdims=True)
        p_t = s_t * jnp.exp(jnp.where(causal_t, rc - rmax_t, -jnp.inf))
        row_sum = jnp.sum(p_t, axis=0, keepdims=True)
        kw = ks * jnp.exp(rc - rmax)
        fc_row = fc_rows[c:c + 1]
        inter = fc_row + m_row
        intra = fc_row + rmax_t
        m_t = jnp.maximum(inter, intra)
        w_inter, w_intra = jnp.exp(inter - m_t), jnp.exp(intra - m_t)
        den = w_inter * qn + w_intra * row_sum
        inv = 1.0 / jnp.maximum(jnp.abs(den), jnp.exp(-m_t))
        full = _dot_dims(p_t * (w_intra * inv), v, TN_DIMS)
        yield
        out = _expand3(jnp.where(eye_t, w_inter * inv, 0.0), e_h) * _dot(q, ct)
        for h in range(MLSTM_HEADS):
            out = out + jnp.where(head_lane[h], full[h * CHUNK:(h + 1) * CHUNK], 0.0)
        h_ref[0, rows, :] = out
        m_max = jnp.maximum(m_row, rmax)
        keep, add = jnp.exp(m_row - m_max), jnp.exp(rmax - m_max)
        ct = keep * ct + add * jnp.where(block_diag, _dot_dims(kw, v, TN_DIMS), 0.0)
        n_row = keep * n_row + add * jnp.sum(kw, axis=0, keepdims=True)
        m_row = f_tot_rows[c:c + 1] + m_max
        yield
    ct_ref[d], n_ref[d], m_ref[d] = ct, n_row, m_row


def _mlstm_kernel(qf_ref, kf_ref, vf_ref, gcf_ref, grf_ref, qb_ref, kb_ref, vb_ref, gcb_ref, grb_ref,
                  bcol_ref, brow_ref, tbd_ref, tbdt_ref, er_ref, ef_ref, eh_ref,
                  hf_ref, hb_ref, ct_ref, n_ref, m_ref):
    @pl.when(pl.program_id(1) == 0)
    def _():
        ct_ref[...] = jnp.zeros(ct_ref.shape, F32)
        n_ref[...] = jnp.zeros(n_ref.shape, F32)
        m_ref[...] = jnp.zeros(m_ref.shape, F32)

    _interleave(
        _mlstm_direction(qf_ref, kf_ref, vf_ref, gcf_ref[0] + bcol_ref[...], grf_ref[0, 0, 0] + brow_ref[0],
                         hf_ref, tbd_ref[0], tbdt_ref[0], er_ref[0], ef_ref[0], eh_ref[...],
                         ct_ref, n_ref, m_ref, 0),
        _mlstm_direction(qb_ref, kb_ref, vb_ref, gcb_ref[0] + bcol_ref[...], grb_ref[0, 0, 0] + brow_ref[1],
                         hb_ref, tbd_ref[1], tbdt_ref[1], er_ref[1], ef_ref[1], eh_ref[...],
                         ct_ref, n_ref, m_ref, 1))


def _mlstm(mls_proj, gate_proj, f_rows, bcol, brow, tbd, tbd_t, e_r, e_f, e_h):
    b, s, _ = mls_proj.shape
    nb = s // REC_BLOCK
    fwd = lambda col: pl.BlockSpec((1, REC_BLOCK, MLSTM_W), lambda bi, j: (bi, j, col))
    bwd = lambda col: pl.BlockSpec((1, REC_BLOCK, MLSTM_W), lambda bi, j: (bi, nb - 1 - j, col))
    const = lambda shape: pl.BlockSpec(shape, lambda bi, j: (0,) * len(shape))
    gcol_f = pl.BlockSpec((1, REC_BLOCK, GATE_W), lambda bi, j: (bi, j, 0))
    gcol_b = pl.BlockSpec((1, REC_BLOCK, GATE_W), lambda bi, j: (bi, nb - 1 - j, 0))
    frow_shape = (1, 1, 1) + f_rows.shape[3:]
    frow_f = pl.BlockSpec(frow_shape, lambda bi, j: (bi, 0, j, 0, 0))
    frow_b = pl.BlockSpec(frow_shape, lambda bi, j: (bi, 1, nb - 1 - j, 0, 0))
    out_shape = jax.ShapeDtypeStruct((b, s, MLSTM_W), F32)
    return pl.pallas_call(
        _mlstm_kernel,
        grid=(b, nb),
        in_specs=[fwd(0), fwd(1), fwd(2), gcol_f, frow_f, bwd(0), bwd(1), bwd(2), gcol_b, frow_b,
                  const(bcol.shape), const(brow.shape), const(tbd.shape), const(tbd_t.shape),
                  const(e_r.shape), const(e_f.shape), const(e_h.shape)],
        out_specs=[pl.BlockSpec((1, REC_BLOCK, MLSTM_W), lambda bi, j: (bi, j, 0)),
                   pl.BlockSpec((1, REC_BLOCK, MLSTM_W), lambda bi, j: (bi, nb - 1 - j, 0))],
        out_shape=[out_shape, out_shape],
        scratch_shapes=[pltpu.VMEM((2, MLSTM_W, MLSTM_W), F32),
                        pltpu.VMEM((2, 1, MLSTM_W), F32),
                        pltpu.VMEM((2, 1, MLSTM_W), F32)],
        compiler_params=_params("parallel", "arbitrary"),
        name="mlstm",
    )(mls_proj, mls_proj, mls_proj, gate_proj, f_rows,
      mls_proj, mls_proj, mls_proj, gate_proj, f_rows, bcol, brow, tbd, tbd_t, e_r, e_f, e_h)


def _outproj_kernel(x_ref, attn_ref, gof_ref, gob_ref, gg_ref, mhf_ref, mhb_ref, mo_ref,
                    gng_ref, mng_ref, bd_ref, wo_ref, out_ref):
    bd = bd_ref[...]

    def head_norm(t, gain):
        return t * lax.rsqrt(_group_mean(t * t, bd) + EPS) * gain

    gg = gg_ref[0]
    gla = head_norm(gof_ref[0] + gob_ref[0], gng_ref[...]) * (gg * _sigmoid(gg))
    mls = head_norm(_sigmoid(mo_ref[0]) * (mhf_ref[0] + mhb_ref[0]), mng_ref[...])
    acc = jnp.dot(attn_ref[0], wo_ref[:ATTN_W, :], preferred_element_type=F32)
    acc += jnp.dot(gla.astype(BF16), wo_ref[ATTN_W:ATTN_W + GLA_W, :], preferred_element_type=F32)
    acc += jnp.dot(mls.astype(BF16), wo_ref[ATTN_W + GLA_W:, :], preferred_element_type=F32)
    out_ref[0] = x_ref[0] + acc


def _outproj(x, attn, gof, gob, gla_proj, mhf, mhb, mls_proj, gng, mng, bd, wo):
    b, s, d = x.shape
    row = lambda n, col=0: pl.BlockSpec((1, ROW_TILE, n), lambda bi, i: (bi, i, col))
    const = lambda shape: pl.BlockSpec(shape, lambda bi, i: (0,) * len(shape))
    return pl.pallas_call(
        _outproj_kernel,
        grid=(b, s // ROW_TILE),
        in_specs=[row(d), row(ATTN_W), row(GLA_W), row(GLA_W), row(GLA_W, 3),
                  row(MLSTM_W), row(MLSTM_W), row(MLSTM_W, 3),
                  const((1, GLA_W)), const((1, MLSTM_W)), const(bd.shape), const(wo.shape)],
        out_specs=row(d),
        out_shape=jax.ShapeDtypeStruct((b, s, d), F32),
        compiler_params=_params("parallel", "parallel"),
        name="outproj",
    )(x, attn, gof, gob, gla_proj, mhf, mhb, mls_proj, gng, mng, bd, wo)


def _ffn_kernel(x_ref, prev_ref, next_ref, g_ref, wg_ref, wv_ref, cwg_ref, cwv_ref, cbg_ref, cbv_ref,
                wd_ref, out_ref, h_ref, ug_ref, uv_ref, acc_ref):
    i, j = pl.program_id(1), pl.program_id(2)
    tm = x_ref.shape[1]

    def normed(t):
        ms = jnp.mean(t * t, axis=-1, keepdims=True)
        return t * lax.rsqrt(ms + EPS) * g_ref[...]

    @pl.when(j == 0)
    def _():
        keep_prev = (i > 0).astype(F32)
        keep_next = (i < pl.num_programs(1) - 1).astype(F32)
        h_ref[0:HALO, :] = (normed(prev_ref[0]) * keep_prev).astype(BF16)
        h_ref[HALO:HALO + tm, :] = normed(x_ref[0]).astype(BF16)
        h_ref[HALO + tm:, :] = (normed(next_ref[0]) * keep_next).astype(BF16)
        acc_ref[...] = jnp.zeros(acc_ref.shape, F32)

    h = h_ref[...]
    ug_ref[...] = jnp.dot(h, wg_ref[...], preferred_element_type=F32)
    uv_ref[...] = jnp.dot(h, wv_ref[...], preferred_element_type=F32)

    def conv(u_ref, cw_ref, cb_ref):
        out = cb_ref[...] + u_ref[HALO - 1:HALO - 1 + tm, :] * cw_ref[0:1, :]
        for t in range(1, CONV_WIDTH):
            out = out + u_ref[HALO - 1 + t:HALO - 1 + t + tm, :] * cw_ref[t:t + 1, :]
        return out

    gate = conv(ug_ref, cwg_ref, cbg_ref)
    val = conv(uv_ref, cwv_ref, cbv_ref)
    act = (gate * _sigmoid(gate) * val).astype(BF16)
    acc_ref[...] += jnp.dot(act, wd_ref[...], preferred_element_type=F32)

    @pl.when(j == pl.num_programs(2) - 1)
    def _():
        out_ref[0] = x_ref[0] + acc_ref[...]


def _ffn(x, g, w_up, conv_w, conv_b, w_down):
    b, s, d = x.shape
    nj = D_FF // FF_BLOCK
    tiles_per_halo = ROW_TILE // HALO
    last_halo = s // HALO - 1
    row = pl.BlockSpec((1, ROW_TILE, d), lambda bi, i, j: (bi, i, 0))
    prev = pl.BlockSpec((1, HALO, d), lambda bi, i, j: (bi, jnp.maximum(i * tiles_per_halo - 1, 0), 0))
    nxt = pl.BlockSpec((1, HALO, d),
                       lambda bi, i, j: (bi, jnp.minimum((i + 1) * tiles_per_halo, last_halo), 0))
    gate_cols = lambda rows: pl.BlockSpec((rows, FF_BLOCK), lambda bi, i, j: (0, j))
    val_cols = lambda rows: pl.BlockSpec((rows, FF_BLOCK), lambda bi, i, j: (0, nj + j))
    return pl.pallas_call(
        _ffn_kernel,
        grid=(b, s // ROW_TILE, nj),
        in_specs=[row, prev, nxt, pl.BlockSpec((1, d), lambda bi, i, j: (0, 0)),
                  gate_cols(d), val_cols(d), gate_cols(CONV_WIDTH), val_cols(CONV_WIDTH),
                  gate_cols(1), val_cols(1),
                  pl.BlockSpec((FF_BLOCK, d), lambda bi, i, j: (j, 0))],
        out_specs=row,
        out_shape=jax.ShapeDtypeStruct((b, s, d), F32),
        scratch_shapes=[pltpu.VMEM((ROW_TILE + 2 * HALO, d), BF16),
                        pltpu.VMEM((ROW_TILE + 2 * HALO, FF_BLOCK), F32),
                        pltpu.VMEM((ROW_TILE + 2 * HALO, FF_BLOCK), F32),
                        pltpu.VMEM((ROW_TILE, d), F32)],
        compiler_params=_params("parallel", "parallel", "arbitrary"),
        name="ffn",
    )(x, x, x, g, w_up, w_up, conv_w, conv_w, conv_b, conv_b, w_down)


def _rope_tables(seq):
    n_rows = seq // GRID_W
    row = jnp.repeat(jnp.arange(n_rows, dtype=F32), GRID_W)
    col = jnp.tile(jnp.arange(GRID_W, dtype=F32), n_rows)
    n_freq = HEAD_DIM // 4
    inv_freq = jnp.power(ROPE_THETA, -jnp.arange(n_freq, dtype=F32) / n_freq)
    ang_r, ang_c = row[:, None] * inv_freq, col[:, None] * inv_freq
    cos_h = jnp.concatenate([jnp.cos(ang_r), jnp.cos(ang_r), jnp.cos(ang_c), jnp.cos(ang_c)], axis=1)
    sin_h = jnp.concatenate([-jnp.sin(ang_r), jnp.sin(ang_r), -jnp.sin(ang_c), jnp.sin(ang_c)], axis=1)
    reps = QK_W // HEAD_DIM
    return jnp.tile(cos_h, (1, reps)), jnp.tile(sin_h, (1, reps))


def _block_diag_mean(width):
    group = np.arange(width) // HEAD_DIM
    return jnp.asarray((group[:, None] == group[None, :]).astype(np.float32) / HEAD_DIM, dtype=BF16)


def _tri_constants():
    t = np.arange(REC_BLOCK)
    same_chunk = (t[:, None] // CHUNK) == (t[None, :] // CHUNK)
    prefix = (same_chunk & (t[None, :] <= t[:, None])).astype(np.float32)
    suffix = (same_chunk & (t[None, :] >= t[:, None])).astype(np.float32)
    tbd = jnp.asarray(np.stack([prefix, suffix]), dtype=BF16)
    tbd_t = jnp.asarray(np.stack([prefix.T, suffix.T]), dtype=BF16)
    return tbd, tbd_t


def _mlstm_expanders():
    e_r = np.zeros((2, GATE_W, MLSTM_W), np.float32)
    e_f = np.zeros((2, GATE_W, MLSTM_W), np.float32)
    for d in range(2):
        for h in range(MLSTM_HEADS):
            lanes = slice(h * HEAD_DIM, (h + 1) * HEAD_DIM)
            i_lane = 2 * GLA_RANK + d * MLSTM_HEADS + h
            f_lane = i_lane + 2 * MLSTM_HEADS
            e_r[d, i_lane, lanes] = 1.0
            e_r[d, f_lane, lanes] = -1.0
            e_f[d, f_lane, lanes] = 1.0
    group = np.arange(MLSTM_W) // HEAD_DIM
    e_h = (group[:, None] == group[None, :]).astype(np.float32)
    return jnp.asarray(e_r, dtype=BF16), jnp.asarray(e_f, dtype=BF16), jnp.asarray(e_h, dtype=BF16)


def _layer(x, consts, norm_mix_g, w_in, attn_qn_g, attn_kn_g, gla_gate_w2, gla_gate_b, gla_norm_g,
           mlstm_gate_b, mlstm_norm_g, w_out, norm_ffn_g, w_up, conv_w, conv_b, w_down):
    cos_t, sin_t, bd_qk, bd_head, tbd, tbd_t, e_r, e_f, e_h = consts
    b, s, _ = x.shape
    n_gates = 4 * MLSTM_HEADS
    main_w = QKV_W + 4 * GLA_W
    gla_a = w_in[:, main_w:main_w + 2 * GLA_RANK]
    mls_main = w_in[:, main_w + 2 * GLA_RANK:main_w + 2 * GLA_RANK + 4 * MLSTM_W]
    mls_g = w_in[:, main_w + 2 * GLA_RANK + 4 * MLSTM_W:]
    pad = jnp.zeros((D_MODEL, GATE_W - 2 * GLA_RANK - n_gates), F32)
    w_in_r = jnp.concatenate([w_in[:, :main_w], mls_main, gla_a, mls_g, pad], axis=1).astype(BF16)

    qkv, gla_proj, mls_proj, gate_proj = _inproj(x, norm_mix_g[None, :], w_in_r)

    qk_gain = jnp.concatenate([jnp.tile(attn_qn_g, ATTN_HEADS), jnp.tile(attn_kn_g, ATTN_KV_HEADS)])[None, :]
    q_t, k, v_t = _qkprep(qkv, cos_t, sin_t, qk_gain, bd_qk)
    attn = _attention(q_t, k, v_t)

    w2ext = jnp.zeros((2, GATE_W, GLA_W), F32)
    w2ext = w2ext.at[0, :GLA_RANK].set(gla_gate_w2[0]).at[1, GLA_RANK:2 * GLA_RANK].set(gla_gate_w2[1])
    w2_hi = w2ext.astype(BF16)
    w2_lo = (w2ext - w2_hi.astype(F32)).astype(BF16)
    cum_f, cum_b = _glagate(gate_proj, w2_hi, w2_lo, gla_gate_b[:, None, :], tbd)
    gof, gob = _gla(gla_proj, cum_f, cum_b)

    nb, cpb = s // REC_BLOCK, REC_BLOCK // CHUNK
    f_lo = 2 * GLA_RANK + 2 * MLSTM_HEADS
    f_rows = gate_proj[:, :, f_lo:f_lo + 2 * MLSTM_HEADS].reshape(b, nb, cpb, CHUNK, 2, MLSTM_HEADS)
    f_rows = f_rows.transpose(0, 4, 1, 2, 5, 3).reshape(b, 2, nb, cpb, MLSTM_W)
    f_rows = jnp.pad(f_rows, ((0, 0), (0, 0), (0, 0), (0, SUBLANE - cpb), (0, 0)))
    bcol = jnp.zeros((1, GATE_W), F32).at[0, 2 * GLA_RANK:2 * GLA_RANK + n_gates].set(
        mlstm_gate_b.reshape(n_gates))
    brow = jnp.repeat(mlstm_gate_b[2:], HEAD_DIM, axis=1)[:, None, :]
    mhf, mhb = _mlstm(mls_proj, gate_proj, f_rows, bcol, brow, tbd, tbd_t, e_r, e_f, e_h)

    x = _outproj(x, attn, gof, gob, gla_proj, mhf, mhb, mls_proj,
                 jnp.tile(gla_norm_g, GLA_HEADS)[None, :], jnp.tile(mlstm_norm_g, MLSTM_HEADS)[None, :],
                 bd_head, w_out.astype(BF16))
    return _ffn(x, norm_ffn_g[None, :], w_up.astype(BF16), conv_w, conv_b[None, :], w_down.astype(BF16))


def _trunk(x, weights):
    seq = x.shape[1]
    cos_t, sin_t = _rope_tables(seq)
    consts = ((cos_t, sin_t, _block_diag_mean(QK_W), _block_diag_mean(GLA_W))
              + _tri_constants() + _mlstm_expanders())
    depth = weights[0].shape[0]
    for l in range(depth):
        x = _layer(x, consts, *[w[l] for w in weights])
    return x


def kernel(x_prompt, x_sample, norm_mix_g, w_in, attn_qn_g, attn_kn_g, gla_gate_w2, gla_gate_b,
           gla_norm_g, mlstm_gate_b, mlstm_norm_g, w_out, norm_ffn_g, w_up, conv_w, conv_b, w_down):
    weights = (norm_mix_g, w_in, attn_qn_g, attn_kn_g, gla_gate_w2, gla_gate_b, gla_norm_g,
               mlstm_gate_b, mlstm_norm_g, w_out, norm_ffn_g, w_up, conv_w, conv_b, w_down)
    return _trunk(x_prompt, weights), _trunk(x_sample, weights)
```

```python
import functools

import jax
import jax.numpy as jnp
import numpy as np
from jax import lax
from jax.experimental import pallas as pl
from jax.experimental.pallas import tpu as pltpu

F32 = jnp.float32
BF16 = jnp.bfloat16

D_MODEL = 1024
GRID_W = 64
HEAD_DIM = 64
EPS = 1e-6
ATTN_HEADS = 8
ATTN_KV_HEADS = 2
ATTN_GROUP = ATTN_HEADS // ATTN_KV_HEADS
ROPE_THETA = 10000.0
GLA_HEADS = 4
GLA_RANK = 16
GLA_TAU = 16.0
MLSTM_HEADS = 4
CHUNK = 64
ATTN_W = ATTN_HEADS * HEAD_DIM
KV_W = ATTN_KV_HEADS * HEAD_DIM
GLA_W = GLA_HEADS * HEAD_DIM
MLSTM_W = MLSTM_HEADS * HEAD_DIM
QK_W = ATTN_W + KV_W
QKV_W = ATTN_W + 2 * KV_W
D_FF = 2816
CONV_WIDTH = 3
GATE_W = 128
Q_SCALE = HEAD_DIM ** -0.5
LOG2_E = 1.4426950408889634

LANE = 128
SUBLANE = 8
VMEM_LIMIT = 56 * 1024 * 1024

ROW_TILE = 512
REC_BLOCK = 256
REC_BATCH = 4
ATTN_TQ = 512
ATTN_STREAM_LANES = 256
ATTN_TK = 256
VT_ROWS = HEAD_DIM + 16
FF_BLOCK = 1408
FF_SUB = 1408
HALO = SUBLANE

NT_DIMS = (((1,), (1,)), ((), ()))
TN_DIMS = (((0,), (0,)), ((), ()))


def _params(*sem):
    return pltpu.CompilerParams(dimension_semantics=sem, vmem_limit_bytes=VMEM_LIMIT)


def _log_sigmoid(z):
    return jnp.minimum(z, 0.0) - jnp.log(1.0 + jnp.exp(-jnp.abs(z)))


def _sigmoid(z):
    return 1.0 / (1.0 + jnp.exp(-z))


def _dot(a, b):
    return jnp.dot(a.astype(BF16), b.astype(BF16), preferred_element_type=F32)


def _dot_dims(a, b, dims):
    return lax.dot_general(a.astype(BF16), b.astype(BF16), dims, preferred_element_type=F32)


def _split3(x):
    p1 = x.astype(BF16)
    rest = x - p1.astype(F32)
    p2 = rest.astype(BF16)
    return p1, p2, (rest - p2.astype(F32)).astype(BF16)


def _cumsum_cols(tbd, x):
    return sum(jnp.dot(tbd, p, preferred_element_type=F32) for p in _split3(x))


def _cumsum_rows(x, tbd_t):
    return sum(jnp.dot(p, tbd_t, preferred_element_type=F32) for p in _split3(x))


def _dot_split(a, b_hi, b_lo):
    a_hi = a.astype(BF16)
    a_lo = (a - a_hi.astype(F32)).astype(BF16)
    return (jnp.dot(a_hi, b_hi, preferred_element_type=F32)
            + jnp.dot(a_lo, b_hi, preferred_element_type=F32)
            + jnp.dot(a_hi, b_lo, preferred_element_type=F32))


def _group_mean(sq, bd):
    hi = sq.astype(BF16)
    lo = (sq - hi.astype(F32)).astype(BF16)
    return (jnp.dot(hi, bd, preferred_element_type=F32)
            + jnp.dot(lo, bd, preferred_element_type=F32))


def _inproj_kernel(x_ref, g_ref, w_ref, qkv_ref, gla_ref, mls_ref, gate_ref):
    x = x_ref[0]
    ms = jnp.mean(x * x, axis=-1, keepdims=True)
    h = (x * lax.rsqrt(ms + EPS) * g_ref[...]).astype(BF16)
    col = 0
    for ref in (qkv_ref, gla_ref, mls_ref, gate_ref):
        width = ref.shape[-1]
        ref[0] = jnp.dot(h, w_ref[:, col:col + width], preferred_element_type=F32)
        col += width


def _inproj(x, g, w):
    b, s, d = x.shape
    widths = (QKV_W, 4 * GLA_W, 4 * MLSTM_W, GATE_W)
    row = lambda n: pl.BlockSpec((1, ROW_TILE, n), lambda bi, i: (bi, i, 0))
    const = lambda shape: pl.BlockSpec(shape, lambda bi, i: (0,) * len(shape))
    return pl.pallas_call(
        _inproj_kernel,
        grid=(b, s // ROW_TILE),
        in_specs=[row(d), const((1, d)), const(w.shape)],
        out_specs=[row(n) for n in widths],
        out_shape=[jax.ShapeDtypeStruct((b, s, n), F32) for n in widths],
        compiler_params=_params("parallel", "parallel"),
        name="inproj",
    )(x, g, w)


def _qkprep_kernel(qkv_ref, cos_ref, sin_ref, gain_ref, bd_ref, qt_ref, k_ref, vt_ref):
    x = qkv_ref[0]
    qk = x[:, :QK_W]
    ms = _group_mean(qk * qk, bd_ref[...])
    y = qk * lax.rsqrt(ms + EPS) * gain_ref[...]
    lane = lax.broadcasted_iota(jnp.int32, y.shape, 1)
    first_half = (lane & 31) < 16
    quarter = HEAD_DIM // 4
    swapped = jnp.where(first_half, pltpu.roll(y, QK_W - quarter, 1), pltpu.roll(y, quarter, 1))
    r = y * cos_ref[...] + swapped * sin_ref[...]
    q_t = (r[:, :ATTN_W] * (Q_SCALE * LOG2_E)).T.astype(BF16)
    qt_ref[0] = q_t.reshape(ATTN_HEADS, HEAD_DIM, q_t.shape[-1])
    v_t = x[:, QK_W:].T.astype(BF16)
    v_t = v_t.reshape(ATTN_KV_HEADS, HEAD_DIM, v_t.shape[-1])
    row = lax.broadcasted_iota(jnp.int32, (ATTN_KV_HEADS, VT_ROWS - HEAD_DIM, v_t.shape[-1]), 1)
    vt_ref[0] = jnp.concatenate([v_t, jnp.where(row == 0, 1.0, 0.0).astype(BF16)], axis=1)
    for h in range(ATTN_KV_HEADS):
        k_ref[0, h] = r[:, ATTN_W + h * HEAD_DIM:ATTN_W + (h + 1) * HEAD_DIM].astype(BF16)


def _qkprep(qkv, cos_t, sin_t, gain, bd):
    b, s, _ = qkv.shape
    const = lambda shape: pl.BlockSpec(shape, lambda bi, i: (0,) * len(shape))
    tab = pl.BlockSpec((ROW_TILE, QK_W), lambda bi, i: (i, 0))
    t_spec = lambda heads, rows: pl.BlockSpec((1, heads, rows, ROW_TILE), lambda bi, i: (bi, 0, 0, i))
    t_shape = lambda heads, rows: jax.ShapeDtypeStruct((b, heads, rows, s), BF16)
    return pl.pallas_call(
        _qkprep_kernel,
        grid=(b, s // ROW_TILE),
        in_specs=[pl.BlockSpec((1, ROW_TILE, QKV_W), lambda bi, i: (bi, i, 0)),
                  tab, tab, const((1, QK_W)), const((QK_W, QK_W))],
        out_specs=[t_spec(ATTN_HEADS, HEAD_DIM),
                   pl.BlockSpec((1, ATTN_KV_HEADS, ROW_TILE, HEAD_DIM), lambda bi, i: (bi, 0, i, 0)),
                   t_spec(ATTN_KV_HEADS, VT_ROWS)],
        out_shape=[t_shape(ATTN_HEADS, HEAD_DIM),
                   jax.ShapeDtypeStruct((b, ATTN_KV_HEADS, s, HEAD_DIM), BF16),
                   t_shape(ATTN_KV_HEADS, VT_ROWS)],
        compiler_params=_params("parallel", "parallel"),
        name="qkprep",
    )(qkv, cos_t, sin_t, gain, bd)


def _attn_kernel(qt_ref, k_ref, vt_ref, o_ref):
    tq = qt_ref.shape[-1]
    n_chunks = k_ref.shape[2] // ATTN_TK
    keys = lambda c: slice(c * ATTN_TK, (c + 1) * ATTN_TK)

    def stream(head, lo):
        q_t = qt_ref[0, head, :, lo:lo + ATTN_STREAM_LANES]
        m = jnp.full((1, ATTN_STREAM_LANES), -jnp.inf, F32)
        acc = jnp.zeros((VT_ROWS, ATTN_STREAM_LANES), F32)

        def scores(c):
            s = jnp.dot(k_ref[0, 0, keys(c), :], q_t, preferred_element_type=F32)
            return s, jnp.max(s, axis=0, keepdims=True)

        nxt = scores(0)
        yield
        for c in range(n_chunks):
            s_t, s_max = nxt
            if c + 1 < n_chunks:
                nxt = scores(c + 1)
            m_new = jnp.maximum(m, s_max)
            yield
            p_t = jnp.exp2((s_t - m_new).astype(BF16))
            yield
            acc = (jnp.exp2(m - m_new) * acc
                   + jnp.dot(vt_ref[0, 0, :, keys(c)], p_t, preferred_element_type=F32))
            m = m_new
            yield
        out = (acc[:HEAD_DIM] / acc[HEAD_DIM:HEAD_DIM + 1]).T
        o_ref[0, lo:lo + ATTN_STREAM_LANES, head * HEAD_DIM:(head + 1) * HEAD_DIM] = out.astype(o_ref.dtype)

    _interleave(*[stream(j, lo) for j in range(ATTN_GROUP) for lo in range(0, tq, ATTN_STREAM_LANES)])


def _attention(q_t, k, v_t):
    b, _, _, s = q_t.shape
    gw = ATTN_GROUP * HEAD_DIM
    return pl.pallas_call(
        _attn_kernel,
        grid=(b, ATTN_KV_HEADS, s // ATTN_TQ),
        in_specs=[pl.BlockSpec((1, ATTN_GROUP, HEAD_DIM, ATTN_TQ), lambda bi, h, i: (bi, h, 0, i)),
                  pl.BlockSpec((1, 1, s, HEAD_DIM), lambda bi, h, i: (bi, h, 0, 0)),
                  pl.BlockSpec((1, 1, VT_ROWS, s), lambda bi, h, i: (bi, h, 0, 0))],
        out_specs=pl.BlockSpec((1, ATTN_TQ, gw), lambda bi, h, i: (bi, i, h)),
        out_shape=jax.ShapeDtypeStruct((b, s, ATTN_W), BF16),
        compiler_params=_params("parallel", "parallel", "arbitrary"),
        name="attention",
    )(q_t, k, v_t)


def _chunk_order(d):
    n_chunks = REC_BLOCK // CHUNK
    return range(n_chunks) if d == 0 else range(n_chunks - 1, -1, -1)


def _glagate_kernel(a_ref, w2hi_ref, w2lo_ref, gb_ref, tbd_ref, cf_ref, cb_ref):
    a = a_ref[0]
    for d, out_ref in enumerate((cf_ref, cb_ref)):
        z = _dot_split(a, w2hi_ref[d], w2lo_ref[d]) + gb_ref[d]
        g = _log_sigmoid(z) * (1.0 / GLA_TAU)
        for sb in range(a.shape[0] // REC_BLOCK):
            rows = slice(sb * REC_BLOCK, (sb + 1) * REC_BLOCK)
            out_ref[0, rows, :] = _cumsum_cols(tbd_ref[d], g[rows])


def _glagate(gate_proj, w2_hi, w2_lo, gb, tbd):
    b, s, _ = gate_proj.shape
    const = lambda shape: pl.BlockSpec(shape, lambda bi, i: (0,) * len(shape))
    out_spec = pl.BlockSpec((1, ROW_TILE, GLA_W), lambda bi, i: (bi, i, 0))
    out_shape = jax.ShapeDtypeStruct((b, s, GLA_W), F32)
    return pl.pallas_call(
        _glagate_kernel,
        grid=(b, s // ROW_TILE),
        in_specs=[pl.BlockSpec((1, ROW_TILE, GATE_W), lambda bi, i: (bi, i, 0)),
                  const(w2_hi.shape), const(w2_lo.shape), const(gb.shape), const(tbd.shape)],
        out_specs=[out_spec, out_spec],
        out_shape=[out_shape, out_shape],
        compiler_params=_params("parallel", "parallel"),
        name="glagate",
    )(gate_proj, w2_hi, w2_lo, gb, tbd)


def _gla_direction(q_ref, k_ref, v_ref, cum_ref, o_ref, st_ref, d, n):
    causal_t, _, head_lane, block_diag = _head_layout_masks(d)
    last = CHUNK - 1 if d == 0 else 0
    st = st_ref[n, d]
    for i in _chunk_order(d):
        rows = slice(i * CHUNK, (i + 1) * CHUNK)
        cum = cum_ref[n, rows, :]
        total = cum[last:last + 1]
        ref = cum[CHUNK // 2:CHUNK // 2 + 1]
        qt = q_ref[n, rows, :] * Q_SCALE * jnp.exp(cum - ref)
        kt = k_ref[n, rows, :] * jnp.exp(ref - cum)
        qe = qt * jnp.exp(ref)
        kd = kt * jnp.exp(total - ref)
        v = v_ref[n, rows, :]
        q4 = jnp.concatenate([jnp.where(head_lane[h], qt, 0.0) for h in range(GLA_HEADS)], axis=0)
        att_t = jnp.where(causal_t, _dot_dims(kt, q4, NT_DIMS), 0.0)
        yield
        full = _dot_dims(att_t, v, TN_DIMS)
        out = _dot_dims(qe, st, NT_DIMS)
        yield
        for h in range(GLA_HEADS):
            out = out + jnp.where(head_lane[h], full[h * CHUNK:(h + 1) * CHUNK], 0.0)
        o_ref[n, rows, :] = out
        st = st * jnp.exp(total) + jnp.where(block_diag, _dot_dims(v, kd, TN_DIMS), 0.0)
        yield
    st_ref[n, d] = st


def _interleave(*stages):
    stages = list(stages)
    while stages:
        for g in list(stages):
            if next(g, StopIteration) is StopIteration:
                stages.remove(g)


def _gla_kernel(qf_ref, kf_ref, vf_ref, cf_ref, qb_ref, kb_ref, vb_ref, cb_ref, of_ref, ob_ref, st_ref):
    @pl.when(pl.program_id(1) == 0)
    def _():
        st_ref[...] = jnp.zeros(st_ref.shape, F32)

    stages = []
    for n in range(qf_ref.shape[0]):
        stages.append(_gla_direction(qf_ref, kf_ref, vf_ref, cf_ref, of_ref, st_ref, 0, n))
        stages.append(_gla_direction(qb_ref, kb_ref, vb_ref, cb_ref, ob_ref, st_ref, 1, n))
    _interleave(*stages)


def _gla(gla_proj, cum_f, cum_b):
    b, s, _ = gla_proj.shape
    nb = s // REC_BLOCK
    fwd = lambda col: pl.BlockSpec((REC_BATCH, REC_BLOCK, GLA_W), lambda bi, j: (bi, j, col))
    bwd = lambda col: pl.BlockSpec((REC_BATCH, REC_BLOCK, GLA_W), lambda bi, j: (bi, nb - 1 - j, col))
    out_shape = jax.ShapeDtypeStruct((b, s, GLA_W), F32)
    return pl.pallas_call(
        _gla_kernel,
        grid=(b // REC_BATCH, nb),
        in_specs=[fwd(0), fwd(1), fwd(2), fwd(0), bwd(0), bwd(1), bwd(2), bwd(0)],
        out_specs=[fwd(0), bwd(0)],
        out_shape=[out_shape, out_shape],
        scratch_shapes=[pltpu.VMEM((REC_BATCH, 2, GLA_W, GLA_W), F32)],
        compiler_params=_params("parallel", "arbitrary"),
        name="gla",
    )(gla_proj, gla_proj, gla_proj, cum_f, gla_proj, gla_proj, gla_proj, cum_b)


def _head_layout_masks(d):
    sub = lax.broadcasted_iota(jnp.int32, (CHUNK, MLSTM_W), 0)
    lane = lax.broadcasted_iota(jnp.int32, (CHUNK, MLSTM_W), 1)
    t_lane = lane & (HEAD_DIM - 1)
    causal_t = (sub <= t_lane) if d == 0 else (sub >= t_lane)
    eye_t = sub == t_lane
    head_lane = [(lane >> 6) == h for h in range(MLSTM_HEADS)]
    brow = lax.broadcasted_iota(jnp.int32, (MLSTM_W, MLSTM_W), 0) >> 6
    bcol = lax.broadcasted_iota(jnp.int32, (MLSTM_W, MLSTM_W), 1) >> 6
    return causal_t, eye_t, head_lane, brow == bcol


def _expand3(x, e):
    return sum(jnp.dot(p, e, preferred_element_type=F32) for p in _split3(x))


def _mlstm_direction(q_ref, k_ref, v_ref, gates, frow, h_ref, tbd, tbd_t, e_r, e_f, e_h,
                     ct_ref, n_ref, m_ref, d, n):
    causal_t, eye_t, head_lane, block_diag = _head_layout_masks(d)
    last = CHUNK - 1 if d == 0 else 0
    n_chunks = REC_BLOCK // CHUNK
    lane = lax.broadcasted_iota(jnp.int32, gates.shape, 1)
    i_lo = 2 * GLA_RANK + d * MLSTM_HEADS
    is_i_lane = (lane >= i_lo) & (lane < i_lo + MLSTM_HEADS)
    fcum = _cumsum_cols(tbd, _log_sigmoid(gates))
    r_exp = _expand3(jnp.where(is_i_lane, gates, fcum), e_r)
    f_last = jnp.concatenate([fcum[c * CHUNK + last:c * CHUNK + last + 1] for c in range(n_chunks)]
                             + [jnp.zeros((SUBLANE - n_chunks, fcum.shape[1]), F32)], axis=0)
    f_tot_rows = _expand3(f_last, e_f)
    fc_rows = _cumsum_rows(_log_sigmoid(frow), tbd_t)

    ct, n_row, m_row = ct_ref[n, d], n_ref[n, d], m_ref[n, d]
    for c in _chunk_order(d):
        rows = slice(c * CHUNK, (c + 1) * CHUNK)
        q, ks, v = q_ref[n, rows, :], k_ref[n, rows, :] * Q_SCALE, v_ref[n, rows, :]
        q4 = jnp.concatenate([jnp.where(head_lane[h], q, 0.0) for h in range(MLSTM_HEADS)], axis=0)
        kn = jnp.concatenate([ks, jnp.broadcast_to(n_row, (2 * SUBLANE, MLSTM_W))], axis=0)
        z = _dot_dims(kn, q4, NT_DIMS)
        s_t, qn = z[:CHUNK], z[CHUNK:CHUNK + 1]
        yield
        rc = r_exp[rows]
        rmax_t = jnp.max(jnp.where(causal_t, rc, -jnp.inf), axis=0, keepdims=True)
        rmax = jnp.max(rc, axis=0, keepdims=True)
        p_t = s_t * jnp.exp(jnp.where(causal_t, rc - rmax_t, -jnp.inf))
        row_sum = jnp.sum(p_t, axis=0, keepdims=True)
        kw = ks * jnp.exp(rc - rmax)
        fc_row = fc_rows[c:c + 1]
        inter = fc_row + m_row
        intra = fc_row + rmax_t
        m_t = jnp.maximum(inter, intra)
        w_inter, w_intra = jnp.exp(inter - m_t), jnp.exp(intra - m_t)
        den = w_inter * qn + w_intra * row_sum
        inv = 1.0 / jnp.maximum(jnp.abs(den), jnp.exp(-m_t))
        full = _dot_dims(p_t * (w_intra * inv), v, TN_DIMS)
        yield
        out = _expand3(jnp.where(eye_t, w_inter * inv, 0.0), e_h) * _dot(q, ct)
        for h in range(MLSTM_HEADS):
            out = out + jnp.where(head_lane[h], full[h * CHUNK:(h + 1) * CHUNK], 0.0)
        h_ref[n, rows, :] = out
        m_max = jnp.maximum(m_row, rmax)
        keep, add = jnp.exp(m_row - m_max), jnp.exp(rmax - m_max)
        ct = keep * ct + add * jnp.where(block_diag, _dot_dims(kw, v, TN_DIMS), 0.0)
        n_row = keep * n_row + add * jnp.sum(kw, axis=0, keepdims=True)
        m_row = f_tot_rows[c:c + 1] + m_max
        yield
    ct_ref[n, d], n_ref[n, d], m_ref[n, d] = ct, n_row, m_row


def _mlstm_kernel(qf_ref, kf_ref, vf_ref, gcf_ref, grf_ref, qb_ref, kb_ref, vb_ref, gcb_ref, grb_ref,
                  bcol_ref, brow_ref, tbd_ref, tbdt_ref, er_ref, ef_ref, eh_ref,
                  hf_ref, hb_ref, ct_ref, n_ref, m_ref):
    @pl.when(pl.program_id(1) == 0)
    def _():
        ct_ref[...] = jnp.zeros(ct_ref.shape, F32)
        n_ref[...] = jnp.zeros(n_ref.shape, F32)
        m_ref[...] = jnp.zeros(m_ref.shape, F32)

    stages = []
    for n in range(qf_ref.shape[0]):
        stages.append(_mlstm_direction(
            qf_ref, kf_ref, vf_ref, gcf_ref[n] + bcol_ref[...], grf_ref[n, 0, 0] + brow_ref[0],
            hf_ref, tbd_ref[0], tbdt_ref[0], er_ref[0], ef_ref[0], eh_ref[...], ct_ref, n_ref, m_ref, 0, n))
        stages.append(_mlstm_direction(
            qb_ref, kb_ref, vb_ref, gcb_ref[n] + bcol_ref[...], grb_ref[n, 0, 0] + brow_ref[1],
            hb_ref, tbd_ref[1], tbdt_ref[1], er_ref[1], ef_ref[1], eh_ref[...], ct_ref, n_ref, m_ref, 1, n))
    _interleave(*stages)


def _mlstm(mls_proj, gate_proj, f_rows, bcol, brow, tbd, tbd_t, e_r, e_f, e_h):
    b, s, _ = mls_proj.shape
    nb = s // REC_BLOCK
    fwd = lambda col: pl.BlockSpec((REC_BATCH, REC_BLOCK, MLSTM_W), lambda bi, j: (bi, j, col))
    bwd = lambda col: pl.BlockSpec((REC_BATCH, REC_BLOCK, MLSTM_W), lambda bi, j: (bi, nb - 1 - j, col))
    const = lambda shape: pl.BlockSpec(shape, lambda bi, j: (0,) * len(shape))
    gcol_f = pl.BlockSpec((REC_BATCH, REC_BLOCK, GATE_W), lambda bi, j: (bi, j, 0))
    gcol_b = pl.BlockSpec((REC_BATCH, REC_BLOCK, GATE_W), lambda bi, j: (bi, nb - 1 - j, 0))
    frow_shape = (REC_BATCH, 1, 1) + f_rows.shape[3:]
    frow_f = pl.BlockSpec(frow_shape, lambda bi, j: (bi, 0, j, 0, 0))
    frow_b = pl.BlockSpec(frow_shape, lambda bi, j: (bi, 1, nb - 1 - j, 0, 0))
    out_shape = jax.ShapeDtypeStruct((b, s, MLSTM_W), F32)
    return pl.pallas_call(
        _mlstm_kernel,
        grid=(b // REC_BATCH, nb),
        in_specs=[fwd(0), fwd(1), fwd(2), gcol_f, frow_f, bwd(0), bwd(1), bwd(2), gcol_b, frow_b,
                  const(bcol.shape), const(brow.shape), const(tbd.shape), const(tbd_t.shape),
                  const(e_r.shape), const(e_f.shape), const(e_h.shape)],
        out_specs=[fwd(0), bwd(0)],
        out_shape=[out_shape, out_shape],
        scratch_shapes=[pltpu.VMEM((REC_BATCH, 2, MLSTM_W, MLSTM_W), F32),
                        pltpu.VMEM((REC_BATCH, 2, 1, MLSTM_W), F32),
                        pltpu.VMEM((REC_BATCH, 2, 1, MLSTM_W), F32)],
        compiler_params=_params("parallel", "arbitrary"),
        name="mlstm",
    )(mls_proj, mls_proj, mls_proj, gate_proj, f_rows,
      mls_proj, mls_proj, mls_proj, gate_proj, f_rows, bcol, brow, tbd, tbd_t, e_r, e_f, e_h)


def _outproj_kernel(x_ref, attn_ref, gof_ref, gob_ref, gg_ref, mhf_ref, mhb_ref, mo_ref,
                    gng_ref, mng_ref, bd_ref, wo_ref, out_ref):
    bd = bd_ref[...]

    def head_norm(t, gain):
        return t * lax.rsqrt(_group_mean(t * t, bd) + EPS) * gain

    gg = gg_ref[0]
    gla = head_norm(gof_ref[0] + gob_ref[0], gng_ref[...]) * (gg * _sigmoid(gg))
    mls = head_norm(_sigmoid(mo_ref[0]) * (mhf_ref[0] + mhb_ref[0]), mng_ref[...])
    acc = jnp.dot(attn_ref[0], wo_ref[:ATTN_W, :], preferred_element_type=F32)
    acc += jnp.dot(gla.astype(BF16), wo_ref[ATTN_W:ATTN_W + GLA_W, :], preferred_element_type=F32)
    acc += jnp.dot(mls.astype(BF16), wo_ref[ATTN_W + GLA_W:, :], preferred_element_type=F32)
    out_ref[0] = x_ref[0] + acc


def _outproj(x, attn, gof, gob, gla_proj, mhf, mhb, mls_proj, gng, mng, bd, wo):
    b, s, d = x.shape
    row = lambda n, col=0: pl.BlockSpec((1, ROW_TILE, n), lambda bi, i: (bi, i, col))
    const = lambda shape: pl.BlockSpec(shape, lambda bi, i: (0,) * len(shape))
    return pl.pallas_call(
        _outproj_kernel,
        grid=(b, s // ROW_TILE),
        in_specs=[row(d), row(ATTN_W), row(GLA_W), row(GLA_W), row(GLA_W, 3),
                  row(MLSTM_W), row(MLSTM_W), row(MLSTM_W, 3),
                  const((1, GLA_W)), const((1, MLSTM_W)), const(bd.shape), const(wo.shape)],
        out_specs=row(d),
        out_shape=jax.ShapeDtypeStruct((b, s, d), F32),
        compiler_params=_params("parallel", "parallel"),
        name="outproj",
    )(x, attn, gof, gob, gla_proj, mhf, mhb, mls_proj, gng, mng, bd, wo)


def _ffn_kernel(x_ref, prev_ref, next_ref, g_ref, wg_ref, wv_ref, cwg_ref, cwv_ref, cbg_ref, cbv_ref,
                wd_ref, out_ref, h_ref, acc_ref):
    i, j = pl.program_id(1), pl.program_id(2)
    tm = x_ref.shape[1]

    def normed(t):
        ms = jnp.mean(t * t, axis=-1, keepdims=True)
        return t * lax.rsqrt(ms + EPS) * g_ref[...]

    @pl.when(j == 0)
    def _():
        keep_prev = (i > 0).astype(F32)
        keep_next = (i < pl.num_programs(1) - 1).astype(F32)
        h_ref[0:HALO, :] = (normed(prev_ref[0]) * keep_prev).astype(BF16)
        h_ref[HALO:HALO + tm, :] = normed(x_ref[0]).astype(BF16)
        h_ref[HALO + tm:, :] = (normed(next_ref[0]) * keep_next).astype(BF16)
        acc_ref[...] = jnp.zeros(acc_ref.shape, F32)

    h = h_ref[...]
    n_rows = h.shape[0]

    def up(cols):
        return (jnp.dot(h, wg_ref[:, cols], preferred_element_type=F32),
                jnp.dot(h, wv_ref[:, cols], preferred_element_type=F32))

    def conv(u, cw_ref, cb_ref, cols):
        out = (cb_ref[:, cols] + pltpu.roll(u, 1, 0) * cw_ref[0:1, cols] + u * cw_ref[1:2, cols]
               + pltpu.roll(u, n_rows - 1, 0) * cw_ref[2:3, cols])
        return out[HALO:HALO + tm]

    bounds = list(range(0, FF_BLOCK, FF_SUB)) + [FF_BLOCK]
    subs = [slice(lo, hi) for lo, hi in zip(bounds[:-1], bounds[1:])]
    acc = acc_ref[...]
    nxt = up(subs[0])
    for n, cols in enumerate(subs):
        ug, uv = nxt
        if n + 1 < len(subs):
            nxt = up(subs[n + 1])
        gate = conv(ug, cwg_ref, cbg_ref, cols)
        val = conv(uv, cwv_ref, cbv_ref, cols)
        act = (gate * _sigmoid(gate) * val).astype(BF16)
        acc = acc + jnp.dot(act, wd_ref[cols, :], preferred_element_type=F32)
    acc_ref[...] = acc

    @pl.when(j == pl.num_programs(2) - 1)
    def _():
        out_ref[0] = x_ref[0] + acc_ref[...]


def _ffn(x, g, w_up, conv_w, conv_b, w_down):
    b, s, d = x.shape
    nj = D_FF // FF_BLOCK
    tiles_per_halo = ROW_TILE // HALO
    last_halo = s // HALO - 1
    row = pl.BlockSpec((1, ROW_TILE, d), lambda bi, i, j: (bi, i, 0))
    prev = pl.BlockSpec((1, HALO, d), lambda bi, i, j: (bi, jnp.maximum(i * tiles_per_halo - 1, 0), 0))
    nxt = pl.BlockSpec((1, HALO, d),
                       lambda bi, i, j: (bi, jnp.minimum((i + 1) * tiles_per_halo, last_halo), 0))
    gate_cols = lambda rows: pl.BlockSpec((rows, FF_BLOCK), lambda bi, i, j: (0, j))
    val_cols = lambda rows: pl.BlockSpec((rows, FF_BLOCK), lambda bi, i, j: (0, nj + j))
    return pl.pallas_call(
        _ffn_kernel,
        grid=(b, s // ROW_TILE, nj),
        in_specs=[row, prev, nxt, pl.BlockSpec((1, d), lambda bi, i, j: (0, 0)),
                  gate_cols(d), val_cols(d), gate_cols(CONV_WIDTH), val_cols(CONV_WIDTH),
                  gate_cols(1), val_cols(1),
                  pl.BlockSpec((FF_BLOCK, d), lambda bi, i, j: (j, 0))],
        out_specs=row,
        out_shape=jax.ShapeDtypeStruct((b, s, d), F32),
        scratch_shapes=[pltpu.VMEM((ROW_TILE + 2 * HALO, d), BF16),
                        pltpu.VMEM((ROW_TILE, d), F32)],
        compiler_params=_params("parallel", "parallel", "arbitrary"),
        name="ffn",
    )(x, x, x, g, w_up, w_up, conv_w, conv_w, conv_b, conv_b, w_down)


def _rope_tables(seq):
    n_rows = seq // GRID_W
    row = jnp.repeat(jnp.arange(n_rows, dtype=F32), GRID_W)
    col = jnp.tile(jnp.arange(GRID_W, dtype=F32), n_rows)
    n_freq = HEAD_DIM // 4
    inv_freq = jnp.power(ROPE_THETA, -jnp.arange(n_freq, dtype=F32) / n_freq)
    ang_r, ang_c = row[:, None] * inv_freq, col[:, None] * inv_freq
    cos_h = jnp.concatenate([jnp.cos(ang_r), jnp.cos(ang_r), jnp.cos(ang_c), jnp.cos(ang_c)], axis=1)
    sin_h = jnp.concatenate([-jnp.sin(ang_r), jnp.sin(ang_r), -jnp.sin(ang_c), jnp.sin(ang_c)], axis=1)
    reps = QK_W // HEAD_DIM
    return jnp.tile(cos_h, (1, reps)), jnp.tile(sin_h, (1, reps))


def _block_diag_mean(width):
    group = np.arange(width) // HEAD_DIM
    return jnp.asarray((group[:, None] == group[None, :]).astype(np.float32) / HEAD_DIM, dtype=BF16)


def _tri_constants():
    t = np.arange(REC_BLOCK)
    same_chunk = (t[:, None] // CHUNK) == (t[None, :] // CHUNK)
    prefix = (same_chunk & (t[None, :] <= t[:, None])).astype(np.float32)
    suffix = (same_chunk & (t[None, :] >= t[:, None])).astype(np.float32)
    tbd = jnp.asarray(np.stack([prefix, suffix]), dtype=BF16)
    tbd_t = jnp.asarray(np.stack([prefix.T, suffix.T]), dtype=BF16)
    return tbd, tbd_t


def _mlstm_expanders():
    e_r = np.zeros((2, GATE_W, MLSTM_W), np.float32)
    e_f = np.zeros((2, GATE_W, MLSTM_W), np.float32)
    for d in range(2):
        for h in range(MLSTM_HEADS):
            lanes = slice(h * HEAD_DIM, (h + 1) * HEAD_DIM)
            i_lane = 2 * GLA_RANK + d * MLSTM_HEADS + h
            f_lane = i_lane + 2 * MLSTM_HEADS
            e_r[d, i_lane, lanes] = 1.0
            e_r[d, f_lane, lanes] = -1.0
            e_f[d, f_lane, lanes] = 1.0
    group = np.arange(MLSTM_W) // HEAD_DIM
    e_h = (group[:, None] == group[None, :]).astype(np.float32)
    return jnp.asarray(e_r, dtype=BF16), jnp.asarray(e_f, dtype=BF16), jnp.asarray(e_h, dtype=BF16)


def _layer(x, consts, norm_mix_g, w_in, attn_qn_g, attn_kn_g, gla_gate_w2, gla_gate_b, gla_norm_g,
           mlstm_gate_b, mlstm_norm_g, w_out, norm_ffn_g, w_up, conv_w, conv_b, w_down):
    cos_t, sin_t, bd_qk, bd_head, tbd, tbd_t, e_r, e_f, e_h = consts
    b, s, _ = x.shape
    n_gates = 4 * MLSTM_HEADS
    main_w = QKV_W + 4 * GLA_W
    gla_a = w_in[:, main_w:main_w + 2 * GLA_RANK]
    mls_main = w_in[:, main_w + 2 * GLA_RANK:main_w + 2 * GLA_RANK + 4 * MLSTM_W]
    mls_g = w_in[:, main_w + 2 * GLA_RANK + 4 * MLSTM_W:]
    pad = jnp.zeros((D_MODEL, GATE_W - 2 * GLA_RANK - n_gates), F32)
    w_in_r = jnp.concatenate([w_in[:, :main_w], mls_main, gla_a, mls_g, pad], axis=1).astype(BF16)

    qkv, gla_proj, mls_proj, gate_proj = _inproj(x, norm_mix_g[None, :], w_in_r)

    qk_gain = jnp.concatenate([jnp.tile(attn_qn_g, ATTN_HEADS), jnp.tile(attn_kn_g, ATTN_KV_HEADS)])[None, :]
    q_t, k, v_t = _qkprep(qkv, cos_t, sin_t, qk_gain, bd_qk)
    attn = _attention(q_t, k, v_t)

    w2ext = jnp.zeros((2, GATE_W, GLA_W), F32)
    w2ext = w2ext.at[0, :GLA_RANK].set(gla_gate_w2[0]).at[1, GLA_RANK:2 * GLA_RANK].set(gla_gate_w2[1])
    w2_hi = w2ext.astype(BF16)
    w2_lo = (w2ext - w2_hi.astype(F32)).astype(BF16)
    cum_f, cum_b = _glagate(gate_proj, w2_hi, w2_lo, gla_gate_b[:, None, :], tbd)
    gof, gob = _gla(gla_proj, cum_f, cum_b)

    nb, cpb = s // REC_BLOCK, REC_BLOCK // CHUNK
    f_lo = 2 * GLA_RANK + 2 * MLSTM_HEADS
    f_rows = gate_proj[:, :, f_lo:f_lo + 2 * MLSTM_HEADS].reshape(b, nb, cpb, CHUNK, 2, MLSTM_HEADS)
    f_rows = f_rows.transpose(0, 4, 1, 2, 5, 3).reshape(b, 2, nb, cpb, MLSTM_W)
    f_rows = jnp.pad(f_rows, ((0, 0), (0, 0), (0, 0), (0, SUBLANE - cpb), (0, 0)))
    bcol = jnp.zeros((1, GATE_W), F32).at[0, 2 * GLA_RANK:2 * GLA_RANK + n_gates].set(
        mlstm_gate_b.reshape(n_gates))
    brow = jnp.repeat(mlstm_gate_b[2:], HEAD_DIM, axis=1)[:, None, :]
    mhf, mhb = _mlstm(mls_proj, gate_proj, f_rows, bcol, brow, tbd, tbd_t, e_r, e_f, e_h)

    x = _outproj(x, attn, gof, gob, gla_proj, mhf, mhb, mls_proj,
                 jnp.tile(gla_norm_g, GLA_HEADS)[None, :], jnp.tile(mlstm_norm_g, MLSTM_HEADS)[None, :],
                 bd_head, w_out.astype(BF16))
    return _ffn(x, norm_ffn_g[None, :], w_up.astype(BF16), conv_w, conv_b[None, :], w_down.astype(BF16))


def _trunk(x, weights):
    seq = x.shape[1]
    cos_t, sin_t = _rope_tables(seq)
    consts = ((cos_t, sin_t, _block_diag_mean(QK_W), _block_diag_mean(GLA_W))
              + _tri_constants() + _mlstm_expanders())
    depth = weights[0].shape[0]
    for l in range(depth):
        x = _layer(x, consts, *[w[l] for w in weights])
    return x


def kernel(x_prompt, x_sample, norm_mix_g, w_in, attn_qn_g, attn_kn_g, gla_gate_w2, gla_gate_b,
           gla_norm_g, mlstm_gate_b, mlstm_norm_g, w_out, norm_ffn_g, w_up, conv_w, conv_b, w_down):
    weights = (norm_mix_g, w_in, attn_qn_g, attn_kn_g, gla_gate_w2, gla_gate_b, gla_norm_g,
               mlstm_gate_b, mlstm_norm_g, w_out, norm_ffn_g, w_up, conv_w, conv_b, w_down)
    return _trunk(x_prompt, weights), _trunk(x_sample, weights)
```

```python
import functools

import jax
import jax.numpy as jnp
import numpy as np
from jax import lax
from jax.experimental import pallas as pl
from jax.experimental.pallas import tpu as pltpu

F32 = jnp.float32
BF16 = jnp.bfloat16

D_MODEL = 1024
GRID_W = 64
HEAD_DIM = 64
EPS = 1e-6
ATTN_HEADS = 8
ATTN_KV_HEADS = 2
ATTN_GROUP = ATTN_HEADS // ATTN_KV_HEADS
ROPE_THETA = 10000.0
GLA_HEADS = 4
GLA_RANK = 16
GLA_TAU = 16.0
MLSTM_HEADS = 4
CHUNK = 64
ATTN_W = ATTN_HEADS * HEAD_DIM
KV_W = ATTN_KV_HEADS * HEAD_DIM
GLA_W = GLA_HEADS * HEAD_DIM
MLSTM_W = MLSTM_HEADS * HEAD_DIM
QK_W = ATTN_W + KV_W
QKV_W = ATTN_W + 2 * KV_W
D_FF = 2816
CONV_WIDTH = 3
GATE_W = 128
Q_SCALE = HEAD_DIM ** -0.5
LOG2_E = 1.4426950408889634

LANE = 128
SUBLANE = 8
VMEM_LIMIT = 56 * 1024 * 1024

ROW_TILE = 512
REC_BLOCK = 256
REC_BATCH = 4
ATTN_TQ = 512
ATTN_STREAM_LANES = 256
ATTN_TK = 256
VT_ROWS = HEAD_DIM + 16
KEY_W = 128
ATTN_MAX_SCORE_BOUND = 32.0
FF_BLOCK = 1408
HALO = SUBLANE

NT_DIMS = (((1,), (1,)), ((), ()))
TN_DIMS = (((0,), (0,)), ((), ()))


def _params(*sem):
    return pltpu.CompilerParams(dimension_semantics=sem, vmem_limit_bytes=VMEM_LIMIT)


def _log_sigmoid(z):
    return jnp.minimum(z, 0.0) - jnp.log(1.0 + jnp.exp(-jnp.abs(z)))


def _sigmoid(z):
    return 1.0 / (1.0 + jnp.exp(-z))


def _dot(a, b):
    return jnp.dot(a.astype(BF16), b.astype(BF16), preferred_element_type=F32)


def _dot_dims(a, b, dims):
    return lax.dot_general(a.astype(BF16), b.astype(BF16), dims, preferred_element_type=F32)


def _split3(x):
    p1 = x.astype(BF16)
    rest = x - p1.astype(F32)
    p2 = rest.astype(BF16)
    return p1, p2, (rest - p2.astype(F32)).astype(BF16)


def _cumsum_cols(tbd, x):
    return sum(jnp.dot(tbd, p, preferred_element_type=F32) for p in _split3(x))


def _cumsum_rows(x, tbd_t):
    return sum(jnp.dot(p, tbd_t, preferred_element_type=F32) for p in _split3(x))


def _dot_split(a, b_hi, b_lo):
    a_hi = a.astype(BF16)
    a_lo = (a - a_hi.astype(F32)).astype(BF16)
    return (jnp.dot(a_hi, b_hi, preferred_element_type=F32)
            + jnp.dot(a_lo, b_hi, preferred_element_type=F32)
            + jnp.dot(a_hi, b_lo, preferred_element_type=F32))


def _group_mean(sq, bd):
    hi = sq.astype(BF16)
    lo = (sq - hi.astype(F32)).astype(BF16)
    return (jnp.dot(hi, bd, preferred_element_type=F32)
            + jnp.dot(lo, bd, preferred_element_type=F32))


def _inproj_kernel(x_ref, g_ref, w_ref, qkv_ref, gla_ref, mls_ref, gate_ref):
    x = x_ref[0]
    ms = jnp.mean(x * x, axis=-1, keepdims=True)
    h = (x * lax.rsqrt(ms + EPS) * g_ref[...]).astype(BF16)
    col = 0
    for ref in (qkv_ref, gla_ref, mls_ref, gate_ref):
        width = ref.shape[-1]
        ref[0] = jnp.dot(h, w_ref[:, col:col + width], preferred_element_type=F32)
        col += width


def _inproj(x, g, w):
    b, s, d = x.shape
    widths = (QKV_W, 4 * GLA_W, 4 * MLSTM_W, GATE_W)
    row = lambda n: pl.BlockSpec((1, ROW_TILE, n), lambda bi, i: (bi, i, 0))
    const = lambda shape: pl.BlockSpec(shape, lambda bi, i: (0,) * len(shape))
    return pl.pallas_call(
        _inproj_kernel,
        grid=(b, s // ROW_TILE),
        in_specs=[row(d), const((1, d)), const(w.shape)],
        out_specs=[row(n) for n in widths],
        out_shape=[jax.ShapeDtypeStruct((b, s, n), F32) for n in widths],
        compiler_params=_params("parallel", "parallel"),
        name="inproj",
    )(x, g, w)


def _qkprep_kernel(qkv_ref, cos_ref, sin_ref, gain_ref, bd_ref, qt_ref, k_ref, vt_ref):
    x = qkv_ref[0]
    qk = x[:, :QK_W]
    ms = _group_mean(qk * qk, bd_ref[...])
    y = qk * lax.rsqrt(ms + EPS) * gain_ref[...]
    lane = lax.broadcasted_iota(jnp.int32, y.shape, 1)
    first_half = (lane & 31) < 16
    quarter = HEAD_DIM // 4
    swapped = jnp.where(first_half, pltpu.roll(y, QK_W - quarter, 1), pltpu.roll(y, quarter, 1))
    r = y * cos_ref[...] + swapped * sin_ref[...]
    q_t = (r[:, :ATTN_W] * (Q_SCALE * LOG2_E)).T.astype(BF16)
    qt_ref[0] = q_t.reshape(ATTN_HEADS, HEAD_DIM, q_t.shape[-1])
    v_t = x[:, QK_W:].T.astype(BF16)
    v_t = v_t.reshape(ATTN_KV_HEADS, HEAD_DIM, v_t.shape[-1])
    row = lax.broadcasted_iota(jnp.int32, (ATTN_KV_HEADS, VT_ROWS - HEAD_DIM, v_t.shape[-1]), 1)
    vt_ref[0] = jnp.concatenate([v_t, jnp.where(row == 0, 1.0, 0.0).astype(BF16)], axis=1)
    ones_lane = lax.broadcasted_iota(jnp.int32, (x.shape[0], KEY_W - HEAD_DIM), 1) == 0
    k_pad = jnp.where(ones_lane, 1.0, 0.0).astype(BF16)
    for h in range(ATTN_KV_HEADS):
        k_h = r[:, ATTN_W + h * HEAD_DIM:ATTN_W + (h + 1) * HEAD_DIM].astype(BF16)
        k_ref[0, h] = jnp.concatenate([k_h, k_pad], axis=1)


def _qkprep(qkv, cos_t, sin_t, gain, bd):
    b, s, _ = qkv.shape
    const = lambda shape: pl.BlockSpec(shape, lambda bi, i: (0,) * len(shape))
    tab = pl.BlockSpec((ROW_TILE, QK_W), lambda bi, i: (i, 0))
    t_spec = lambda heads, rows: pl.BlockSpec((1, heads, rows, ROW_TILE), lambda bi, i: (bi, 0, 0, i))
    t_shape = lambda heads, rows: jax.ShapeDtypeStruct((b, heads, rows, s), BF16)
    return pl.pallas_call(
        _qkprep_kernel,
        grid=(b, s // ROW_TILE),
        in_specs=[pl.BlockSpec((1, ROW_TILE, QKV_W), lambda bi, i: (bi, i, 0)),
                  tab, tab, const((1, QK_W)), const((QK_W, QK_W))],
        out_specs=[t_spec(ATTN_HEADS, HEAD_DIM),
                   pl.BlockSpec((1, ATTN_KV_HEADS, ROW_TILE, KEY_W), lambda bi, i: (bi, 0, i, 0)),
                   t_spec(ATTN_KV_HEADS, VT_ROWS)],
        out_shape=[t_shape(ATTN_HEADS, HEAD_DIM),
                   jax.ShapeDtypeStruct((b, ATTN_KV_HEADS, s, KEY_W), BF16),
                   t_shape(ATTN_KV_HEADS, VT_ROWS)],
        compiler_params=_params("parallel", "parallel"),
        name="qkprep",
    )(qkv, cos_t, sin_t, gain, bd)


def _attn_kernel(bounded_ref, qt_ref, k_ref, vt_ref, o_ref):
    tq = qt_ref.shape[-1]
    n_chunks = k_ref.shape[2] // ATTN_TK
    keys = lambda c: slice(c * ATTN_TK, (c + 1) * ATTN_TK)
    pad_rows = KEY_W - HEAD_DIM
    lanes = ATTN_STREAM_LANES

    def finish(acc, head, lo):
        out = (acc[:HEAD_DIM] / acc[HEAD_DIM:HEAD_DIM + 1]).T
        o_ref[0, lo:lo + lanes, head * HEAD_DIM:(head + 1) * HEAD_DIM] = out.astype(o_ref.dtype)

    def scores(c, q_ext):
        return jnp.dot(k_ref[0, 0, keys(c), :], q_ext, preferred_element_type=F32)

    def weighted_values(c, p_t):
        return jnp.dot(vt_ref[0, 0, :, keys(c)], p_t, preferred_element_type=F32)

    def fixed_shift_stream(head, lo):
        q_t = qt_ref[0, head, :, lo:lo + lanes]
        s0 = scores(0, jnp.concatenate([q_t, jnp.zeros((pad_rows, lanes), BF16)], axis=0))
        shift = jnp.max(s0, axis=0, keepdims=True).astype(BF16).astype(F32)
        yield
        first_row = lax.broadcasted_iota(jnp.int32, (pad_rows, lanes), 0) == 0
        q_ext = jnp.concatenate([q_t, jnp.where(first_row, -shift, 0.0).astype(BF16)], axis=0)
        nxt = scores(1, q_ext)
        acc = weighted_values(0, jnp.exp2(s0 - shift).astype(BF16))
        yield
        for c in range(1, n_chunks):
            s_t = nxt
            if c + 1 < n_chunks:
                nxt = scores(c + 1, q_ext)
            p_t = jnp.exp2(s_t).astype(BF16)
            yield
            acc = acc + weighted_values(c, p_t)
            yield
        finish(acc, head, lo)

    def running_max_stream(head, lo):
        q_ext = jnp.concatenate([qt_ref[0, head, :, lo:lo + lanes], jnp.zeros((pad_rows, lanes), BF16)], axis=0)
        m = jnp.full((1, lanes), -jnp.inf, F32)
        acc = jnp.zeros((VT_ROWS, lanes), F32)
        nxt = scores(0, q_ext)
        yield
        for c in range(n_chunks):
            s_t = nxt
            if c + 1 < n_chunks:
                nxt = scores(c + 1, q_ext)
            m_new = jnp.maximum(m, jnp.max(s_t, axis=0, keepdims=True))
            yield
            p_t = jnp.exp2((s_t - m_new).astype(BF16))
            yield
            acc = jnp.exp2(m - m_new) * acc + weighted_values(c, p_t)
            m = m_new
            yield
        finish(acc, head, lo)

    def run(stream):
        _interleave(*[stream(j, lo) for j in range(ATTN_GROUP) for lo in range(0, tq, lanes)])

    pl.when(bounded_ref[0] == 1)(functools.partial(run, fixed_shift_stream))
    pl.when(bounded_ref[0] == 0)(functools.partial(run, running_max_stream))


def _attention(bounded, q_t, k, v_t):
    b, _, _, s = q_t.shape
    gw = ATTN_GROUP * HEAD_DIM
    return pl.pallas_call(
        _attn_kernel,
        grid=(b, ATTN_KV_HEADS, s // ATTN_TQ),
        in_specs=[pl.BlockSpec(memory_space=pltpu.SMEM),
                  pl.BlockSpec((1, ATTN_GROUP, HEAD_DIM, ATTN_TQ), lambda bi, h, i: (bi, h, 0, i)),
                  pl.BlockSpec((1, 1, s, KEY_W), lambda bi, h, i: (bi, h, 0, 0)),
                  pl.BlockSpec((1, 1, VT_ROWS, s), lambda bi, h, i: (bi, h, 0, 0))],
        out_specs=pl.BlockSpec((1, ATTN_TQ, gw), lambda bi, h, i: (bi, i, h)),
        out_shape=jax.ShapeDtypeStruct((b, s, ATTN_W), BF16),
        compiler_params=_params("parallel", "parallel", "arbitrary"),
        name="attention",
    )(bounded, q_t, k, v_t)


def _chunk_order(d):
    n_chunks = REC_BLOCK // CHUNK
    return range(n_chunks) if d == 0 else range(n_chunks - 1, -1, -1)


def _glagate_kernel(a_ref, w2hi_ref, w2lo_ref, gb_ref, tbd_ref, cf_ref, cb_ref):
    a = a_ref[0]
    for d, out_ref in enumerate((cf_ref, cb_ref)):
        z = _dot_split(a, w2hi_ref[d], w2lo_ref[d]) + gb_ref[d]
        g = _log_sigmoid(z) * (1.0 / GLA_TAU)
        for sb in range(a.shape[0] // REC_BLOCK):
            rows = slice(sb * REC_BLOCK, (sb + 1) * REC_BLOCK)
            out_ref[0, rows, :] = _cumsum_cols(tbd_ref[d], g[rows])


def _glagate(gate_proj, w2_hi, w2_lo, gb, tbd):
    b, s, _ = gate_proj.shape
    const = lambda shape: pl.BlockSpec(shape, lambda bi, i: (0,) * len(shape))
    out_spec = pl.BlockSpec((1, ROW_TILE, GLA_W), lambda bi, i: (bi, i, 0))
    out_shape = jax.ShapeDtypeStruct((b, s, GLA_W), F32)
    return pl.pallas_call(
        _glagate_kernel,
        grid=(b, s // ROW_TILE),
        in_specs=[pl.BlockSpec((1, ROW_TILE, GATE_W), lambda bi, i: (bi, i, 0)),
                  const(w2_hi.shape), const(w2_lo.shape), const(gb.shape), const(tbd.shape)],
        out_specs=[out_spec, out_spec],
        out_shape=[out_shape, out_shape],
        compiler_params=_params("parallel", "parallel"),
        name="glagate",
    )(gate_proj, w2_hi, w2_lo, gb, tbd)


def _gla_direction(q_ref, k_ref, v_ref, cum_ref, o_ref, st_ref, d, n):
    causal_t, _, head_lane, block_diag = _head_layout_masks(d)
    last = CHUNK - 1 if d == 0 else 0
    st = st_ref[n, d]
    for i in _chunk_order(d):
        rows = slice(i * CHUNK, (i + 1) * CHUNK)
        cum = cum_ref[n, rows, :]
        total = cum[last:last + 1]
        ref = cum[CHUNK // 2:CHUNK // 2 + 1]
        qt = q_ref[n, rows, :] * Q_SCALE * jnp.exp(cum - ref)
        kt = k_ref[n, rows, :] * jnp.exp(ref - cum)
        qe = qt * jnp.exp(ref)
        kd = kt * jnp.exp(total - ref)
        v = v_ref[n, rows, :]
        q4 = jnp.concatenate([jnp.where(head_lane[h], qt, 0.0) for h in range(GLA_HEADS)], axis=0)
        att_t = jnp.where(causal_t, _dot_dims(kt, q4, NT_DIMS), 0.0)
        yield
        full = _dot_dims(att_t, v, TN_DIMS)
        out = _dot_dims(qe, st, NT_DIMS)
        yield
        for h in range(GLA_HEADS):
            out = out + jnp.where(head_lane[h], full[h * CHUNK:(h + 1) * CHUNK], 0.0)
        o_ref[n, rows, :] = out
        st = st * jnp.exp(total) + jnp.where(block_diag, _dot_dims(v, kd, TN_DIMS), 0.0)
        yield
    st_ref[n, d] = st


def _interleave(*stages):
    stages = list(stages)
    while stages:
        for g in list(stages):
            if next(g, StopIteration) is StopIteration:
                stages.remove(g)


def _gla_kernel(qf_ref, kf_ref, vf_ref, cf_ref, qb_ref, kb_ref, vb_ref, cb_ref, of_ref, ob_ref, st_ref):
    @pl.when(pl.program_id(1) == 0)
    def _():
        st_ref[...] = jnp.zeros(st_ref.shape, F32)

    stages = []
    for n in range(qf_ref.shape[0]):
        stages.append(_gla_direction(qf_ref, kf_ref, vf_ref, cf_ref, of_ref, st_ref, 0, n))
        stages.append(_gla_direction(qb_ref, kb_ref, vb_ref, cb_ref, ob_ref, st_ref, 1, n))
    _interleave(*stages)


def _gla(gla_proj, cum_f, cum_b):
    b, s, _ = gla_proj.shape
    nb = s // REC_BLOCK
    fwd = lambda col: pl.BlockSpec((REC_BATCH, REC_BLOCK, GLA_W), lambda bi, j: (bi, j, col))
    bwd = lambda col: pl.BlockSpec((REC_BATCH, REC_BLOCK, GLA_W), lambda bi, j: (bi, nb - 1 - j, col))
    out_shape = jax.ShapeDtypeStruct((b, s, GLA_W), F32)
    return pl.pallas_call(
        _gla_kernel,
        grid=(b // REC_BATCH, nb),
        in_specs=[fwd(0), fwd(1), fwd(2), fwd(0), bwd(0), bwd(1), bwd(2), bwd(0)],
        out_specs=[fwd(0), bwd(0)],
        out_shape=[out_shape, out_shape],
        scratch_shapes=[pltpu.VMEM((REC_BATCH, 2, GLA_W, GLA_W), F32)],
        compiler_params=_params("parallel", "arbitrary"),
        name="gla",
    )(gla_proj, gla_proj, gla_proj, cum_f, gla_proj, gla_proj, gla_proj, cum_b)


def _head_layout_masks(d):
    sub = lax.broadcasted_iota(jnp.int32, (CHUNK, MLSTM_W), 0)
    lane = lax.broadcasted_iota(jnp.int32, (CHUNK, MLSTM_W), 1)
    t_lane = lane & (HEAD_DIM - 1)
    causal_t = (sub <= t_lane) if d == 0 else (sub >= t_lane)
    eye_t = sub == t_lane
    head_lane = [(lane >> 6) == h for h in range(MLSTM_HEADS)]
    brow = lax.broadcasted_iota(jnp.int32, (MLSTM_W, MLSTM_W), 0) >> 6
    bcol = lax.broadcasted_iota(jnp.int32, (MLSTM_W, MLSTM_W), 1) >> 6
    return causal_t, eye_t, head_lane, brow == bcol


def _expand3(x, e):
    return sum(jnp.dot(p, e, preferred_element_type=F32) for p in _split3(x))


def _mlstm_direction(q_ref, k_ref, v_ref, gates, frow, h_ref, tbd, tbd_t, e_r, e_f, e_h,
                     ct_ref, n_ref, m_ref, d, n):
    causal_t, eye_t, head_lane, block_diag = _head_layout_masks(d)
    last = CHUNK - 1 if d == 0 else 0
    n_chunks = REC_BLOCK // CHUNK
    lane = lax.broadcasted_iota(jnp.int32, gates.shape, 1)
    i_lo = 2 * GLA_RANK + d * MLSTM_HEADS
    is_i_lane = (lane >= i_lo) & (lane < i_lo + MLSTM_HEADS)
    fcum = _cumsum_cols(tbd, _log_sigmoid(gates))
    r_exp = _expand3(jnp.where(is_i_lane, gates, fcum), e_r)
    f_last = jnp.concatenate([fcum[c * CHUNK + last:c * CHUNK + last + 1] for c in range(n_chunks)]
                             + [jnp.zeros((SUBLANE - n_chunks, fcum.shape[1]), F32)], axis=0)
    f_tot_rows = _expand3(f_last, e_f)
    fc_rows = _cumsum_rows(_log_sigmoid(frow), tbd_t)

    ct, n_row, m_row = ct_ref[n, d], n_ref[n, d], m_ref[n, d]
    for c in _chunk_order(d):
        rows = slice(c * CHUNK, (c + 1) * CHUNK)
        q, ks, v = q_ref[n, rows, :], k_ref[n, rows, :] * Q_SCALE, v_ref[n, rows, :]
        q4 = jnp.concatenate([jnp.where(head_lane[h], q, 0.0) for h in range(MLSTM_HEADS)], axis=0)
        kn = jnp.concatenate([ks, jnp.broadcast_to(n_row, (2 * SUBLANE, MLSTM_W))], axis=0)
        z = _dot_dims(kn, q4, NT_DIMS)
        s_t, qn = z[:CHUNK], z[CHUNK:CHUNK + 1]
        yield
        rc = r_exp[rows]
        rmax_t = jnp.max(jnp.where(causal_t, rc, -jnp.inf), axis=0, keepdims=True)
        rmax = jnp.max(rc, axis=0, keepdims=True)
        p_t = s_t * jnp.exp(jnp.where(causal_t, rc - rmax_t, -jnp.inf))
        row_sum = jnp.sum(p_t, axis=0, keepdims=True)
        kw = ks * jnp.exp(rc - rmax)
        fc_row = fc_rows[c:c + 1]
        inter = fc_row + m_row
        intra = fc_row + rmax_t
        m_t = jnp.maximum(inter, intra)
        w_inter, w_intra = jnp.exp(inter - m_t), jnp.exp(intra - m_t)
        den = w_inter * qn + w_intra * row_sum
        inv = 1.0 / jnp.maximum(jnp.abs(den), jnp.exp(-m_t))
        full = _dot_dims(p_t * (w_intra * inv), v, TN_DIMS)
        yield
        out = _expand3(jnp.where(eye_t, w_inter * inv, 0.0), e_h) * _dot(q, ct)
        for h in range(MLSTM_HEADS):
            out = out + jnp.where(head_lane[h], full[h * CHUNK:(h + 1) * CHUNK], 0.0)
        h_ref[n, rows, :] = out
        m_max = jnp.maximum(m_row, rmax)
        keep, add = jnp.exp(m_row - m_max), jnp.exp(rmax - m_max)
        ct = keep * ct + add * jnp.where(block_diag, _dot_dims(kw, v, TN_DIMS), 0.0)
        n_row = keep * n_row + add * jnp.sum(kw, axis=0, keepdims=True)
        m_row = f_tot_rows[c:c + 1] + m_max
        yield
    ct_ref[n, d], n_ref[n, d], m_ref[n, d] = ct, n_row, m_row


def _mlstm_kernel(qf_ref, kf_ref, vf_ref, gcf_ref, grf_ref, qb_ref, kb_ref, vb_ref, gcb_ref, grb_ref,
                  bcol_ref, brow_ref, tbd_ref, tbdt_ref, er_ref, ef_ref, eh_ref,
                  hf_ref, hb_ref, ct_ref, n_ref, m_ref):
    @pl.when(pl.program_id(1) == 0)
    def _():
        ct_ref[...] = jnp.zeros(ct_ref.shape, F32)
        n_ref[...] = jnp.zeros(n_ref.shape, F32)
        m_ref[...] = jnp.zeros(m_ref.shape, F32)

    stages = []
    for n in range(qf_ref.shape[0]):
        stages.append(_mlstm_direction(
            qf_ref, kf_ref, vf_ref, gcf_ref[n] + bcol_ref[...], grf_ref[n, 0, 0] + brow_ref[0],
            hf_ref, tbd_ref[0], tbdt_ref[0], er_ref[0], ef_ref[0], eh_ref[...], ct_ref, n_ref, m_ref, 0, n))
        stages.append(_mlstm_direction(
            qb_ref, kb_ref, vb_ref, gcb_ref[n] + bcol_ref[...], grb_ref[n, 0, 0] + brow_ref[1],
            hb_ref, tbd_ref[1], tbdt_ref[1], er_ref[1], ef_ref[1], eh_ref[...], ct_ref, n_ref, m_ref, 1, n))
    _interleave(*stages)


def _mlstm(mls_proj, gate_proj, f_rows, bcol, brow, tbd, tbd_t, e_r, e_f, e_h):
    b, s, _ = mls_proj.shape
    nb = s // REC_BLOCK
    fwd = lambda col: pl.BlockSpec((REC_BATCH, REC_BLOCK, MLSTM_W), lambda bi, j: (bi, j, col))
    bwd = lambda col: pl.BlockSpec((REC_BATCH, REC_BLOCK, MLSTM_W), lambda bi, j: (bi, nb - 1 - j, col))
    const = lambda shape: pl.BlockSpec(shape, lambda bi, j: (0,) * len(shape))
    gcol_f = pl.BlockSpec((REC_BATCH, REC_BLOCK, GATE_W), lambda bi, j: (bi, j, 0))
    gcol_b = pl.BlockSpec((REC_BATCH, REC_BLOCK, GATE_W), lambda bi, j: (bi, nb - 1 - j, 0))
    frow_shape = (REC_BATCH, 1, 1) + f_rows.shape[3:]
    frow_f = pl.BlockSpec(frow_shape, lambda bi, j: (bi, 0, j, 0, 0))
    frow_b = pl.BlockSpec(frow_shape, lambda bi, j: (bi, 1, nb - 1 - j, 0, 0))
    out_shape = jax.ShapeDtypeStruct((b, s, MLSTM_W), F32)
    return pl.pallas_call(
        _mlstm_kernel,
        grid=(b // REC_BATCH, nb),
        in_specs=[fwd(0), fwd(1), fwd(2), gcol_f, frow_f, bwd(0), bwd(1), bwd(2), gcol_b, frow_b,
                  const(bcol.shape), const(brow.shape), const(tbd.shape), const(tbd_t.shape),
                  const(e_r.shape), const(e_f.shape), const(e_h.shape)],
        out_specs=[fwd(0), bwd(0)],
        out_shape=[out_shape, out_shape],
        scratch_shapes=[pltpu.VMEM((REC_BATCH, 2, MLSTM_W, MLSTM_W), F32),
                        pltpu.VMEM((REC_BATCH, 2, 1, MLSTM_W), F32),
                        pltpu.VMEM((REC_BATCH, 2, 1, MLSTM_W), F32)],
        compiler_params=_params("parallel", "arbitrary"),
        name="mlstm",
    )(mls_proj, mls_proj, mls_proj, gate_proj, f_rows,
      mls_proj, mls_proj, mls_proj, gate_proj, f_rows, bcol, brow, tbd, tbd_t, e_r, e_f, e_h)


def _outproj_kernel(x_ref, attn_ref, gof_ref, gob_ref, gg_ref, mhf_ref, mhb_ref, mo_ref,
                    gng_ref, mng_ref, bd_ref, wo_ref, out_ref):
    bd = bd_ref[...]

    def head_norm(t, gain):
        return t * lax.rsqrt(_group_mean(t * t, bd) + EPS) * gain

    gg = gg_ref[0]
    gla = head_norm(gof_ref[0] + gob_ref[0], gng_ref[...]) * (gg * _sigmoid(gg))
    mls = head_norm(_sigmoid(mo_ref[0]) * (mhf_ref[0] + mhb_ref[0]), mng_ref[...])
    acc = jnp.dot(attn_ref[0], wo_ref[:ATTN_W, :], preferred_element_type=F32)
    acc += jnp.dot(gla.astype(BF16), wo_ref[ATTN_W:ATTN_W + GLA_W, :], preferred_element_type=F32)
    acc += jnp.dot(mls.astype(BF16), wo_ref[ATTN_W + GLA_W:, :], preferred_element_type=F32)
    out_ref[0] = x_ref[0] + acc


def _outproj(x, attn, gof, gob, gla_proj, mhf, mhb, mls_proj, gng, mng, bd, wo):
    b, s, d = x.shape
    row = lambda n, col=0: pl.BlockSpec((1, ROW_TILE, n), lambda bi, i: (bi, i, col))
    const = lambda shape: pl.BlockSpec(shape, lambda bi, i: (0,) * len(shape))
    return pl.pallas_call(
        _outproj_kernel,
        grid=(b, s // ROW_TILE),
        in_specs=[row(d), row(ATTN_W), row(GLA_W), row(GLA_W), row(GLA_W, 3),
                  row(MLSTM_W), row(MLSTM_W), row(MLSTM_W, 3),
                  const((1, GLA_W)), const((1, MLSTM_W)), const(bd.shape), const(wo.shape)],
        out_specs=row(d),
        out_shape=jax.ShapeDtypeStruct((b, s, d), F32),
        compiler_params=_params("parallel", "parallel"),
        name="outproj",
    )(x, attn, gof, gob, gla_proj, mhf, mhb, mls_proj, gng, mng, bd, wo)


def _ffn_kernel(x_ref, prev_ref, next_ref, g_ref, wg_ref, wv_ref, cwg_ref, cwv_ref, cbg_ref, cbv_ref,
                wd_ref, out_ref, h_ref, ug_ref, uv_ref, acc_ref):
    i, j = pl.program_id(1), pl.program_id(2)
    tm = x_ref.shape[1]

    def normed(t):
        ms = jnp.mean(t * t, axis=-1, keepdims=True)
        return t * lax.rsqrt(ms + EPS) * g_ref[...]

    @pl.when(j == 0)
    def _():
        keep_prev = (i > 0).astype(F32)
        keep_next = (i < pl.num_programs(1) - 1).astype(F32)
        h_ref[0:HALO, :] = (normed(prev_ref[0]) * keep_prev).astype(BF16)
        h_ref[HALO:HALO + tm, :] = normed(x_ref[0]).astype(BF16)
        h_ref[HALO + tm:, :] = (normed(next_ref[0]) * keep_next).astype(BF16)
        acc_ref[...] = jnp.zeros(acc_ref.shape, F32)

    h = h_ref[...]
    ug_ref[...] = jnp.dot(h, wg_ref[...], preferred_element_type=F32)
    uv_ref[...] = jnp.dot(h, wv_ref[...], preferred_element_type=F32)

    def conv(u_ref, cw_ref, cb_ref):
        out = cb_ref[...] + u_ref[HALO - 1:HALO - 1 + tm, :] * cw_ref[0:1, :]
        for t in range(1, CONV_WIDTH):
            out = out + u_ref[HALO - 1 + t:HALO - 1 + t + tm, :] * cw_ref[t:t + 1, :]
        return out

    gate = conv(ug_ref, cwg_ref, cbg_ref)
    val = conv(uv_ref, cwv_ref, cbv_ref)
    act = (gate * _sigmoid(gate) * val).astype(BF16)
    acc_ref[...] += jnp.dot(act, wd_ref[...], preferred_element_type=F32)

    @pl.when(j == pl.num_programs(2) - 1)
    def _():
        out_ref[0] = x_ref[0] + acc_ref[...]


def _ffn(x, g, w_up, conv_w, conv_b, w_down):
    b, s, d = x.shape
    nj = D_FF // FF_BLOCK
    tiles_per_halo = ROW_TILE // HALO
    last_halo = s // HALO - 1
    row = pl.BlockSpec((1, ROW_TILE, d), lambda bi, i, j: (bi, i, 0))
    prev = pl.BlockSpec((1, HALO, d), lambda bi, i, j: (bi, jnp.maximum(i * tiles_per_halo - 1, 0), 0))
    nxt = pl.BlockSpec((1, HALO, d),
                       lambda bi, i, j: (bi, jnp.minimum((i + 1) * tiles_per_halo, last_halo), 0))
    gate_cols = lambda rows: pl.BlockSpec((rows, FF_BLOCK), lambda bi, i, j: (0, j))
    val_cols = lambda rows: pl.BlockSpec((rows, FF_BLOCK), lambda bi, i, j: (0, nj + j))
    return pl.pallas_call(
        _ffn_kernel,
        grid=(b, s // ROW_TILE, nj),
        in_specs=[row, prev, nxt, pl.BlockSpec((1, d), lambda bi, i, j: (0, 0)),
                  gate_cols(d), val_cols(d), gate_cols(CONV_WIDTH), val_cols(CONV_WIDTH),
                  gate_cols(1), val_cols(1),
                  pl.BlockSpec((FF_BLOCK, d), lambda bi, i, j: (j, 0))],
        out_specs=row,
        out_shape=jax.ShapeDtypeStruct((b, s, d), F32),
        scratch_shapes=[pltpu.VMEM((ROW_TILE + 2 * HALO, d), BF16),
                        pltpu.VMEM((ROW_TILE + 2 * HALO, FF_BLOCK), F32),
                        pltpu.VMEM((ROW_TILE + 2 * HALO, FF_BLOCK), F32),
                        pltpu.VMEM((ROW_TILE, d), F32)],
        compiler_params=_params("parallel", "parallel", "arbitrary"),
        name="ffn",
    )(x, x, x, g, w_up, w_up, conv_w, conv_w, conv_b, conv_b, w_down)


def _rope_tables(seq):
    n_rows = seq // GRID_W
    row = jnp.repeat(jnp.arange(n_rows, dtype=F32), GRID_W)
    col = jnp.tile(jnp.arange(GRID_W, dtype=F32), n_rows)
    n_freq = HEAD_DIM // 4
    inv_freq = jnp.power(ROPE_THETA, -jnp.arange(n_freq, dtype=F32) / n_freq)
    ang_r, ang_c = row[:, None] * inv_freq, col[:, None] * inv_freq
    cos_h = jnp.concatenate([jnp.cos(ang_r), jnp.cos(ang_r), jnp.cos(ang_c), jnp.cos(ang_c)], axis=1)
    sin_h = jnp.concatenate([-jnp.sin(ang_r), jnp.sin(ang_r), -jnp.sin(ang_c), jnp.sin(ang_c)], axis=1)
    reps = QK_W // HEAD_DIM
    return jnp.tile(cos_h, (1, reps)), jnp.tile(sin_h, (1, reps))


def _block_diag_mean(width):
    group = np.arange(width) // HEAD_DIM
    return jnp.asarray((group[:, None] == group[None, :]).astype(np.float32) / HEAD_DIM, dtype=BF16)


def _tri_constants():
    t = np.arange(REC_BLOCK)
    same_chunk = (t[:, None] // CHUNK) == (t[None, :] // CHUNK)
    prefix = (same_chunk & (t[None, :] <= t[:, None])).astype(np.float32)
    suffix = (same_chunk & (t[None, :] >= t[:, None])).astype(np.float32)
    tbd = jnp.asarray(np.stack([prefix, suffix]), dtype=BF16)
    tbd_t = jnp.asarray(np.stack([prefix.T, suffix.T]), dtype=BF16)
    return tbd, tbd_t


def _mlstm_expanders():
    e_r = np.zeros((2, GATE_W, MLSTM_W), np.float32)
    e_f = np.zeros((2, GATE_W, MLSTM_W), np.float32)
    for d in range(2):
        for h in range(MLSTM_HEADS):
            lanes = slice(h * HEAD_DIM, (h + 1) * HEAD_DIM)
            i_lane = 2 * GLA_RANK + d * MLSTM_HEADS + h
            f_lane = i_lane + 2 * MLSTM_HEADS
            e_r[d, i_lane, lanes] = 1.0
            e_r[d, f_lane, lanes] = -1.0
            e_f[d, f_lane, lanes] = 1.0
    group = np.arange(MLSTM_W) // HEAD_DIM
    e_h = (group[:, None] == group[None, :]).astype(np.float32)
    return jnp.asarray(e_r, dtype=BF16), jnp.asarray(e_f, dtype=BF16), jnp.asarray(e_h, dtype=BF16)


def _layer(x, consts, norm_mix_g, w_in, attn_qn_g, attn_kn_g, gla_gate_w2, gla_gate_b, gla_norm_g,
           mlstm_gate_b, mlstm_norm_g, w_out, norm_ffn_g, w_up, conv_w, conv_b, w_down):
    cos_t, sin_t, bd_qk, bd_head, tbd, tbd_t, e_r, e_f, e_h = consts
    b, s, _ = x.shape
    n_gates = 4 * MLSTM_HEADS
    main_w = QKV_W + 4 * GLA_W
    gla_a = w_in[:, main_w:main_w + 2 * GLA_RANK]
    mls_main = w_in[:, main_w + 2 * GLA_RANK:main_w + 2 * GLA_RANK + 4 * MLSTM_W]
    mls_g = w_in[:, main_w + 2 * GLA_RANK + 4 * MLSTM_W:]
    pad = jnp.zeros((D_MODEL, GATE_W - 2 * GLA_RANK - n_gates), F32)
    w_in_r = jnp.concatenate([w_in[:, :main_w], mls_main, gla_a, mls_g, pad], axis=1).astype(BF16)

    qkv, gla_proj, mls_proj, gate_proj = _inproj(x, norm_mix_g[None, :], w_in_r)

    qk_gain = jnp.concatenate([jnp.tile(attn_qn_g, ATTN_HEADS), jnp.tile(attn_kn_g, ATTN_KV_HEADS)])[None, :]
    q_t, k, v_t = _qkprep(qkv, cos_t, sin_t, qk_gain, bd_qk)
    score_bound = (HEAD_DIM * Q_SCALE * LOG2_E) * jnp.max(jnp.abs(attn_qn_g)) * jnp.max(jnp.abs(attn_kn_g))
    bounded = (score_bound < ATTN_MAX_SCORE_BOUND).astype(jnp.int32).reshape(1)
    attn = _attention(bounded, q_t, k, v_t)

    w2ext = jnp.zeros((2, GATE_W, GLA_W), F32)
    w2ext = w2ext.at[0, :GLA_RANK].set(gla_gate_w2[0]).at[1, GLA_RANK:2 * GLA_RANK].set(gla_gate_w2[1])
    w2_hi = w2ext.astype(BF16)
    w2_lo = (w2ext - w2_hi.astype(F32)).astype(BF16)
    cum_f, cum_b = _glagate(gate_proj, w2_hi, w2_lo, gla_gate_b[:, None, :], tbd)
    gof, gob = _gla(gla_proj, cum_f, cum_b)

    nb, cpb = s // REC_BLOCK, REC_BLOCK // CHUNK
    f_lo = 2 * GLA_RANK + 2 * MLSTM_HEADS
    f_rows = gate_proj[:, :, f_lo:f_lo + 2 * MLSTM_HEADS].reshape(b, nb, cpb, CHUNK, 2, MLSTM_HEADS)
    f_rows = f_rows.transpose(0, 4, 1, 2, 5, 3).reshape(b, 2, nb, cpb, MLSTM_W)
    f_rows = jnp.pad(f_rows, ((0, 0), (0, 0), (0, 0), (0, SUBLANE - cpb), (0, 0)))
    bcol = jnp.zeros((1, GATE_W), F32).at[0, 2 * GLA_RANK:2 * GLA_RANK + n_gates].set(
        mlstm_gate_b.reshape(n_gates))
    brow = jnp.repeat(mlstm_gate_b[2:], HEAD_DIM, axis=1)[:, None, :]
    mhf, mhb = _mlstm(mls_proj, gate_proj, f_rows, bcol, brow, tbd, tbd_t, e_r, e_f, e_h)

    x = _outproj(x, attn, gof, gob, gla_proj, mhf, mhb, mls_proj,
                 jnp.tile(gla_norm_g, GLA_HEADS)[None, :], jnp.tile(mlstm_norm_g, MLSTM_HEADS)[None, :],
                 bd_head, w_out.astype(BF16))
    return _ffn(x, norm_ffn_g[None, :], w_up.astype(BF16), conv_w, conv_b[None, :], w_down.astype(BF16))


def _trunk(x, weights):
    seq = x.shape[1]
    cos_t, sin_t = _rope_tables(seq)
    consts = ((cos_t, sin_t, _block_diag_mean(QK_W), _block_diag_mean(GLA_W))
              + _tri_constants() + _mlstm_expanders())
    depth = weights[0].shape[0]
    for l in range(depth):
        x = _layer(x, consts, *[w[l] for w in weights])
    return x


def kernel(x_prompt, x_sample, norm_mix_g, w_in, attn_qn_g, attn_kn_g, gla_gate_w2, gla_gate_b,
           gla_norm_g, mlstm_gate_b, mlstm_norm_g, w_out, norm_ffn_g, w_up, conv_w, conv_b, w_down):
    weights = (norm_mix_g, w_in, attn_qn_g, attn_kn_g, gla_gate_w2, gla_gate_b, gla_norm_g,
               mlstm_gate_b, mlstm_norm_g, w_out, norm_ffn_g, w_up, conv_w, conv_b, w_down)
    return _trunk(x_prompt, weights), _trunk(x_sample, weights)
```

```python
import functools

import jax
import jax.numpy as jnp
import numpy as np
from jax import lax
from jax.experimental import pallas as pl
from jax.experimental.pallas import tpu as pltpu

F32 = jnp.float32
BF16 = jnp.bfloat16

D_MODEL = 1024
GRID_W = 64
HEAD_DIM = 64
EPS = 1e-6
ATTN_HEADS = 8
ATTN_KV_HEADS = 2
ATTN_GROUP = ATTN_HEADS // ATTN_KV_HEADS
ROPE_THETA = 10000.0
GLA_HEADS = 4
GLA_RANK = 16
GLA_TAU = 16.0
MLSTM_HEADS = 4
CHUNK = 64
ATTN_W = ATTN_HEADS * HEAD_DIM
KV_W = ATTN_KV_HEADS * HEAD_DIM
GLA_W = GLA_HEADS * HEAD_DIM
MLSTM_W = MLSTM_HEADS * HEAD_DIM
QK_W = ATTN_W + KV_W
QKV_W = ATTN_W + 2 * KV_W
D_FF = 2816
CONV_WIDTH = 3
GATE_W = 128
Q_SCALE = HEAD_DIM ** -0.5
LOG2_E = 1.4426950408889634

LANE = 128
SUBLANE = 8
VMEM_LIMIT = 56 * 1024 * 1024

ROW_TILE = 1024
FFN_ROW_TILE = 512
REC_BLOCK = 256
REC_BATCH = 4
ATTN_TQ = 512
ATTN_STREAM_LANES = 256
ATTN_TK = 256
VT_ROWS = HEAD_DIM + 16
KEY_W = 128
ATTN_MAX_SCORE_BOUND = 32.0
FF_BLOCK = 1408
HALO = SUBLANE

NT_DIMS = (((1,), (1,)), ((), ()))
TN_DIMS = (((0,), (0,)), ((), ()))


def _params(*sem):
    return pltpu.CompilerParams(dimension_semantics=sem, vmem_limit_bytes=VMEM_LIMIT)


def _log_sigmoid(z):
    return jnp.minimum(z, 0.0) - jnp.log(1.0 + jnp.exp(-jnp.abs(z)))


def _sigmoid(z):
    return 1.0 / (1.0 + jnp.exp(-z))


def _dot(a, b):
    return jnp.dot(a.astype(BF16), b.astype(BF16), preferred_element_type=F32)


def _dot_dims(a, b, dims):
    return lax.dot_general(a.astype(BF16), b.astype(BF16), dims, preferred_element_type=F32)


def _split3(x):
    p1 = x.astype(BF16)
    rest = x - p1.astype(F32)
    p2 = rest.astype(BF16)
    return p1, p2, (rest - p2.astype(F32)).astype(BF16)


def _cumsum_cols(tbd, x):
    return sum(jnp.dot(tbd, p, preferred_element_type=F32) for p in _split3(x))


def _cumsum_rows(x, tbd_t):
    return sum(jnp.dot(p, tbd_t, preferred_element_type=F32) for p in _split3(x))


def _dot_split(a, b_hi, b_lo):
    a_hi = a.astype(BF16)
    a_lo = (a - a_hi.astype(F32)).astype(BF16)
    return (jnp.dot(a_hi, b_hi, preferred_element_type=F32)
            + jnp.dot(a_lo, b_hi, preferred_element_type=F32)
            + jnp.dot(a_hi, b_lo, preferred_element_type=F32))


def _group_mean(sq, bd):
    hi = sq.astype(BF16)
    lo = (sq - hi.astype(F32)).astype(BF16)
    return (jnp.dot(hi, bd, preferred_element_type=F32)
            + jnp.dot(lo, bd, preferred_element_type=F32))


def _inproj_kernel(x_ref, g_ref, w_ref, qkv_ref, gla_ref, mls_ref, gate_ref):
    x = x_ref[0]
    ms = jnp.mean(x * x, axis=-1, keepdims=True)
    h = (x * lax.rsqrt(ms + EPS) * g_ref[...]).astype(BF16)
    col = 0
    for ref in (qkv_ref, gla_ref, mls_ref, gate_ref):
        width = ref.shape[-1]
        ref[0] = jnp.dot(h, w_ref[:, col:col + width], preferred_element_type=F32)
        col += width


def _inproj(x, g, w):
    b, s, d = x.shape
    widths = (QKV_W, 4 * GLA_W, 4 * MLSTM_W, GATE_W)
    row = lambda n: pl.BlockSpec((1, ROW_TILE, n), lambda bi, i: (bi, i, 0))
    const = lambda shape: pl.BlockSpec(shape, lambda bi, i: (0,) * len(shape))
    return pl.pallas_call(
        _inproj_kernel,
        grid=(b, s // ROW_TILE),
        in_specs=[row(d), const((1, d)), const(w.shape)],
        out_specs=[row(n) for n in widths],
        out_shape=[jax.ShapeDtypeStruct((b, s, n), F32) for n in widths],
        compiler_params=_params("parallel", "parallel"),
        name="inproj",
    )(x, g, w)


def _qkprep_kernel(qkv_ref, cos_ref, sin_ref, gain_ref, bd_ref, qt_ref, k_ref, vt_ref):
    x = qkv_ref[0]
    qk = x[:, :QK_W]
    ms = _group_mean(qk * qk, bd_ref[...])
    y = qk * lax.rsqrt(ms + EPS) * gain_ref[...]
    lane = lax.broadcasted_iota(jnp.int32, y.shape, 1)
    first_half = (lane & 31) < 16
    quarter = HEAD_DIM // 4
    swapped = jnp.where(first_half, pltpu.roll(y, QK_W - quarter, 1), pltpu.roll(y, quarter, 1))
    r = y * cos_ref[...] + swapped * sin_ref[...]
    q_t = (r[:, :ATTN_W] * (Q_SCALE * LOG2_E)).T.astype(BF16)
    qt_ref[0] = q_t.reshape(ATTN_HEADS, HEAD_DIM, q_t.shape[-1])
    v_t = x[:, QK_W:].T.astype(BF16)
    v_t = v_t.reshape(ATTN_KV_HEADS, HEAD_DIM, v_t.shape[-1])
    row = lax.broadcasted_iota(jnp.int32, (ATTN_KV_HEADS, VT_ROWS - HEAD_DIM, v_t.shape[-1]), 1)
    vt_ref[0] = jnp.concatenate([v_t, jnp.where(row == 0, 1.0, 0.0).astype(BF16)], axis=1)
    ones_lane = lax.broadcasted_iota(jnp.int32, (x.shape[0], KEY_W - HEAD_DIM), 1) == 0
    k_pad = jnp.where(ones_lane, 1.0, 0.0).astype(BF16)
    for h in range(ATTN_KV_HEADS):
        k_h = r[:, ATTN_W + h * HEAD_DIM:ATTN_W + (h + 1) * HEAD_DIM].astype(BF16)
        k_ref[0, h] = jnp.concatenate([k_h, k_pad], axis=1)


def _qkprep(qkv, cos_t, sin_t, gain, bd):
    b, s, _ = qkv.shape
    const = lambda shape: pl.BlockSpec(shape, lambda bi, i: (0,) * len(shape))
    tab = pl.BlockSpec((ROW_TILE, QK_W), lambda bi, i: (i, 0))
    t_spec = lambda heads, rows: pl.BlockSpec((1, heads, rows, ROW_TILE), lambda bi, i: (bi, 0, 0, i))
    t_shape = lambda heads, rows: jax.ShapeDtypeStruct((b, heads, rows, s), BF16)
    return pl.pallas_call(
        _qkprep_kernel,
        grid=(b, s // ROW_TILE),
        in_specs=[pl.BlockSpec((1, ROW_TILE, QKV_W), lambda bi, i: (bi, i, 0)),
                  tab, tab, const((1, QK_W)), const((QK_W, QK_W))],
        out_specs=[t_spec(ATTN_HEADS, HEAD_DIM),
                   pl.BlockSpec((1, ATTN_KV_HEADS, ROW_TILE, KEY_W), lambda bi, i: (bi, 0, i, 0)),
                   t_spec(ATTN_KV_HEADS, VT_ROWS)],
        out_shape=[t_shape(ATTN_HEADS, HEAD_DIM),
                   jax.ShapeDtypeStruct((b, ATTN_KV_HEADS, s, KEY_W), BF16),
                   t_shape(ATTN_KV_HEADS, VT_ROWS)],
        compiler_params=_params("parallel", "parallel"),
        name="qkprep",
    )(qkv, cos_t, sin_t, gain, bd)


def _attn_kernel(bounded_ref, qt_ref, k_ref, vt_ref, o_ref):
    tq = qt_ref.shape[-1]
    n_chunks = k_ref.shape[2] // ATTN_TK
    keys = lambda c: slice(c * ATTN_TK, (c + 1) * ATTN_TK)
    pad_rows = KEY_W - HEAD_DIM
    lanes = ATTN_STREAM_LANES

    def finish(acc, head, lo):
        out = (acc[:HEAD_DIM] / acc[HEAD_DIM:HEAD_DIM + 1]).T
        o_ref[0, lo:lo + lanes, head * HEAD_DIM:(head + 1) * HEAD_DIM] = out.astype(o_ref.dtype)

    def scores(c, q_ext):
        return jnp.dot(k_ref[0, 0, keys(c), :], q_ext, preferred_element_type=F32)

    def weighted_values(c, p_t):
        return jnp.dot(vt_ref[0, 0, :, keys(c)], p_t, preferred_element_type=F32)

    def fixed_shift_stream(head, lo):
        q_t = qt_ref[0, head, :, lo:lo + lanes]
        s0 = scores(0, jnp.concatenate([q_t, jnp.zeros((pad_rows, lanes), BF16)], axis=0))
        shift = jnp.max(s0, axis=0, keepdims=True).astype(BF16).astype(F32)
        yield
        first_row = lax.broadcasted_iota(jnp.int32, (pad_rows, lanes), 0) == 0
        q_ext = jnp.concatenate([q_t, jnp.where(first_row, -shift, 0.0).astype(BF16)], axis=0)
        nxt = scores(1, q_ext)
        acc = weighted_values(0, jnp.exp2(s0 - shift).astype(BF16))
        yield
        for c in range(1, n_chunks):
            s_t = nxt
            if c + 1 < n_chunks:
                nxt = scores(c + 1, q_ext)
            p_t = jnp.exp2(s_t).astype(BF16)
            yield
            acc = acc + weighted_values(c, p_t)
            yield
        finish(acc, head, lo)

    def running_max_stream(head, lo):
        q_ext = jnp.concatenate([qt_ref[0, head, :, lo:lo + lanes], jnp.zeros((pad_rows, lanes), BF16)], axis=0)
        m = jnp.full((1, lanes), -jnp.inf, F32)
        acc = jnp.zeros((VT_ROWS, lanes), F32)
        nxt = scores(0, q_ext)
        yield
        for c in range(n_chunks):
            s_t = nxt
            if c + 1 < n_chunks:
                nxt = scores(c + 1, q_ext)
            m_new = jnp.maximum(m, jnp.max(s_t, axis=0, keepdims=True))
            yield
            p_t = jnp.exp2((s_t - m_new).astype(BF16))
            yield
            acc = jnp.exp2(m - m_new) * acc + weighted_values(c, p_t)
            m = m_new
            yield
        finish(acc, head, lo)

    def run(stream):
        _interleave(*[stream(j, lo) for j in range(ATTN_GROUP) for lo in range(0, tq, lanes)])

    pl.when(bounded_ref[0] == 1)(functools.partial(run, fixed_shift_stream))
    pl.when(bounded_ref[0] == 0)(functools.partial(run, running_max_stream))


def _attention(bounded, q_t, k, v_t):
    b, _, _, s = q_t.shape
    gw = ATTN_GROUP * HEAD_DIM
    return pl.pallas_call(
        _attn_kernel,
        grid=(b, ATTN_KV_HEADS, s // ATTN_TQ),
        in_specs=[pl.BlockSpec(memory_space=pltpu.SMEM),
                  pl.BlockSpec((1, ATTN_GROUP, HEAD_DIM, ATTN_TQ), lambda bi, h, i: (bi, h, 0, i)),
                  pl.BlockSpec((1, 1, s, KEY_W), lambda bi, h, i: (bi, h, 0, 0)),
                  pl.BlockSpec((1, 1, VT_ROWS, s), lambda bi, h, i: (bi, h, 0, 0))],
        out_specs=pl.BlockSpec((1, ATTN_TQ, gw), lambda bi, h, i: (bi, i, h)),
        out_shape=jax.ShapeDtypeStruct((b, s, ATTN_W), BF16),
        compiler_params=_params("parallel", "parallel", "arbitrary"),
        name="attention",
    )(bounded, q_t, k, v_t)


def _chunk_order(d):
    n_chunks = REC_BLOCK // CHUNK
    return range(n_chunks) if d == 0 else range(n_chunks - 1, -1, -1)


def _glagate_kernel(a_ref, w2hi_ref, w2lo_ref, gb_ref, tbd_ref, cf_ref, cb_ref):
    a = a_ref[0]
    for d, out_ref in enumerate((cf_ref, cb_ref)):
        z = _dot_split(a, w2hi_ref[d], w2lo_ref[d]) + gb_ref[d]
        g = _log_sigmoid(z) * (1.0 / GLA_TAU)
        for sb in range(a.shape[0] // REC_BLOCK):
            rows = slice(sb * REC_BLOCK, (sb + 1) * REC_BLOCK)
            out_ref[0, rows, :] = _cumsum_cols(tbd_ref[d], g[rows])


def _glagate(gate_proj, w2_hi, w2_lo, gb, tbd):
    b, s, _ = gate_proj.shape
    const = lambda shape: pl.BlockSpec(shape, lambda bi, i: (0,) * len(shape))
    out_spec = pl.BlockSpec((1, ROW_TILE, GLA_W), lambda bi, i: (bi, i, 0))
    out_shape = jax.ShapeDtypeStruct((b, s, GLA_W), F32)
    return pl.pallas_call(
        _glagate_kernel,
        grid=(b, s // ROW_TILE),
        in_specs=[pl.BlockSpec((1, ROW_TILE, GATE_W), lambda bi, i: (bi, i, 0)),
                  const(w2_hi.shape), const(w2_lo.shape), const(gb.shape), const(tbd.shape)],
        out_specs=[out_spec, out_spec],
        out_shape=[out_shape, out_shape],
        compiler_params=_params("parallel", "parallel"),
        name="glagate",
    )(gate_proj, w2_hi, w2_lo, gb, tbd)


def _gla_direction(q_ref, k_ref, v_ref, cum_ref, o_ref, st_ref, d, n):
    causal_t, _, head_lane, block_diag = _head_layout_masks(d)
    last = CHUNK - 1 if d == 0 else 0
    st = st_ref[n, d]
    for i in _chunk_order(d):
        rows = slice(i * CHUNK, (i + 1) * CHUNK)
        cum = cum_ref[n, rows, :]
        total = cum[last:last + 1]
        ref = cum[CHUNK // 2:CHUNK // 2 + 1]
        qt = q_ref[n, rows, :] * Q_SCALE * jnp.exp(cum - ref)
        kt = k_ref[n, rows, :] * jnp.exp(ref - cum)
        qe = qt * jnp.exp(ref)
        kd = kt * jnp.exp(total - ref)
        v = v_ref[n, rows, :]
        q4 = jnp.concatenate([jnp.where(head_lane[h], qt, 0.0) for h in range(GLA_HEADS)], axis=0)
        att_t = jnp.where(causal_t, _dot_dims(kt, q4, NT_DIMS), 0.0)
        yield
        full = _dot_dims(att_t, v, TN_DIMS)
        out = _dot_dims(qe, st, NT_DIMS)
        yield
        for h in range(GLA_HEADS):
            out = out + jnp.where(head_lane[h], full[h * CHUNK:(h + 1) * CHUNK], 0.0)
        o_ref[n, rows, :] = out
        st = st * jnp.exp(total) + jnp.where(block_diag, _dot_dims(v, kd, TN_DIMS), 0.0)
        yield
    st_ref[n, d] = st


def _interleave(*stages):
    stages = list(stages)
    while stages:
        for g in list(stages):
            if next(g, StopIteration) is StopIteration:
                stages.remove(g)


def _gla_kernel(qf_ref, kf_ref, vf_ref, cf_ref, qb_ref, kb_ref, vb_ref, cb_ref, of_ref, ob_ref, st_ref):
    @pl.when(pl.program_id(1) == 0)
    def _():
        st_ref[...] = jnp.zeros(st_ref.shape, F32)

    stages = []
    for n in range(qf_ref.shape[0]):
        stages.append(_gla_direction(qf_ref, kf_ref, vf_ref, cf_ref, of_ref, st_ref, 0, n))
        stages.append(_gla_direction(qb_ref, kb_ref, vb_ref, cb_ref, ob_ref, st_ref, 1, n))
    _interleave(*stages)


def _gla(gla_proj, cum_f, cum_b):
    b, s, _ = gla_proj.shape
    nb = s // REC_BLOCK
    fwd = lambda col: pl.BlockSpec((REC_BATCH, REC_BLOCK, GLA_W), lambda bi, j: (bi, j, col))
    bwd = lambda col: pl.BlockSpec((REC_BATCH, REC_BLOCK, GLA_W), lambda bi, j: (bi, nb - 1 - j, col))
    out_shape = jax.ShapeDtypeStruct((b, s, GLA_W), F32)
    return pl.pallas_call(
        _gla_kernel,
        grid=(b // REC_BATCH, nb),
        in_specs=[fwd(0), fwd(1), fwd(2), fwd(0), bwd(0), bwd(1), bwd(2), bwd(0)],
        out_specs=[fwd(0), bwd(0)],
        out_shape=[out_shape, out_shape],
        scratch_shapes=[pltpu.VMEM((REC_BATCH, 2, GLA_W, GLA_W), F32)],
        compiler_params=_params("parallel", "arbitrary"),
        name="gla",
    )(gla_proj, gla_proj, gla_proj, cum_f, gla_proj, gla_proj, gla_proj, cum_b)


def _head_layout_masks(d):
    sub = lax.broadcasted_iota(jnp.int32, (CHUNK, MLSTM_W), 0)
    lane = lax.broadcasted_iota(jnp.int32, (CHUNK, MLSTM_W), 1)
    t_lane = lane & (HEAD_DIM - 1)
    causal_t = (sub <= t_lane) if d == 0 else (sub >= t_lane)
    eye_t = sub == t_lane
    head_lane = [(lane >> 6) == h for h in range(MLSTM_HEADS)]
    brow = lax.broadcasted_iota(jnp.int32, (MLSTM_W, MLSTM_W), 0) >> 6
    bcol = lax.broadcasted_iota(jnp.int32, (MLSTM_W, MLSTM_W), 1) >> 6
    return causal_t, eye_t, head_lane, brow == bcol


def _expand3(x, e):
    return sum(jnp.dot(p, e, preferred_element_type=F32) for p in _split3(x))


def _mlstm_direction(q_ref, k_ref, v_ref, gates, frow, h_ref, tbd, tbd_t, e_r, e_f, e_h,
                     ct_ref, n_ref, m_ref, d, n):
    causal_t, eye_t, head_lane, block_diag = _head_layout_masks(d)
    last = CHUNK - 1 if d == 0 else 0
    n_chunks = REC_BLOCK // CHUNK
    lane = lax.broadcasted_iota(jnp.int32, gates.shape, 1)
    i_lo = 2 * GLA_RANK + d * MLSTM_HEADS
    is_i_lane = (lane >= i_lo) & (lane < i_lo + MLSTM_HEADS)
    fcum = _cumsum_cols(tbd, _log_sigmoid(gates))
    r_exp = _expand3(jnp.where(is_i_lane, gates, fcum), e_r)
    f_last = jnp.concatenate([fcum[c * CHUNK + last:c * CHUNK + last + 1] for c in range(n_chunks)]
                             + [jnp.zeros((SUBLANE - n_chunks, fcum.shape[1]), F32)], axis=0)
    f_tot_rows = _expand3(f_last, e_f)
    fc_rows = _cumsum_rows(_log_sigmoid(frow), tbd_t)

    ct, n_row, m_row = ct_ref[n, d], n_ref[n, d], m_ref[n, d]
    for c in _chunk_order(d):
        rows = slice(c * CHUNK, (c + 1) * CHUNK)
        q, ks, v = q_ref[n, rows, :], k_ref[n, rows, :] * Q_SCALE, v_ref[n, rows, :]
        q4 = jnp.concatenate([jnp.where(head_lane[h], q, 0.0) for h in range(MLSTM_HEADS)], axis=0)
        kn = jnp.concatenate([ks, jnp.broadcast_to(n_row, (2 * SUBLANE, MLSTM_W))], axis=0)
        z = _dot_dims(kn, q4, NT_DIMS)
        s_t, qn = z[:CHUNK], z[CHUNK:CHUNK + 1]
        yield
        rc = r_exp[rows]
        rmax_t = jnp.max(jnp.where(causal_t, rc, -jnp.inf), axis=0, keepdims=True)
        rmax = jnp.max(rc, axis=0, keepdims=True)
        p_t = s_t * jnp.exp(jnp.where(causal_t, rc - rmax_t, -jnp.inf))
        row_sum = jnp.sum(p_t, axis=0, keepdims=True)
        kw = ks * jnp.exp(rc - rmax)
        fc_row = fc_rows[c:c + 1]
        inter = fc_row + m_row
        intra = fc_row + rmax_t
        m_t = jnp.maximum(inter, intra)
        w_inter, w_intra = jnp.exp(inter - m_t), jnp.exp(intra - m_t)
        den = w_inter * qn + w_intra * row_sum
        inv = 1.0 / jnp.maximum(jnp.abs(den), jnp.exp(-m_t))
        full = _dot_dims(p_t * (w_intra * inv), v, TN_DIMS)
        yield
        out = _expand3(jnp.where(eye_t, w_inter * inv, 0.0), e_h) * _dot(q, ct)
        for h in range(MLSTM_HEADS):
            out = out + jnp.where(head_lane[h], full[h * CHUNK:(h + 1) * CHUNK], 0.0)
        h_ref[n, rows, :] = out
        m_max = jnp.maximum(m_row, rmax)
        keep, add = jnp.exp(m_row - m_max), jnp.exp(rmax - m_max)
        ct = keep * ct + add * jnp.where(block_diag, _dot_dims(kw, v, TN_DIMS), 0.0)
        n_row = keep * n_row + add * jnp.sum(kw, axis=0, keepdims=True)
        m_row = f_tot_rows[c:c + 1] + m_max
        yield
    ct_ref[n, d], n_ref[n, d], m_ref[n, d] = ct, n_row, m_row


def _mlstm_kernel(qf_ref, kf_ref, vf_ref, gcf_ref, grf_ref, qb_ref, kb_ref, vb_ref, gcb_ref, grb_ref,
                  bcol_ref, brow_ref, tbd_ref, tbdt_ref, er_ref, ef_ref, eh_ref,
                  hf_ref, hb_ref, ct_ref, n_ref, m_ref):
    @pl.when(pl.program_id(1) == 0)
    def _():
        ct_ref[...] = jnp.zeros(ct_ref.shape, F32)
        n_ref[...] = jnp.zeros(n_ref.shape, F32)
        m_ref[...] = jnp.zeros(m_ref.shape, F32)

    stages = []
    for n in range(qf_ref.shape[0]):
        stages.append(_mlstm_direction(
            qf_ref, kf_ref, vf_ref, gcf_ref[n] + bcol_ref[...], grf_ref[n, 0, 0] + brow_ref[0],
            hf_ref, tbd_ref[0], tbdt_ref[0], er_ref[0], ef_ref[0], eh_ref[...], ct_ref, n_ref, m_ref, 0, n))
        stages.append(_mlstm_direction(
            qb_ref, kb_ref, vb_ref, gcb_ref[n] + bcol_ref[...], grb_ref[n, 0, 0] + brow_ref[1],
            hb_ref, tbd_ref[1], tbdt_ref[1], er_ref[1], ef_ref[1], eh_ref[...], ct_ref, n_ref, m_ref, 1, n))
    _interleave(*stages)


def _mlstm(mls_proj, gate_proj, f_rows, bcol, brow, tbd, tbd_t, e_r, e_f, e_h):
    b, s, _ = mls_proj.shape
    nb = s // REC_BLOCK
    fwd = lambda col: pl.BlockSpec((REC_BATCH, REC_BLOCK, MLSTM_W), lambda bi, j: (bi, j, col))
    bwd = lambda col: pl.BlockSpec((REC_BATCH, REC_BLOCK, MLSTM_W), lambda bi, j: (bi, nb - 1 - j, col))
    const = lambda shape: pl.BlockSpec(shape, lambda bi, j: (0,) * len(shape))
    gcol_f = pl.BlockSpec((REC_BATCH, REC_BLOCK, GATE_W), lambda bi, j: (bi, j, 0))
    gcol_b = pl.BlockSpec((REC_BATCH, REC_BLOCK, GATE_W), lambda bi, j: (bi, nb - 1 - j, 0))
    frow_shape = (REC_BATCH, 1, 1) + f_rows.shape[3:]
    frow_f = pl.BlockSpec(frow_shape, lambda bi, j: (bi, 0, j, 0, 0))
    frow_b = pl.BlockSpec(frow_shape, lambda bi, j: (bi, 1, nb - 1 - j, 0, 0))
    out_shape = jax.ShapeDtypeStruct((b, s, MLSTM_W), F32)
    return pl.pallas_call(
        _mlstm_kernel,
        grid=(b // REC_BATCH, nb),
        in_specs=[fwd(0), fwd(1), fwd(2), gcol_f, frow_f, bwd(0), bwd(1), bwd(2), gcol_b, frow_b,
                  const(bcol.shape), const(brow.shape), const(tbd.shape), const(tbd_t.shape),
                  const(e_r.shape), const(e_f.shape), const(e_h.shape)],
        out_specs=[fwd(0), bwd(0)],
        out_shape=[out_shape, out_shape],
        scratch_shapes=[pltpu.VMEM((REC_BATCH, 2, MLSTM_W, MLSTM_W), F32),
                        pltpu.VMEM((REC_BATCH, 2, 1, MLSTM_W), F32),
                        pltpu.VMEM((REC_BATCH, 2, 1, MLSTM_W), F32)],
        compiler_params=_params("parallel", "arbitrary"),
        name="mlstm",
    )(mls_proj, mls_proj, mls_proj, gate_proj, f_rows,
      mls_proj, mls_proj, mls_proj, gate_proj, f_rows, bcol, brow, tbd, tbd_t, e_r, e_f, e_h)


def _outproj_kernel(x_ref, attn_ref, gof_ref, gob_ref, gg_ref, mhf_ref, mhb_ref, mo_ref,
                    gng_ref, mng_ref, bd_ref, wo_ref, out_ref):
    bd = bd_ref[...]

    def head_norm(t, gain):
        return t * lax.rsqrt(_group_mean(t * t, bd) + EPS) * gain

    gg = gg_ref[0]
    gla = head_norm(gof_ref[0] + gob_ref[0], gng_ref[...]) * (gg * _sigmoid(gg))
    mls = head_norm(_sigmoid(mo_ref[0]) * (mhf_ref[0] + mhb_ref[0]), mng_ref[...])
    acc = jnp.dot(attn_ref[0], wo_ref[:ATTN_W, :], preferred_element_type=F32)
    acc += jnp.dot(gla.astype(BF16), wo_ref[ATTN_W:ATTN_W + GLA_W, :], preferred_element_type=F32)
    acc += jnp.dot(mls.astype(BF16), wo_ref[ATTN_W + GLA_W:, :], preferred_element_type=F32)
    out_ref[0] = x_ref[0] + acc


def _outproj(x, attn, gof, gob, gla_proj, mhf, mhb, mls_proj, gng, mng, bd, wo):
    b, s, d = x.shape
    row = lambda n, col=0: pl.BlockSpec((1, ROW_TILE, n), lambda bi, i: (bi, i, col))
    const = lambda shape: pl.BlockSpec(shape, lambda bi, i: (0,) * len(shape))
    return pl.pallas_call(
        _outproj_kernel,
        grid=(b, s // ROW_TILE),
        in_specs=[row(d), row(ATTN_W), row(GLA_W), row(GLA_W), row(GLA_W, 3),
                  row(MLSTM_W), row(MLSTM_W), row(MLSTM_W, 3),
                  const((1, GLA_W)), const((1, MLSTM_W)), const(bd.shape), const(wo.shape)],
        out_specs=row(d),
        out_shape=jax.ShapeDtypeStruct((b, s, d), F32),
        compiler_params=_params("parallel", "parallel"),
        name="outproj",
    )(x, attn, gof, gob, gla_proj, mhf, mhb, mls_proj, gng, mng, bd, wo)


def _ffn_kernel(x_ref, prev_ref, next_ref, g_ref, wg_ref, wv_ref, cwg_ref, cwv_ref, cbg_ref, cbv_ref,
                wd_ref, out_ref, h_ref, ug_ref, uv_ref, acc_ref):
    i, j = pl.program_id(1), pl.program_id(2)
    tm = x_ref.shape[1]

    def normed(t):
        ms = jnp.mean(t * t, axis=-1, keepdims=True)
        return t * lax.rsqrt(ms + EPS) * g_ref[...]

    @pl.when(j == 0)
    def _():
        keep_prev = (i > 0).astype(F32)
        keep_next = (i < pl.num_programs(1) - 1).astype(F32)
        h_ref[0:HALO, :] = (normed(prev_ref[0]) * keep_prev).astype(BF16)
        h_ref[HALO:HALO + tm, :] = normed(x_ref[0]).astype(BF16)
        h_ref[HALO + tm:, :] = (normed(next_ref[0]) * keep_next).astype(BF16)
        acc_ref[...] = jnp.zeros(acc_ref.shape, F32)

    h = h_ref[...]
    ug_ref[...] = jnp.dot(h, wg_ref[...], preferred_element_type=F32)
    uv_ref[...] = jnp.dot(h, wv_ref[...], preferred_element_type=F32)

    def conv(u_ref, cw_ref, cb_ref):
        out = cb_ref[...] + u_ref[HALO - 1:HALO - 1 + tm, :] * cw_ref[0:1, :]
        for t in range(1, CONV_WIDTH):
            out = out + u_ref[HALO - 1 + t:HALO - 1 + t + tm, :] * cw_ref[t:t + 1, :]
        return out

    gate = conv(ug_ref, cwg_ref, cbg_ref)
    val = conv(uv_ref, cwv_ref, cbv_ref)
    act = (gate * _sigmoid(gate) * val).astype(BF16)
    acc_ref[...] += jnp.dot(act, wd_ref[...], preferred_element_type=F32)

    @pl.when(j == pl.num_programs(2) - 1)
    def _():
        out_ref[0] = x_ref[0] + acc_ref[...]


def _ffn(x, g, w_up, conv_w, conv_b, w_down):
    b, s, d = x.shape
    nj = D_FF // FF_BLOCK
    tiles_per_halo = FFN_ROW_TILE // HALO
    last_halo = s // HALO - 1
    row = pl.BlockSpec((1, FFN_ROW_TILE, d), lambda bi, i, j: (bi, i, 0))
    prev = pl.BlockSpec((1, HALO, d), lambda bi, i, j: (bi, jnp.maximum(i * tiles_per_halo - 1, 0), 0))
    nxt = pl.BlockSpec((1, HALO, d),
                       lambda bi, i, j: (bi, jnp.minimum((i + 1) * tiles_per_halo, last_halo), 0))
    gate_cols = lambda rows: pl.BlockSpec((rows, FF_BLOCK), lambda bi, i, j: (0, j))
    val_cols = lambda rows: pl.BlockSpec((rows, FF_BLOCK), lambda bi, i, j: (0, nj + j))
    return pl.pallas_call(
        _ffn_kernel,
        grid=(b, s // FFN_ROW_TILE, nj),
        in_specs=[row, prev, nxt, pl.BlockSpec((1, d), lambda bi, i, j: (0, 0)),
                  gate_cols(d), val_cols(d), gate_cols(CONV_WIDTH), val_cols(CONV_WIDTH),
                  gate_cols(1), val_cols(1),
                  pl.BlockSpec((FF_BLOCK, d), lambda bi, i, j: (j, 0))],
        out_specs=row,
        out_shape=jax.ShapeDtypeStruct((b, s, d), F32),
        scratch_shapes=[pltpu.VMEM((FFN_ROW_TILE + 2 * HALO, d), BF16),
                        pltpu.VMEM((FFN_ROW_TILE + 2 * HALO, FF_BLOCK), F32),
                        pltpu.VMEM((FFN_ROW_TILE + 2 * HALO, FF_BLOCK), F32),
                        pltpu.VMEM((FFN_ROW_TILE, d), F32)],
        compiler_params=_params("parallel", "parallel", "arbitrary"),
        name="ffn",
    )(x, x, x, g, w_up, w_up, conv_w, conv_w, conv_b, conv_b, w_down)


def _rope_tables(seq):
    n_rows = seq // GRID_W
    row = jnp.repeat(jnp.arange(n_rows, dtype=F32), GRID_W)
    col = jnp.tile(jnp.arange(GRID_W, dtype=F32), n_rows)
    n_freq = HEAD_DIM // 4
    inv_freq = jnp.power(ROPE_THETA, -jnp.arange(n_freq, dtype=F32) / n_freq)
    ang_r, ang_c = row[:, None] * inv_freq, col[:, None] * inv_freq
    cos_h = jnp.concatenate([jnp.cos(ang_r), jnp.cos(ang_r), jnp.cos(ang_c), jnp.cos(ang_c)], axis=1)
    sin_h = jnp.concatenate([-jnp.sin(ang_r), jnp.sin(ang_r), -jnp.sin(ang_c), jnp.sin(ang_c)], axis=1)
    reps = QK_W // HEAD_DIM
    return jnp.tile(cos_h, (1, reps)), jnp.tile(sin_h, (1, reps))


def _block_diag_mean(width):
    group = np.arange(width) // HEAD_DIM
    return jnp.asarray((group[:, None] == group[None, :]).astype(np.float32) / HEAD_DIM, dtype=BF16)


def _tri_constants():
    t = np.arange(REC_BLOCK)
    same_chunk = (t[:, None] // CHUNK) == (t[None, :] // CHUNK)
    prefix = (same_chunk & (t[None, :] <= t[:, None])).astype(np.float32)
    suffix = (same_chunk & (t[None, :] >= t[:, None])).astype(np.float32)
    tbd = jnp.asarray(np.stack([prefix, suffix]), dtype=BF16)
    tbd_t = jnp.asarray(np.stack([prefix.T, suffix.T]), dtype=BF16)
    return tbd, tbd_t


def _mlstm_expanders():
    e_r = np.zeros((2, GATE_W, MLSTM_W), np.float32)
    e_f = np.zeros((2, GATE_W, MLSTM_W), np.float32)
    for d in range(2):
        for h in range(MLSTM_HEADS):
            lanes = slice(h * HEAD_DIM, (h + 1) * HEAD_DIM)
            i_lane = 2 * GLA_RANK + d * MLSTM_HEADS + h
            f_lane = i_lane + 2 * MLSTM_HEADS
            e_r[d, i_lane, lanes] = 1.0
            e_r[d, f_lane, lanes] = -1.0
            e_f[d, f_lane, lanes] = 1.0
    group = np.arange(MLSTM_W) // HEAD_DIM
    e_h = (group[:, None] == group[None, :]).astype(np.float32)
    return jnp.asarray(e_r, dtype=BF16), jnp.asarray(e_f, dtype=BF16), jnp.asarray(e_h, dtype=BF16)


def _layer(x, consts, norm_mix_g, w_in, attn_qn_g, attn_kn_g, gla_gate_w2, gla_gate_b, gla_norm_g,
           mlstm_gate_b, mlstm_norm_g, w_out, norm_ffn_g, w_up, conv_w, conv_b, w_down):
    cos_t, sin_t, bd_qk, bd_head, tbd, tbd_t, e_r, e_f, e_h = consts
    b, s, _ = x.shape
    n_gates = 4 * MLSTM_HEADS
    main_w = QKV_W + 4 * GLA_W
    gla_a = w_in[:, main_w:main_w + 2 * GLA_RANK]
    mls_main = w_in[:, main_w + 2 * GLA_RANK:main_w + 2 * GLA_RANK + 4 * MLSTM_W]
    mls_g = w_in[:, main_w + 2 * GLA_RANK + 4 * MLSTM_W:]
    pad = jnp.zeros((D_MODEL, GATE_W - 2 * GLA_RANK - n_gates), F32)
    w_in_r = jnp.concatenate([w_in[:, :main_w], mls_main, gla_a, mls_g, pad], axis=1).astype(BF16)

    qkv, gla_proj, mls_proj, gate_proj = _inproj(x, norm_mix_g[None, :], w_in_r)

    qk_gain = jnp.concatenate([jnp.tile(attn_qn_g, ATTN_HEADS), jnp.tile(attn_kn_g, ATTN_KV_HEADS)])[None, :]
    q_t, k, v_t = _qkprep(qkv, cos_t, sin_t, qk_gain, bd_qk)
    score_bound = (HEAD_DIM * Q_SCALE * LOG2_E) * jnp.max(jnp.abs(attn_qn_g)) * jnp.max(jnp.abs(attn_kn_g))
    bounded = (score_bound < ATTN_MAX_SCORE_BOUND).astype(jnp.int32).reshape(1)
    attn = _attention(bounded, q_t, k, v_t)

    w2ext = jnp.zeros((2, GATE_W, GLA_W), F32)
    w2ext = w2ext.at[0, :GLA_RANK].set(gla_gate_w2[0]).at[1, GLA_RANK:2 * GLA_RANK].set(gla_gate_w2[1])
    w2_hi = w2ext.astype(BF16)
    w2_lo = (w2ext - w2_hi.astype(F32)).astype(BF16)
    cum_f, cum_b = _glagate(gate_proj, w2_hi, w2_lo, gla_gate_b[:, None, :], tbd)
    gof, gob = _gla(gla_proj, cum_f, cum_b)

    nb, cpb = s // REC_BLOCK, REC_BLOCK // CHUNK
    f_lo = 2 * GLA_RANK + 2 * MLSTM_HEADS
    f_rows = gate_proj[:, :, f_lo:f_lo + 2 * MLSTM_HEADS].reshape(b, nb, cpb, CHUNK, 2, MLSTM_HEADS)
    f_rows = f_rows.transpose(0, 4, 1, 2, 5, 3).reshape(b, 2, nb, cpb, MLSTM_W)
    f_rows = jnp.pad(f_rows, ((0, 0), (0, 0), (0, 0), (0, SUBLANE - cpb), (0, 0)))
    bcol = jnp.zeros((1, GATE_W), F32).at[0, 2 * GLA_RANK:2 * GLA_RANK + n_gates].set(
        mlstm_gate_b.reshape(n_gates))
    brow = jnp.repeat(mlstm_gate_b[2:], HEAD_DIM, axis=1)[:, None, :]
    mhf, mhb = _mlstm(mls_proj, gate_proj, f_rows, bcol, brow, tbd, tbd_t, e_r, e_f, e_h)

    x = _outproj(x, attn, gof, gob, gla_proj, mhf, mhb, mls_proj,
                 jnp.tile(gla_norm_g, GLA_HEADS)[None, :], jnp.tile(mlstm_norm_g, MLSTM_HEADS)[None, :],
                 bd_head, w_out.astype(BF16))
    return _ffn(x, norm_ffn_g[None, :], w_up.astype(BF16), conv_w, conv_b[None, :], w_down.astype(BF16))


def _trunk(x, weights):
    seq = x.shape[1]
    cos_t, sin_t = _rope_tables(seq)
    consts = ((cos_t, sin_t, _block_diag_mean(QK_W), _block_diag_mean(GLA_W))
              + _tri_constants() + _mlstm_expanders())
    depth = weights[0].shape[0]
    for l in range(depth):
        x = _layer(x, consts, *[w[l] for w in weights])
    return x


def kernel(x_prompt, x_sample, norm_mix_g, w_in, attn_qn_g, attn_kn_g, gla_gate_w2, gla_gate_b,
           gla_norm_g, mlstm_gate_b, mlstm_norm_g, w_out, norm_ffn_g, w_up, conv_w, conv_b, w_down):
    weights = (norm_mix_g, w_in, attn_qn_g, attn_kn_g, gla_gate_w2, gla_gate_b, gla_norm_g,
               mlstm_gate_b, mlstm_norm_g, w_out, norm_ffn_g, w_up, conv_w, conv_b, w_down)
    return _trunk(x_prompt, weights), _trunk(x_sample, weights)
```

```python
import functools

import jax
import jax.numpy as jnp
import numpy as np
from jax import lax
from jax.experimental import pallas as pl
from jax.experimental.pallas import tpu as pltpu

F32 = jnp.float32
BF16 = jnp.bfloat16

D_MODEL = 1024
GRID_W = 64
HEAD_DIM = 64
EPS = 1e-6
ATTN_HEADS = 8
ATTN_KV_HEADS = 2
ATTN_GROUP = ATTN_HEADS // ATTN_KV_HEADS
ROPE_THETA = 10000.0
GLA_HEADS = 4
GLA_RANK = 16
GLA_TAU = 16.0
MLSTM_HEADS = 4
CHUNK = 64
GLA_MAX_EXPONENT = 40.0
ATTN_W = ATTN_HEADS * HEAD_DIM
KV_W = ATTN_KV_HEADS * HEAD_DIM
GLA_W = GLA_HEADS * HEAD_DIM
MLSTM_W = MLSTM_HEADS * HEAD_DIM
QK_W = ATTN_W + KV_W
QKV_W = ATTN_W + 2 * KV_W
D_FF = 2816
CONV_WIDTH = 3
GATE_W = 128
Q_SCALE = HEAD_DIM ** -0.5
LOG2_E = 1.4426950408889634

LANE = 128
SUBLANE = 8
VMEM_LIMIT = 56 * 1024 * 1024

ROW_TILE = 1024
FFN_ROW_TILE = 512
REC_BLOCK = 256
REC_BATCH = 4
ATTN_TQ = 512
ATTN_STREAM_LANES = 256
ATTN_TK = 256
VT_ROWS = HEAD_DIM + 16
KEY_W = 128
ATTN_MAX_SCORE_BOUND = 32.0
FF_BLOCK = 1408
HALO = SUBLANE

NT_DIMS = (((1,), (1,)), ((), ()))
TN_DIMS = (((0,), (0,)), ((), ()))


def _params(*sem):
    return pltpu.CompilerParams(dimension_semantics=sem, vmem_limit_bytes=VMEM_LIMIT)


def _log_sigmoid(z):
    return jnp.minimum(z, 0.0) - jnp.log(1.0 + jnp.exp(-jnp.abs(z)))


def _sigmoid(z):
    return 1.0 / (1.0 + jnp.exp(-z))


def _dot(a, b):
    return jnp.dot(a.astype(BF16), b.astype(BF16), preferred_element_type=F32)


def _dot_dims(a, b, dims):
    return lax.dot_general(a.astype(BF16), b.astype(BF16), dims, preferred_element_type=F32)


def _split3(x):
    p1 = x.astype(BF16)
    rest = x - p1.astype(F32)
    p2 = rest.astype(BF16)
    return p1, p2, (rest - p2.astype(F32)).astype(BF16)


def _cumsum_cols(tbd, x):
    return sum(jnp.dot(tbd, p, preferred_element_type=F32) for p in _split3(x))


def _cumsum_rows(x, tbd_t):
    return sum(jnp.dot(p, tbd_t, preferred_element_type=F32) for p in _split3(x))


def _dot_split(a, b_hi, b_lo):
    a_hi = a.astype(BF16)
    a_lo = (a - a_hi.astype(F32)).astype(BF16)
    return (jnp.dot(a_hi, b_hi, preferred_element_type=F32)
            + jnp.dot(a_lo, b_hi, preferred_element_type=F32)
            + jnp.dot(a_hi, b_lo, preferred_element_type=F32))


def _group_mean(sq, bd):
    hi = sq.astype(BF16)
    lo = (sq - hi.astype(F32)).astype(BF16)
    return (jnp.dot(hi, bd, preferred_element_type=F32)
            + jnp.dot(lo, bd, preferred_element_type=F32))


def _inproj_kernel(x_ref, g_ref, w_ref, qkv_ref, gla_ref, mls_ref, gate_ref):
    x = x_ref[0]
    ms = jnp.mean(x * x, axis=-1, keepdims=True)
    h = (x * lax.rsqrt(ms + EPS) * g_ref[...]).astype(BF16)
    col = 0
    for ref in (qkv_ref, gla_ref, mls_ref, gate_ref):
        width = ref.shape[-1]
        ref[0] = jnp.dot(h, w_ref[:, col:col + width], preferred_element_type=F32)
        col += width


def _inproj(x, g, w):
    b, s, d = x.shape
    widths = (QKV_W, 4 * GLA_W, 4 * MLSTM_W, GATE_W)
    row = lambda n: pl.BlockSpec((1, ROW_TILE, n), lambda bi, i: (bi, i, 0))
    const = lambda shape: pl.BlockSpec(shape, lambda bi, i: (0,) * len(shape))
    return pl.pallas_call(
        _inproj_kernel,
        grid=(b, s // ROW_TILE),
        in_specs=[row(d), const((1, d)), const(w.shape)],
        out_specs=[row(n) for n in widths],
        out_shape=[jax.ShapeDtypeStruct((b, s, n), F32) for n in widths],
        compiler_params=_params("parallel", "parallel"),
        name="inproj",
    )(x, g, w)


def _qkprep_kernel(qkv_ref, cos_ref, sin_ref, gain_ref, bd_ref, qt_ref, k_ref, vt_ref):
    x = qkv_ref[0]
    qk = x[:, :QK_W]
    ms = _group_mean(qk * qk, bd_ref[...])
    y = qk * lax.rsqrt(ms + EPS) * gain_ref[...]
    lane = lax.broadcasted_iota(jnp.int32, y.shape, 1)
    first_half = (lane & 31) < 16
    quarter = HEAD_DIM // 4
    swapped = jnp.where(first_half, pltpu.roll(y, QK_W - quarter, 1), pltpu.roll(y, quarter, 1))
    r = y * cos_ref[...] + swapped * sin_ref[...]
    q_t = (r[:, :ATTN_W] * (Q_SCALE * LOG2_E)).T.astype(BF16)
    qt_ref[0] = q_t.reshape(ATTN_HEADS, HEAD_DIM, q_t.shape[-1])
    v_t = x[:, QK_W:].T.astype(BF16)
    v_t = v_t.reshape(ATTN_KV_HEADS, HEAD_DIM, v_t.shape[-1])
    row = lax.broadcasted_iota(jnp.int32, (ATTN_KV_HEADS, VT_ROWS - HEAD_DIM, v_t.shape[-1]), 1)
    vt_ref[0] = jnp.concatenate([v_t, jnp.where(row == 0, 1.0, 0.0).astype(BF16)], axis=1)
    ones_lane = lax.broadcasted_iota(jnp.int32, (x.shape[0], KEY_W - HEAD_DIM), 1) == 0
    k_pad = jnp.where(ones_lane, 1.0, 0.0).astype(BF16)
    for h in range(ATTN_KV_HEADS):
        k_h = r[:, ATTN_W + h * HEAD_DIM:ATTN_W + (h + 1) * HEAD_DIM].astype(BF16)
        k_ref[0, h] = jnp.concatenate([k_h, k_pad], axis=1)


def _qkprep(qkv, cos_t, sin_t, gain, bd):
    b, s, _ = qkv.shape
    const = lambda shape: pl.BlockSpec(shape, lambda bi, i: (0,) * len(shape))
    tab = pl.BlockSpec((ROW_TILE, QK_W), lambda bi, i: (i, 0))
    t_spec = lambda heads, rows: pl.BlockSpec((1, heads, rows, ROW_TILE), lambda bi, i: (bi, 0, 0, i))
    t_shape = lambda heads, rows: jax.ShapeDtypeStruct((b, heads, rows, s), BF16)
    return pl.pallas_call(
        _qkprep_kernel,
        grid=(b, s // ROW_TILE),
        in_specs=[pl.BlockSpec((1, ROW_TILE, QKV_W), lambda bi, i: (bi, i, 0)),
                  tab, tab, const((1, QK_W)), const((QK_W, QK_W))],
        out_specs=[t_spec(ATTN_HEADS, HEAD_DIM),
                   pl.BlockSpec((1, ATTN_KV_HEADS, ROW_TILE, KEY_W), lambda bi, i: (bi, 0, i, 0)),
                   t_spec(ATTN_KV_HEADS, VT_ROWS)],
        out_shape=[t_shape(ATTN_HEADS, HEAD_DIM),
                   jax.ShapeDtypeStruct((b, ATTN_KV_HEADS, s, KEY_W), BF16),
                   t_shape(ATTN_KV_HEADS, VT_ROWS)],
        compiler_params=_params("parallel", "parallel"),
        name="qkprep",
    )(qkv, cos_t, sin_t, gain, bd)


def _attn_kernel(bounded_ref, qt_ref, k_ref, vt_ref, o_ref):
    tq = qt_ref.shape[-1]
    n_chunks = k_ref.shape[2] // ATTN_TK
    keys = lambda c: slice(c * ATTN_TK, (c + 1) * ATTN_TK)
    pad_rows = KEY_W - HEAD_DIM
    lanes = ATTN_STREAM_LANES

    def finish(acc, head, lo):
        out = (acc[:HEAD_DIM] / acc[HEAD_DIM:HEAD_DIM + 1]).T
        o_ref[0, lo:lo + lanes, head * HEAD_DIM:(head + 1) * HEAD_DIM] = out.astype(o_ref.dtype)

    def scores(c, q_ext):
        return jnp.dot(k_ref[0, 0, keys(c), :], q_ext, preferred_element_type=F32)

    def weighted_values(c, p_t):
        return jnp.dot(vt_ref[0, 0, :, keys(c)], p_t, preferred_element_type=F32)

    def fixed_shift_stream(head, lo):
        q_t = qt_ref[0, head, :, lo:lo + lanes]
        s0 = scores(0, jnp.concatenate([q_t, jnp.zeros((pad_rows, lanes), BF16)], axis=0))
        shift = jnp.max(s0, axis=0, keepdims=True).astype(BF16).astype(F32)
        yield
        first_row = lax.broadcasted_iota(jnp.int32, (pad_rows, lanes), 0) == 0
        q_ext = jnp.concatenate([q_t, jnp.where(first_row, -shift, 0.0).astype(BF16)], axis=0)
        nxt = scores(1, q_ext)
        acc = weighted_values(0, jnp.exp2(s0 - shift).astype(BF16))
        yield
        for c in range(1, n_chunks):
            s_t = nxt
            if c + 1 < n_chunks:
                nxt = scores(c + 1, q_ext)
            p_t = jnp.exp2(s_t).astype(BF16)
            yield
            acc = acc + weighted_values(c, p_t)
            yield
        finish(acc, head, lo)

    def running_max_stream(head, lo):
        q_ext = jnp.concatenate([qt_ref[0, head, :, lo:lo + lanes], jnp.zeros((pad_rows, lanes), BF16)], axis=0)
        m = jnp.full((1, lanes), -jnp.inf, F32)
        acc = jnp.zeros((VT_ROWS, lanes), F32)
        nxt = scores(0, q_ext)
        yield
        for c in range(n_chunks):
            s_t = nxt
            if c + 1 < n_chunks:
                nxt = scores(c + 1, q_ext)
            m_new = jnp.maximum(m, jnp.max(s_t, axis=0, keepdims=True))
            yield
            p_t = jnp.exp2((s_t - m_new).astype(BF16))
            yield
            acc = jnp.exp2(m - m_new) * acc + weighted_values(c, p_t)
            m = m_new
            yield
        finish(acc, head, lo)

    def run(stream):
        _interleave(*[stream(j, lo) for j in range(ATTN_GROUP) for lo in range(0, tq, lanes)])

    pl.when(bounded_ref[0] == 1)(functools.partial(run, fixed_shift_stream))
    pl.when(bounded_ref[0] == 0)(functools.partial(run, running_max_stream))


def _attention(bounded, q_t, k, v_t):
    b, _, _, s = q_t.shape
    gw = ATTN_GROUP * HEAD_DIM
    return pl.pallas_call(
        _attn_kernel,
        grid=(b, ATTN_KV_HEADS, s // ATTN_TQ),
        in_specs=[pl.BlockSpec(memory_space=pltpu.SMEM),
                  pl.BlockSpec((1, ATTN_GROUP, HEAD_DIM, ATTN_TQ), lambda bi, h, i: (bi, h, 0, i)),
                  pl.BlockSpec((1, 1, s, KEY_W), lambda bi, h, i: (bi, h, 0, 0)),
                  pl.BlockSpec((1, 1, VT_ROWS, s), lambda bi, h, i: (bi, h, 0, 0))],
        out_specs=pl.BlockSpec((1, ATTN_TQ, gw), lambda bi, h, i: (bi, i, h)),
        out_shape=jax.ShapeDtypeStruct((b, s, ATTN_W), BF16),
        compiler_params=_params("parallel", "parallel", "arbitrary"),
        name="attention",
    )(bounded, q_t, k, v_t)


def _chunk_order(d):
    n_chunks = REC_BLOCK // CHUNK
    return range(n_chunks) if d == 0 else range(n_chunks - 1, -1, -1)


def _glagate_kernel(a_ref, w2hi_ref, w2lo_ref, gb_ref, tbd_ref, cf_ref, cb_ref):
    a = a_ref[0]
    for d, out_ref in enumerate((cf_ref, cb_ref)):
        z = _dot_split(a, w2hi_ref[d], w2lo_ref[d]) + gb_ref[d]
        g = _log_sigmoid(z) * (1.0 / GLA_TAU)
        for sb in range(a.shape[0] // REC_BLOCK):
            rows = slice(sb * REC_BLOCK, (sb + 1) * REC_BLOCK)
            out_ref[0, rows, :] = _cumsum_cols(tbd_ref[d], g[rows])


def _glagate(gate_proj, w2_hi, w2_lo, gb, tbd):
    b, s, _ = gate_proj.shape
    const = lambda shape: pl.BlockSpec(shape, lambda bi, i: (0,) * len(shape))
    out_spec = pl.BlockSpec((1, ROW_TILE, GLA_W), lambda bi, i: (bi, i, 0))
    out_shape = jax.ShapeDtypeStruct((b, s, GLA_W), F32)
    return pl.pallas_call(
        _glagate_kernel,
        grid=(b, s // ROW_TILE),
        in_specs=[pl.BlockSpec((1, ROW_TILE, GATE_W), lambda bi, i: (bi, i, 0)),
                  const(w2_hi.shape), const(w2_lo.shape), const(gb.shape), const(tbd.shape)],
        out_specs=[out_spec, out_spec],
        out_shape=[out_shape, out_shape],
        compiler_params=_params("parallel", "parallel"),
        name="glagate",
    )(gate_proj, w2_hi, w2_lo, gb, tbd)


def _gla_exact_scores(q, cum, k_ref, cum_ref, n, base, d, eye_t, e_h, att_ref):
    sub = lax.broadcasted_iota(jnp.int32, q.shape, 0)

    def one_key(s, carry):
        k_s = k_ref[n, pl.ds(base + s, 1), :]
        c_s = cum_ref[n, pl.ds(base + s, 1), :]
        visible = (sub >= s) if d == 0 else (sub <= s)
        w = q * k_s * jnp.exp(jnp.where(visible, cum - c_s, -jnp.inf))
        per_head = _expand3(w, e_h)
        att_ref[pl.ds(s, 1), :] = jnp.sum(jnp.where(eye_t, per_head, 0.0), axis=0, keepdims=True)
        return carry

    lax.fori_loop(0, CHUNK, one_key, 0)
    return att_ref[...]


def _gla_exponent_range(cum_ref, n):
    worst = jnp.zeros((1, GLA_W), F32)
    for i in range(REC_BLOCK // CHUNK):
        ref = cum_ref[n, i * CHUNK + CHUNK // 2:i * CHUNK + CHUNK // 2 + 1, :]
        for end in (i * CHUNK, (i + 1) * CHUNK - 1):
            worst = jnp.maximum(worst, jnp.abs(cum_ref[n, end:end + 1, :] - ref))
    return worst


def _gla_direction(q_ref, k_ref, v_ref, cum_ref, o_ref, st_ref, d, n, exact_ref=None):
    causal_t, eye_t, head_lane, block_diag = _head_layout_masks(d)
    last = CHUNK - 1 if d == 0 else 0
    st = st_ref[n, d]
    for i in _chunk_order(d):
        rows = slice(i * CHUNK, (i + 1) * CHUNK)
        cum = cum_ref[n, rows, :]
        total = cum[last:last + 1]
        q = q_ref[n, rows, :] * Q_SCALE
        k = k_ref[n, rows, :]
        v = v_ref[n, rows, :]
        qe = q * jnp.exp(cum)
        kd = k * jnp.exp(total - cum)
        if exact_ref is None:
            ref = cum[CHUNK // 2:CHUNK // 2 + 1]
            qt = q * jnp.exp(cum - ref)
            kt = k * jnp.exp(ref - cum)
            q4 = jnp.concatenate([jnp.where(head_lane[h], qt, 0.0) for h in range(GLA_HEADS)], axis=0)
            att_t = jnp.where(causal_t, _dot_dims(kt, q4, NT_DIMS), 0.0)
        else:
            att_t = _gla_exact_scores(q, cum, k_ref, cum_ref, n, i * CHUNK, d, eye_t, *exact_ref)
        yield
        full = _dot_dims(att_t, v, TN_DIMS)
        out = _dot_dims(qe, st, NT_DIMS)
        yield
        for h in range(GLA_HEADS):
            out = out + jnp.where(head_lane[h], full[h * CHUNK:(h + 1) * CHUNK], 0.0)
        o_ref[n, rows, :] = out
        st = st * jnp.exp(total) + jnp.where(block_diag, _dot_dims(v, kd, TN_DIMS), 0.0)
        yield
    st_ref[n, d] = st


def _interleave(*stages):
    stages = list(stages)
    while stages:
        for g in list(stages):
            if next(g, StopIteration) is StopIteration:
                stages.remove(g)


def _gla_kernel(qf_ref, kf_ref, vf_ref, cf_ref, qb_ref, kb_ref, vb_ref, cb_ref, eh_ref,
                of_ref, ob_ref, st_ref, att_ref):
    @pl.when(pl.program_id(1) == 0)
    def _():
        st_ref[...] = jnp.zeros(st_ref.shape, F32)

    n_seq = qf_ref.shape[0]
    worst = jnp.zeros((1, GLA_W), F32)
    for n in range(n_seq):
        worst = jnp.maximum(worst, jnp.maximum(_gla_exponent_range(cf_ref, n), _gla_exponent_range(cb_ref, n)))
    factorisable = jnp.max(worst) < GLA_MAX_EXPONENT

    def run(exact_ref):
        stages = []
        for n in range(n_seq):
            stages.append(_gla_direction(qf_ref, kf_ref, vf_ref, cf_ref, of_ref, st_ref, 0, n, exact_ref))
            stages.append(_gla_direction(qb_ref, kb_ref, vb_ref, cb_ref, ob_ref, st_ref, 1, n, exact_ref))
        if exact_ref is None:
            _interleave(*stages)
        else:
            for stage in stages:
                _interleave(stage)

    pl.when(factorisable)(functools.partial(run, None))
    pl.when(jnp.logical_not(factorisable))(functools.partial(run, (eh_ref[...], att_ref)))


def _gla(gla_proj, cum_f, cum_b, e_h):
    b, s, _ = gla_proj.shape
    nb = s // REC_BLOCK
    fwd = lambda col: pl.BlockSpec((REC_BATCH, REC_BLOCK, GLA_W), lambda bi, j: (bi, j, col))
    bwd = lambda col: pl.BlockSpec((REC_BATCH, REC_BLOCK, GLA_W), lambda bi, j: (bi, nb - 1 - j, col))
    out_shape = jax.ShapeDtypeStruct((b, s, GLA_W), F32)
    return pl.pallas_call(
        _gla_kernel,
        grid=(b // REC_BATCH, nb),
        in_specs=[fwd(0), fwd(1), fwd(2), fwd(0), bwd(0), bwd(1), bwd(2), bwd(0),
                  pl.BlockSpec(e_h.shape, lambda bi, j: (0, 0))],
        out_specs=[fwd(0), bwd(0)],
        out_shape=[out_shape, out_shape],
        scratch_shapes=[pltpu.VMEM((REC_BATCH, 2, GLA_W, GLA_W), F32),
                        pltpu.VMEM((CHUNK, GLA_W), F32)],
        compiler_params=_params("parallel", "arbitrary"),
        name="gla",
    )(gla_proj, gla_proj, gla_proj, cum_f, gla_proj, gla_proj, gla_proj, cum_b, e_h)


def _head_layout_masks(d):
    sub = lax.broadcasted_iota(jnp.int32, (CHUNK, MLSTM_W), 0)
    lane = lax.broadcasted_iota(jnp.int32, (CHUNK, MLSTM_W), 1)
    t_lane = lane & (HEAD_DIM - 1)
    causal_t = (sub <= t_lane) if d == 0 else (sub >= t_lane)
    eye_t = sub == t_lane
    head_lane = [(lane >> 6) == h for h in range(MLSTM_HEADS)]
    brow = lax.broadcasted_iota(jnp.int32, (MLSTM_W, MLSTM_W), 0) >> 6
    bcol = lax.broadcasted_iota(jnp.int32, (MLSTM_W, MLSTM_W), 1) >> 6
    return causal_t, eye_t, head_lane, brow == bcol


def _expand3(x, e):
    return sum(jnp.dot(p, e, preferred_element_type=F32) for p in _split3(x))


def _mlstm_direction(q_ref, k_ref, v_ref, gates, frow, h_ref, tbd, tbd_t, e_r, e_f, e_h,
                     ct_ref, n_ref, m_ref, d, n):
    causal_t, eye_t, head_lane, block_diag = _head_layout_masks(d)
    last = CHUNK - 1 if d == 0 else 0
    n_chunks = REC_BLOCK // CHUNK
    lane = lax.broadcasted_iota(jnp.int32, gates.shape, 1)
    i_lo = 2 * GLA_RANK + d * MLSTM_HEADS
    is_i_lane = (lane >= i_lo) & (lane < i_lo + MLSTM_HEADS)
    fcum = _cumsum_cols(tbd, _log_sigmoid(gates))
    r_exp = _expand3(jnp.where(is_i_lane, gates, fcum), e_r)
    f_last = jnp.concatenate([fcum[c * CHUNK + last:c * CHUNK + last + 1] for c in range(n_chunks)]
                             + [jnp.zeros((SUBLANE - n_chunks, fcum.shape[1]), F32)], axis=0)
    f_tot_rows = _expand3(f_last, e_f)
    fc_rows = _cumsum_rows(_log_sigmoid(frow), tbd_t)

    ct, n_row, m_row = ct_ref[n, d], n_ref[n, d], m_ref[n, d]
    for c in _chunk_order(d):
        rows = slice(c * CHUNK, (c + 1) * CHUNK)
        q, ks, v = q_ref[n, rows, :], k_ref[n, rows, :] * Q_SCALE, v_ref[n, rows, :]
        q4 = jnp.concatenate([jnp.where(head_lane[h], q, 0.0) for h in range(MLSTM_HEADS)], axis=0)
        kn = jnp.concatenate([ks, jnp.broadcast_to(n_row, (2 * SUBLANE, MLSTM_W))], axis=0)
        z = _dot_dims(kn, q4, NT_DIMS)
        s_t, qn = z[:CHUNK], z[CHUNK:CHUNK + 1]
        yield
        rc = r_exp[rows]
        rmax_t = jnp.max(jnp.where(causal_t, rc, -jnp.inf), axis=0, keepdims=True)
        rmax = jnp.max(rc, axis=0, keepdims=True)
        p_t = s_t * jnp.exp(jnp.where(causal_t, rc - rmax_t, -jnp.inf))
        row_sum = jnp.sum(p_t, axis=0, keepdims=True)
        kw = ks * jnp.exp(rc - rmax)
        fc_row = fc_rows[c:c + 1]
        inter = fc_row + m_row
        intra = fc_row + rmax_t
        m_t = jnp.maximum(inter, intra)
        w_inter, w_intra = jnp.exp(inter - m_t), jnp.exp(intra - m_t)
        den = w_inter * qn + w_intra * row_sum
        inv = 1.0 / jnp.maximum(jnp.abs(den), jnp.exp(-m_t))
        full = _dot_dims(p_t * (w_intra * inv), v, TN_DIMS)
        yield
        out = _expand3(jnp.where(eye_t, w_inter * inv, 0.0), e_h) * _dot(q, ct)
        for h in range(MLSTM_HEADS):
            out = out + jnp.where(head_lane[h], full[h * CHUNK:(h + 1) * CHUNK], 0.0)
        h_ref[n, rows, :] = out
        m_max = jnp.maximum(m_row, rmax)
        keep, add = jnp.exp(m_row - m_max), jnp.exp(rmax - m_max)
        ct = keep * ct + add * jnp.where(block_diag, _dot_dims(kw, v, TN_DIMS), 0.0)
        n_row = keep * n_row + add * jnp.sum(kw, axis=0, keepdims=True)
        m_row = f_tot_rows[c:c + 1] + m_max
        yield
    ct_ref[n, d], n_ref[n, d], m_ref[n, d] = ct, n_row, m_row


def _mlstm_kernel(qf_ref, kf_ref, vf_ref, gcf_ref, grf_ref, qb_ref, kb_ref, vb_ref, gcb_ref, grb_ref,
                  bcol_ref, brow_ref, tbd_ref, tbdt_ref, er_ref, ef_ref, eh_ref,
                  hf_ref, hb_ref, ct_ref, n_ref, m_ref):
    @pl.when(pl.program_id(1) == 0)
    def _():
        ct_ref[...] = jnp.zeros(ct_ref.shape, F32)
        n_ref[...] = jnp.zeros(n_ref.shape, F32)
        m_ref[...] = jnp.zeros(m_ref.shape, F32)

    stages = []
    for n in range(qf_ref.shape[0]):
        stages.append(_mlstm_direction(
            qf_ref, kf_ref, vf_ref, gcf_ref[n] + bcol_ref[...], grf_ref[n, 0, 0] + brow_ref[0],
            hf_ref, tbd_ref[0], tbdt_ref[0], er_ref[0], ef_ref[0], eh_ref[...], ct_ref, n_ref, m_ref, 0, n))
        stages.append(_mlstm_direction(
            qb_ref, kb_ref, vb_ref, gcb_ref[n] + bcol_ref[...], grb_ref[n, 0, 0] + brow_ref[1],
            hb_ref, tbd_ref[1], tbdt_ref[1], er_ref[1], ef_ref[1], eh_ref[...], ct_ref, n_ref, m_ref, 1, n))
    _interleave(*stages)


def _mlstm(mls_proj, gate_proj, f_rows, bcol, brow, tbd, tbd_t, e_r, e_f, e_h):
    b, s, _ = mls_proj.shape
    nb = s // REC_BLOCK
    fwd = lambda col: pl.BlockSpec((REC_BATCH, REC_BLOCK, MLSTM_W), lambda bi, j: (bi, j, col))
    bwd = lambda col: pl.BlockSpec((REC_BATCH, REC_BLOCK, MLSTM_W), lambda bi, j: (bi, nb - 1 - j, col))
    const = lambda shape: pl.BlockSpec(shape, lambda bi, j: (0,) * len(shape))
    gcol_f = pl.BlockSpec((REC_BATCH, REC_BLOCK, GATE_W), lambda bi, j: (bi, j, 0))
    gcol_b = pl.BlockSpec((REC_BATCH, REC_BLOCK, GATE_W), lambda bi, j: (bi, nb - 1 - j, 0))
    frow_shape = (REC_BATCH, 1, 1) + f_rows.shape[3:]
    frow_f = pl.BlockSpec(frow_shape, lambda bi, j: (bi, 0, j, 0, 0))
    frow_b = pl.BlockSpec(frow_shape, lambda bi, j: (bi, 1, nb - 1 - j, 0, 0))
    out_shape = jax.ShapeDtypeStruct((b, s, MLSTM_W), F32)
    return pl.pallas_call(
        _mlstm_kernel,
        grid=(b // REC_BATCH, nb),
        in_specs=[fwd(0), fwd(1), fwd(2), gcol_f, frow_f, bwd(0), bwd(1), bwd(2), gcol_b, frow_b,
                  const(bcol.shape), const(brow.shape), const(tbd.shape), const(tbd_t.shape),
                  const(e_r.shape), const(e_f.shape), const(e_h.shape)],
        out_specs=[fwd(0), bwd(0)],
        out_shape=[out_shape, out_shape],
        scratch_shapes=[pltpu.VMEM((REC_BATCH, 2, MLSTM_W, MLSTM_W), F32),
                        pltpu.VMEM((REC_BATCH, 2, 1, MLSTM_W), F32),
                        pltpu.VMEM((REC_BATCH, 2, 1, MLSTM_W), F32)],
        compiler_params=_params("parallel", "arbitrary"),
        name="mlstm",
    )(mls_proj, mls_proj, mls_proj, gate_proj, f_rows,
      mls_proj, mls_proj, mls_proj, gate_proj, f_rows, bcol, brow, tbd, tbd_t, e_r, e_f, e_h)


def _outproj_kernel(x_ref, attn_ref, gof_ref, gob_ref, gg_ref, mhf_ref, mhb_ref, mo_ref,
                    gng_ref, mng_ref, bd_ref, wo_ref, out_ref):
    bd = bd_ref[...]

    def head_norm(t, gain):
        return t * lax.rsqrt(_group_mean(t * t, bd) + EPS) * gain

    gg = gg_ref[0]
    gla = head_norm(gof_ref[0] + gob_ref[0], gng_ref[...]) * (gg * _sigmoid(gg))
    mls = head_norm(_sigmoid(mo_ref[0]) * (mhf_ref[0] + mhb_ref[0]), mng_ref[...])
    acc = jnp.dot(attn_ref[0], wo_ref[:ATTN_W, :], preferred_element_type=F32)
    acc += jnp.dot(gla.astype(BF16), wo_ref[ATTN_W:ATTN_W + GLA_W, :], preferred_element_type=F32)
    acc += jnp.dot(mls.astype(BF16), wo_ref[ATTN_W + GLA_W:, :], preferred_element_type=F32)
    out_ref[0] = x_ref[0] + acc


def _outproj(x, attn, gof, gob, gla_proj, mhf, mhb, mls_proj, gng, mng, bd, wo):
    b, s, d = x.shape
    row = lambda n, col=0: pl.BlockSpec((1, ROW_TILE, n), lambda bi, i: (bi, i, col))
    const = lambda shape: pl.BlockSpec(shape, lambda bi, i: (0,) * len(shape))
    return pl.pallas_call(
        _outproj_kernel,
        grid=(b, s // ROW_TILE),
        in_specs=[row(d), row(ATTN_W), row(GLA_W), row(GLA_W), row(GLA_W, 3),
                  row(MLSTM_W), row(MLSTM_W), row(MLSTM_W, 3),
                  const((1, GLA_W)), const((1, MLSTM_W)), const(bd.shape), const(wo.shape)],
        out_specs=row(d),
        out_shape=jax.ShapeDtypeStruct((b, s, d), F32),
        compiler_params=_params("parallel", "parallel"),
        name="outproj",
    )(x, attn, gof, gob, gla_proj, mhf, mhb, mls_proj, gng, mng, bd, wo)


def _ffn_kernel(x_ref, prev_ref, next_ref, g_ref, wg_ref, wv_ref, cwg_ref, cwv_ref, cbg_ref, cbv_ref,
                wd_ref, out_ref, h_ref, ug_ref, uv_ref, acc_ref):
    i, j = pl.program_id(1), pl.program_id(2)
    tm = x_ref.shape[1]

    def normed(t):
        ms = jnp.mean(t * t, axis=-1, keepdims=True)
        return t * lax.rsqrt(ms + EPS) * g_ref[...]

    @pl.when(j == 0)
    def _():
        keep_prev = (i > 0).astype(F32)
        keep_next = (i < pl.num_programs(1) - 1).astype(F32)
        h_ref[0:HALO, :] = (normed(prev_ref[0]) * keep_prev).astype(BF16)
        h_ref[HALO:HALO + tm, :] = normed(x_ref[0]).astype(BF16)
        h_ref[HALO + tm:, :] = (normed(next_ref[0]) * keep_next).astype(BF16)
        acc_ref[...] = jnp.zeros(acc_ref.shape, F32)

    h = h_ref[...]
    ug_ref[...] = jnp.dot(h, wg_ref[...], preferred_element_type=F32)
    uv_ref[...] = jnp.dot(h, wv_ref[...], preferred_element_type=F32)

    def conv(u_ref, cw_ref, cb_ref):
        out = cb_ref[...] + u_ref[HALO - 1:HALO - 1 + tm, :] * cw_ref[0:1, :]
        for t in range(1, CONV_WIDTH):
            out = out + u_ref[HALO - 1 + t:HALO - 1 + t + tm, :] * cw_ref[t:t + 1, :]
        return out

    gate = conv(ug_ref, cwg_ref, cbg_ref)
    val = conv(uv_ref, cwv_ref, cbv_ref)
    act = (gate * _sigmoid(gate) * val).astype(BF16)
    acc_ref[...] += jnp.dot(act, wd_ref[...], preferred_element_type=F32)

    @pl.when(j == pl.num_programs(2) - 1)
    def _():
        out_ref[0] = x_ref[0] + acc_ref[...]


def _ffn(x, g, w_up, conv_w, conv_b, w_down):
    b, s, d = x.shape
    nj = D_FF // FF_BLOCK
    tiles_per_halo = FFN_ROW_TILE // HALO
    last_halo = s // HALO - 1
    row = pl.BlockSpec((1, FFN_ROW_TILE, d), lambda bi, i, j: (bi, i, 0))
    prev = pl.BlockSpec((1, HALO, d), lambda bi, i, j: (bi, jnp.maximum(i * tiles_per_halo - 1, 0), 0))
    nxt = pl.BlockSpec((1, HALO, d),
                       lambda bi, i, j: (bi, jnp.minimum((i + 1) * tiles_per_halo, last_halo), 0))
    gate_cols = lambda rows: pl.BlockSpec((rows, FF_BLOCK), lambda bi, i, j: (0, j))
    val_cols = lambda rows: pl.BlockSpec((rows, FF_BLOCK), lambda bi, i, j: (0, nj + j))
    return pl.pallas_call(
        _ffn_kernel,
        grid=(b, s // FFN_ROW_TILE, nj),
        in_specs=[row, prev, nxt, pl.BlockSpec((1, d), lambda bi, i, j: (0, 0)),
                  gate_cols(d), val_cols(d), gate_cols(CONV_WIDTH), val_cols(CONV_WIDTH),
                  gate_cols(1), val_cols(1),
                  pl.BlockSpec((FF_BLOCK, d), lambda bi, i, j: (j, 0))],
        out_specs=row,
        out_shape=jax.ShapeDtypeStruct((b, s, d), F32),
        scratch_shapes=[pltpu.VMEM((FFN_ROW_TILE + 2 * HALO, d), BF16),
                        pltpu.VMEM((FFN_ROW_TILE + 2 * HALO, FF_BLOCK), F32),
                        pltpu.VMEM((FFN_ROW_TILE + 2 * HALO, FF_BLOCK), F32),
                        pltpu.VMEM((FFN_ROW_TILE, d), F32)],
        compiler_params=_params("parallel", "parallel", "arbitrary"),
        name="ffn",
    )(x, x, x, g, w_up, w_up, conv_w, conv_w, conv_b, conv_b, w_down)


def _rope_tables(seq):
    n_rows = seq // GRID_W
    row = jnp.repeat(jnp.arange(n_rows, dtype=F32), GRID_W)
    col = jnp.tile(jnp.arange(GRID_W, dtype=F32), n_rows)
    n_freq = HEAD_DIM // 4
    inv_freq = jnp.power(ROPE_THETA, -jnp.arange(n_freq, dtype=F32) / n_freq)
    ang_r, ang_c = row[:, None] * inv_freq, col[:, None] * inv_freq
    cos_h = jnp.concatenate([jnp.cos(ang_r), jnp.cos(ang_r), jnp.cos(ang_c), jnp.cos(ang_c)], axis=1)
    sin_h = jnp.concatenate([-jnp.sin(ang_r), jnp.sin(ang_r), -jnp.sin(ang_c), jnp.sin(ang_c)], axis=1)
    reps = QK_W // HEAD_DIM
    return jnp.tile(cos_h, (1, reps)), jnp.tile(sin_h, (1, reps))


def _block_diag_mean(width):
    group = np.arange(width) // HEAD_DIM
    return jnp.asarray((group[:, None] == group[None, :]).astype(np.float32) / HEAD_DIM, dtype=BF16)


def _tri_constants():
    t = np.arange(REC_BLOCK)
    same_chunk = (t[:, None] // CHUNK) == (t[None, :] // CHUNK)
    prefix = (same_chunk & (t[None, :] <= t[:, None])).astype(np.float32)
    suffix = (same_chunk & (t[None, :] >= t[:, None])).astype(np.float32)
    tbd = jnp.asarray(np.stack([prefix, suffix]), dtype=BF16)
    tbd_t = jnp.asarray(np.stack([prefix.T, suffix.T]), dtype=BF16)
    return tbd, tbd_t


def _mlstm_expanders():
    e_r = np.zeros((2, GATE_W, MLSTM_W), np.float32)
    e_f = np.zeros((2, GATE_W, MLSTM_W), np.float32)
    for d in range(2):
        for h in range(MLSTM_HEADS):
            lanes = slice(h * HEAD_DIM, (h + 1) * HEAD_DIM)
            i_lane = 2 * GLA_RANK + d * MLSTM_HEADS + h
            f_lane = i_lane + 2 * MLSTM_HEADS
            e_r[d, i_lane, lanes] = 1.0
            e_r[d, f_lane, lanes] = -1.0
            e_f[d, f_lane, lanes] = 1.0
    group = np.arange(MLSTM_W) // HEAD_DIM
    e_h = (group[:, None] == group[None, :]).astype(np.float32)
    return jnp.asarray(e_r, dtype=BF16), jnp.asarray(e_f, dtype=BF16), jnp.asarray(e_h, dtype=BF16)


def _layer(x, consts, norm_mix_g, w_in, attn_qn_g, attn_kn_g, gla_gate_w2, gla_gate_b, gla_norm_g,
           mlstm_gate_b, mlstm_norm_g, w_out, norm_ffn_g, w_up, conv_w, conv_b, w_down):
    cos_t, sin_t, bd_qk, bd_head, tbd, tbd_t, e_r, e_f, e_h = consts
    b, s, _ = x.shape
    n_gates = 4 * MLSTM_HEADS
    main_w = QKV_W + 4 * GLA_W
    gla_a = w_in[:, main_w:main_w + 2 * GLA_RANK]
    mls_main = w_in[:, main_w + 2 * GLA_RANK:main_w + 2 * GLA_RANK + 4 * MLSTM_W]
    mls_g = w_in[:, main_w + 2 * GLA_RANK + 4 * MLSTM_W:]
    pad = jnp.zeros((D_MODEL, GATE_W - 2 * GLA_RANK - n_gates), F32)
    w_in_r = jnp.concatenate([w_in[:, :main_w], mls_main, gla_a, mls_g, pad], axis=1).astype(BF16)

    qkv, gla_proj, mls_proj, gate_proj = _inproj(x, norm_mix_g[None, :], w_in_r)

    qk_gain = jnp.concatenate([jnp.tile(attn_qn_g, ATTN_HEADS), jnp.tile(attn_kn_g, ATTN_KV_HEADS)])[None, :]
    q_t, k, v_t = _qkprep(qkv, cos_t, sin_t, qk_gain, bd_qk)
    score_bound = (HEAD_DIM * Q_SCALE * LOG2_E) * jnp.max(jnp.abs(attn_qn_g)) * jnp.max(jnp.abs(attn_kn_g))
    bounded = (score_bound < ATTN_MAX_SCORE_BOUND).astype(jnp.int32).reshape(1)
    attn = _attention(bounded, q_t, k, v_t)

    w2ext = jnp.zeros((2, GATE_W, GLA_W), F32)
    w2ext = w2ext.at[0, :GLA_RANK].set(gla_gate_w2[0]).at[1, GLA_RANK:2 * GLA_RANK].set(gla_gate_w2[1])
    w2_hi = w2ext.astype(BF16)
    w2_lo = (w2ext - w2_hi.astype(F32)).astype(BF16)
    cum_f, cum_b = _glagate(gate_proj, w2_hi, w2_lo, gla_gate_b[:, None, :], tbd)
    gof, gob = _gla(gla_proj, cum_f, cum_b, e_h)

    nb, cpb = s // REC_BLOCK, REC_BLOCK // CHUNK
    f_lo = 2 * GLA_RANK + 2 * MLSTM_HEADS
    f_rows = gate_proj[:, :, f_lo:f_lo + 2 * MLSTM_HEADS].reshape(b, nb, cpb, CHUNK, 2, MLSTM_HEADS)
    f_rows = f_rows.transpose(0, 4, 1, 2, 5, 3).reshape(b, 2, nb, cpb, MLSTM_W)
    f_rows = jnp.pad(f_rows, ((0, 0), (0, 0), (0, 0), (0, SUBLANE - cpb), (0, 0)))
    bcol = jnp.zeros((1, GATE_W), F32).at[0, 2 * GLA_RANK:2 * GLA_RANK + n_gates].set(
        mlstm_gate_b.reshape(n_gates))
    brow = jnp.repeat(mlstm_gate_b[2:], HEAD_DIM, axis=1)[:, None, :]
    mhf, mhb = _mlstm(mls_proj, gate_proj, f_rows, bcol, brow, tbd, tbd_t, e_r, e_f, e_h)

    x = _outproj(x, attn, gof, gob, gla_proj, mhf, mhb, mls_proj,
                 jnp.tile(gla_norm_g, GLA_HEADS)[None, :], jnp.tile(mlstm_norm_g, MLSTM_HEADS)[None, :],
                 bd_head, w_out.astype(BF16))
    return _ffn(x, norm_ffn_g[None, :], w_up.astype(BF16), conv_w, conv_b[None, :], w_down.astype(BF16))


def _trunk(x, weights):
    seq = x.shape[1]
    cos_t, sin_t = _rope_tables(seq)
    consts = ((cos_t, sin_t, _block_diag_mean(QK_W), _block_diag_mean(GLA_W))
              + _tri_constants() + _mlstm_expanders())
    depth = weights[0].shape[0]
    for l in range(depth):
        x = _layer(x, consts, *[w[l] for w in weights])
    return x


def kernel(x_prompt, x_sample, norm_mix_g, w_in, attn_qn_g, attn_kn_g, gla_gate_w2, gla_gate_b,
           gla_norm_g, mlstm_gate_b, mlstm_norm_g, w_out, norm_ffn_g, w_up, conv_w, conv_b, w_down):
    weights = (norm_mix_g, w_in, attn_qn_g, attn_kn_g, gla_gate_w2, gla_gate_b, gla_norm_g,
               mlstm_gate_b, mlstm_norm_g, w_out, norm_ffn_g, w_up, conv_w, conv_b, w_down)
    return _trunk(x_prompt, weights), _trunk(x_sample, weights)
```

```python
import functools

import jax
import jax.numpy as jnp
import numpy as np
from jax import lax
from jax.experimental import pallas as pl
from jax.experimental.pallas import tpu as pltpu

F32 = jnp.float32
BF16 = jnp.bfloat16

D_MODEL = 1024
GRID_W = 64
HEAD_DIM = 64
EPS = 1e-6
ATTN_HEADS = 8
ATTN_KV_HEADS = 2
ATTN_GROUP = ATTN_HEADS // ATTN_KV_HEADS
ROPE_THETA = 10000.0
GLA_HEADS = 4
GLA_RANK = 16
GLA_TAU = 16.0
MLSTM_HEADS = 4
CHUNK = 64
GLA_MAX_EXPONENT = 40.0
ATTN_W = ATTN_HEADS * HEAD_DIM
KV_W = ATTN_KV_HEADS * HEAD_DIM
GLA_W = GLA_HEADS * HEAD_DIM
MLSTM_W = MLSTM_HEADS * HEAD_DIM
QK_W = ATTN_W + KV_W
QKV_W = ATTN_W + 2 * KV_W
D_FF = 2816
CONV_WIDTH = 3
GATE_W = 128
Q_SCALE = HEAD_DIM ** -0.5
LOG2_E = 1.4426950408889634

SUBLANE = 8
VMEM_LIMIT = 56 * 1024 * 1024

ROW_TILE = 1024
FFN_ROW_TILE = 512
REC_BLOCK = 256
REC_BATCH = 4
ATTN_TQ = 512
ATTN_STREAM_LANES = 256
ATTN_TK = 256
VT_ROWS = HEAD_DIM + 16
KEY_W = 128
ATTN_MAX_SCORE_BOUND = 32.0
FF_BLOCK = 1408
HALO = SUBLANE

NT_DIMS = (((1,), (1,)), ((), ()))
TN_DIMS = (((0,), (0,)), ((), ()))


def _params(*sem):
    return pltpu.CompilerParams(dimension_semantics=sem, vmem_limit_bytes=VMEM_LIMIT)


def _log_sigmoid(z):
    return jnp.minimum(z, 0.0) - jnp.log(1.0 + jnp.exp(-jnp.abs(z)))


def _sigmoid(z):
    return 1.0 / (1.0 + jnp.exp(-z))


def _dot(a, b):
    return jnp.dot(a.astype(BF16), b.astype(BF16), preferred_element_type=F32)


def _dot_dims(a, b, dims):
    return lax.dot_general(a.astype(BF16), b.astype(BF16), dims, preferred_element_type=F32)


def _split3(x):
    p1 = x.astype(BF16)
    rest = x - p1.astype(F32)
    p2 = rest.astype(BF16)
    return p1, p2, (rest - p2.astype(F32)).astype(BF16)


def _cumsum_cols(tbd, x):
    return sum(jnp.dot(tbd, p, preferred_element_type=F32) for p in _split3(x))


def _cumsum_rows(x, tbd_t):
    return sum(jnp.dot(p, tbd_t, preferred_element_type=F32) for p in _split3(x))


def _dot_split(a, b_hi, b_lo):
    a_hi = a.astype(BF16)
    a_lo = (a - a_hi.astype(F32)).astype(BF16)
    return (jnp.dot(a_hi, b_hi, preferred_element_type=F32)
            + jnp.dot(a_lo, b_hi, preferred_element_type=F32)
            + jnp.dot(a_hi, b_lo, preferred_element_type=F32))


def _group_mean(sq, bd):
    hi = sq.astype(BF16)
    lo = (sq - hi.astype(F32)).astype(BF16)
    return (jnp.dot(hi, bd, preferred_element_type=F32)
            + jnp.dot(lo, bd, preferred_element_type=F32))


def _inproj_kernel(x_ref, g_ref, w_ref, qkv_ref, gla_ref, mls_ref, gate_ref):
    x = x_ref[0]
    ms = jnp.mean(x * x, axis=-1, keepdims=True)
    h = (x * lax.rsqrt(ms + EPS) * g_ref[...]).astype(BF16)
    col = 0
    for ref in (qkv_ref, gla_ref, mls_ref, gate_ref):
        width = ref.shape[-1]
        ref[0] = jnp.dot(h, w_ref[:, col:col + width], preferred_element_type=F32)
        col += width


def _inproj(x, g, w):
    b, s, d = x.shape
    widths = (QKV_W, 4 * GLA_W, 4 * MLSTM_W, GATE_W)
    row = lambda n: pl.BlockSpec((1, ROW_TILE, n), lambda bi, i: (bi, i, 0))
    const = lambda shape: pl.BlockSpec(shape, lambda bi, i: (0,) * len(shape))
    return pl.pallas_call(
        _inproj_kernel,
        grid=(b, s // ROW_TILE),
        in_specs=[row(d), const((1, d)), const(w.shape)],
        out_specs=[row(n) for n in widths],
        out_shape=[jax.ShapeDtypeStruct((b, s, n), F32) for n in widths],
        compiler_params=_params("parallel", "parallel"),
        name="inproj",
    )(x, g, w)


def _qkprep_kernel(qkv_ref, cos_ref, sin_ref, gain_ref, bd_ref, qt_ref, k_ref, vt_ref):
    x = qkv_ref[0]
    qk = x[:, :QK_W]
    ms = _group_mean(qk * qk, bd_ref[...])
    y = qk * lax.rsqrt(ms + EPS) * gain_ref[...]
    lane = lax.broadcasted_iota(jnp.int32, y.shape, 1)
    first_half = (lane & 31) < 16
    quarter = HEAD_DIM // 4
    swapped = jnp.where(first_half, pltpu.roll(y, QK_W - quarter, 1), pltpu.roll(y, quarter, 1))
    r = y * cos_ref[...] + swapped * sin_ref[...]
    q_t = (r[:, :ATTN_W] * (Q_SCALE * LOG2_E)).T.astype(BF16)
    qt_ref[0] = q_t.reshape(ATTN_HEADS, HEAD_DIM, q_t.shape[-1])
    v_t = x[:, QK_W:].T.astype(BF16)
    v_t = v_t.reshape(ATTN_KV_HEADS, HEAD_DIM, v_t.shape[-1])
    row = lax.broadcasted_iota(jnp.int32, (ATTN_KV_HEADS, VT_ROWS - HEAD_DIM, v_t.shape[-1]), 1)
    vt_ref[0] = jnp.concatenate([v_t, jnp.where(row == 0, 1.0, 0.0).astype(BF16)], axis=1)
    ones_lane = lax.broadcasted_iota(jnp.int32, (x.shape[0], KEY_W - HEAD_DIM), 1) == 0
    k_pad = jnp.where(ones_lane, 1.0, 0.0).astype(BF16)
    for h in range(ATTN_KV_HEADS):
        k_h = r[:, ATTN_W + h * HEAD_DIM:ATTN_W + (h + 1) * HEAD_DIM].astype(BF16)
        k_ref[0, h] = jnp.concatenate([k_h, k_pad], axis=1)


def _qkprep(qkv, cos_t, sin_t, gain, bd):
    b, s, _ = qkv.shape
    const = lambda shape: pl.BlockSpec(shape, lambda bi, i: (0,) * len(shape))
    tab = pl.BlockSpec((ROW_TILE, QK_W), lambda bi, i: (i, 0))
    t_spec = lambda heads, rows: pl.BlockSpec((1, heads, rows, ROW_TILE), lambda bi, i: (bi, 0, 0, i))
    t_shape = lambda heads, rows: jax.ShapeDtypeStruct((b, heads, rows, s), BF16)
    return pl.pallas_call(
        _qkprep_kernel,
        grid=(b, s // ROW_TILE),
        in_specs=[pl.BlockSpec((1, ROW_TILE, QKV_W), lambda bi, i: (bi, i, 0)),
                  tab, tab, const((1, QK_W)), const((QK_W, QK_W))],
        out_specs=[t_spec(ATTN_HEADS, HEAD_DIM),
                   pl.BlockSpec((1, ATTN_KV_HEADS, ROW_TILE, KEY_W), lambda bi, i: (bi, 0, i, 0)),
                   t_spec(ATTN_KV_HEADS, VT_ROWS)],
        out_shape=[t_shape(ATTN_HEADS, HEAD_DIM),
                   jax.ShapeDtypeStruct((b, ATTN_KV_HEADS, s, KEY_W), BF16),
                   t_shape(ATTN_KV_HEADS, VT_ROWS)],
        compiler_params=_params("parallel", "parallel"),
        name="qkprep",
    )(qkv, cos_t, sin_t, gain, bd)


def _attn_kernel(bounded_ref, qt_ref, k_ref, vt_ref, o_ref):
    tq = qt_ref.shape[-1]
    n_chunks = k_ref.shape[2] // ATTN_TK
    keys = lambda c: slice(c * ATTN_TK, (c + 1) * ATTN_TK)
    pad_rows = KEY_W - HEAD_DIM
    lanes = ATTN_STREAM_LANES

    def finish(acc, head, lo):
        out = (acc[:HEAD_DIM] / acc[HEAD_DIM:HEAD_DIM + 1]).T
        o_ref[0, lo:lo + lanes, head * HEAD_DIM:(head + 1) * HEAD_DIM] = out.astype(o_ref.dtype)

    def scores(c, q_ext):
        return jnp.dot(k_ref[0, 0, keys(c), :], q_ext, preferred_element_type=F32)

    def weighted_values(c, p_t):
        return jnp.dot(vt_ref[0, 0, :, keys(c)], p_t, preferred_element_type=F32)

    def fixed_shift_stream(head, lo):
        q_t = qt_ref[0, head, :, lo:lo + lanes]
        s0 = scores(0, jnp.concatenate([q_t, jnp.zeros((pad_rows, lanes), BF16)], axis=0))
        shift = jnp.max(s0, axis=0, keepdims=True).astype(BF16).astype(F32)
        yield
        first_row = lax.broadcasted_iota(jnp.int32, (pad_rows, lanes), 0) == 0
        q_ext = jnp.concatenate([q_t, jnp.where(first_row, -shift, 0.0).astype(BF16)], axis=0)
        nxt = scores(1, q_ext)
        acc = weighted_values(0, jnp.exp2(s0 - shift).astype(BF16))
        yield
        for c in range(1, n_chunks):
            s_t = nxt
            if c + 1 < n_chunks:
                nxt = scores(c + 1, q_ext)
            p_t = jnp.exp2(s_t).astype(BF16)
            acc = acc + weighted_values(c, p_t)
            yield
        finish(acc, head, lo)

    def running_max_stream(head, lo):
        q_ext = jnp.concatenate([qt_ref[0, head, :, lo:lo + lanes], jnp.zeros((pad_rows, lanes), BF16)], axis=0)
        m = jnp.full((1, lanes), -jnp.inf, F32)
        acc = jnp.zeros((VT_ROWS, lanes), F32)
        nxt = scores(0, q_ext)
        yield
        for c in range(n_chunks):
            s_t = nxt
            if c + 1 < n_chunks:
                nxt = scores(c + 1, q_ext)
            m_new = jnp.maximum(m, jnp.max(s_t, axis=0, keepdims=True))
            p_t = jnp.exp2((s_t - m_new).astype(BF16))
            acc = jnp.exp2(m - m_new) * acc + weighted_values(c, p_t)
            m = m_new
            yield
        finish(acc, head, lo)

    def run(stream):
        _interleave(*[stream(j, lo) for j in range(ATTN_GROUP) for lo in range(0, tq, lanes)])

    pl.when(bounded_ref[0] == 1)(functools.partial(run, fixed_shift_stream))
    pl.when(bounded_ref[0] == 0)(functools.partial(run, running_max_stream))


def _attention(bounded, q_t, k, v_t):
    b, _, _, s = q_t.shape
    gw = ATTN_GROUP * HEAD_DIM
    return pl.pallas_call(
        _attn_kernel,
        grid=(b, ATTN_KV_HEADS, s // ATTN_TQ),
        in_specs=[pl.BlockSpec(memory_space=pltpu.SMEM),
                  pl.BlockSpec((1, ATTN_GROUP, HEAD_DIM, ATTN_TQ), lambda bi, h, i: (bi, h, 0, i)),
                  pl.BlockSpec((1, 1, s, KEY_W), lambda bi, h, i: (bi, h, 0, 0)),
                  pl.BlockSpec((1, 1, VT_ROWS, s), lambda bi, h, i: (bi, h, 0, 0))],
        out_specs=pl.BlockSpec((1, ATTN_TQ, gw), lambda bi, h, i: (bi, i, h)),
        out_shape=jax.ShapeDtypeStruct((b, s, ATTN_W), BF16),
        compiler_params=_params("parallel", "parallel", "arbitrary"),
        name="attention",
    )(bounded, q_t, k, v_t)


def _chunk_order(d):
    n_chunks = REC_BLOCK // CHUNK
    return range(n_chunks) if d == 0 else range(n_chunks - 1, -1, -1)


def _glagate_kernel(a_ref, w2hi_ref, w2lo_ref, gb_ref, tbd_ref, cf_ref, cb_ref):
    a = a_ref[0]
    for d, out_ref in enumerate((cf_ref, cb_ref)):
        z = _dot_split(a, w2hi_ref[d], w2lo_ref[d]) + gb_ref[d]
        g = _log_sigmoid(z) * (1.0 / GLA_TAU)
        for sb in range(a.shape[0] // REC_BLOCK):
            rows = slice(sb * REC_BLOCK, (sb + 1) * REC_BLOCK)
            out_ref[0, rows, :] = _cumsum_cols(tbd_ref[d], g[rows])


def _glagate(gate_proj, w2_hi, w2_lo, gb, tbd):
    b, s, _ = gate_proj.shape
    const = lambda shape: pl.BlockSpec(shape, lambda bi, i: (0,) * len(shape))
    out_spec = pl.BlockSpec((1, ROW_TILE, GLA_W), lambda bi, i: (bi, i, 0))
    out_shape = jax.ShapeDtypeStruct((b, s, GLA_W), F32)
    return pl.pallas_call(
        _glagate_kernel,
        grid=(b, s // ROW_TILE),
        in_specs=[pl.BlockSpec((1, ROW_TILE, GATE_W), lambda bi, i: (bi, i, 0)),
                  const(w2_hi.shape), const(w2_lo.shape), const(gb.shape), const(tbd.shape)],
        out_specs=[out_spec, out_spec],
        out_shape=[out_shape, out_shape],
        compiler_params=_params("parallel", "parallel"),
        name="glagate",
    )(gate_proj, w2_hi, w2_lo, gb, tbd)


def _gla_exact_scores(q, cum, k_ref, cum_ref, n, base, d, eye_t, e_h, att_ref):
    sub = lax.broadcasted_iota(jnp.int32, q.shape, 0)

    def one_key(s, carry):
        k_s = k_ref[n, pl.ds(base + s, 1), :]
        c_s = cum_ref[n, pl.ds(base + s, 1), :]
        visible = (sub >= s) if d == 0 else (sub <= s)
        w = q * k_s * jnp.exp(jnp.where(visible, cum - c_s, -jnp.inf))
        per_head = _expand3(w, e_h)
        att_ref[pl.ds(s, 1), :] = jnp.sum(jnp.where(eye_t, per_head, 0.0), axis=0, keepdims=True)
        return carry

    lax.fori_loop(0, CHUNK, one_key, 0)
    return att_ref[...]


def _gla_exponent_range(cum_ref, n):
    worst = jnp.zeros((1, GLA_W), F32)
    for i in range(REC_BLOCK // CHUNK):
        ref = cum_ref[n, i * CHUNK + CHUNK // 2:i * CHUNK + CHUNK // 2 + 1, :]
        for end in (i * CHUNK, (i + 1) * CHUNK - 1):
            worst = jnp.maximum(worst, jnp.abs(cum_ref[n, end:end + 1, :] - ref))
    return worst


def _gla_direction(q_ref, k_ref, v_ref, cum_ref, o_ref, st_ref, d, n, exact_ref=None):
    causal_t, eye_t, head_lane, block_diag = _head_layout_masks(d)
    last = CHUNK - 1 if d == 0 else 0
    st = st_ref[n, d]
    for i in _chunk_order(d):
        rows = slice(i * CHUNK, (i + 1) * CHUNK)
        cum = cum_ref[n, rows, :]
        total = cum[last:last + 1]
        q = q_ref[n, rows, :] * Q_SCALE
        k = k_ref[n, rows, :]
        v = v_ref[n, rows, :]
        qe = q * jnp.exp(cum)
        kd = k * jnp.exp(total - cum)
        if exact_ref is None:
            ref = cum[CHUNK // 2:CHUNK // 2 + 1]
            qt = q * jnp.exp(cum - ref)
            kt = k * jnp.exp(ref - cum)
            q4 = jnp.concatenate([jnp.where(head_lane[h], qt, 0.0) for h in range(GLA_HEADS)], axis=0)
            att_t = jnp.where(causal_t, _dot_dims(kt, q4, NT_DIMS), 0.0)
        else:
            att_t = _gla_exact_scores(q, cum, k_ref, cum_ref, n, i * CHUNK, d, eye_t, *exact_ref)
        yield
        full = _dot_dims(att_t, v, TN_DIMS)
        out = _dot_dims(qe, st, NT_DIMS)
        yield
        for h in range(GLA_HEADS):
            out = out + jnp.where(head_lane[h], full[h * CHUNK:(h + 1) * CHUNK], 0.0)
        o_ref[n, rows, :] = out
        st = st * jnp.exp(total) + jnp.where(block_diag, _dot_dims(v, kd, TN_DIMS), 0.0)
        yield
    st_ref[n, d] = st


def _interleave(*stages):
    stages = list(stages)
    while stages:
        for g in list(stages):
            if next(g, StopIteration) is StopIteration:
                stages.remove(g)


def _gla_kernel(qf_ref, kf_ref, vf_ref, cf_ref, qb_ref, kb_ref, vb_ref, cb_ref, eh_ref,
                of_ref, ob_ref, st_ref, att_ref):
    @pl.when(pl.program_id(1) == 0)
    def _():
        st_ref[...] = jnp.zeros(st_ref.shape, F32)

    n_seq = qf_ref.shape[0]
    worst = jnp.zeros((1, GLA_W), F32)
    for n in range(n_seq):
        worst = jnp.maximum(worst, jnp.maximum(_gla_exponent_range(cf_ref, n), _gla_exponent_range(cb_ref, n)))
    factorisable = jnp.max(worst) < GLA_MAX_EXPONENT

    def run(exact_ref):
        stages = []
        for n in range(n_seq):
            stages.append(_gla_direction(qf_ref, kf_ref, vf_ref, cf_ref, of_ref, st_ref, 0, n, exact_ref))
            stages.append(_gla_direction(qb_ref, kb_ref, vb_ref, cb_ref, ob_ref, st_ref, 1, n, exact_ref))
        if exact_ref is None:
            _interleave(*stages)
        else:
            for stage in stages:
                _interleave(stage)

    pl.when(factorisable)(functools.partial(run, None))
    pl.when(jnp.logical_not(factorisable))(functools.partial(run, (eh_ref[...], att_ref)))


def _gla(gla_proj, cum_f, cum_b, e_h):
    b, s, _ = gla_proj.shape
    nb = s // REC_BLOCK
    fwd = lambda col: pl.BlockSpec((REC_BATCH, REC_BLOCK, GLA_W), lambda bi, j: (bi, j, col))
    bwd = lambda col: pl.BlockSpec((REC_BATCH, REC_BLOCK, GLA_W), lambda bi, j: (bi, nb - 1 - j, col))
    out_shape = jax.ShapeDtypeStruct((b, s, GLA_W), F32)
    return pl.pallas_call(
        _gla_kernel,
        grid=(b // REC_BATCH, nb),
        in_specs=[fwd(0), fwd(1), fwd(2), fwd(0), bwd(0), bwd(1), bwd(2), bwd(0),
                  pl.BlockSpec(e_h.shape, lambda bi, j: (0, 0))],
        out_specs=[fwd(0), bwd(0)],
        out_shape=[out_shape, out_shape],
        scratch_shapes=[pltpu.VMEM((REC_BATCH, 2, GLA_W, GLA_W), F32),
                        pltpu.VMEM((CHUNK, GLA_W), F32)],
        compiler_params=_params("parallel", "arbitrary"),
        name="gla",
    )(gla_proj, gla_proj, gla_proj, cum_f, gla_proj, gla_proj, gla_proj, cum_b, e_h)


def _head_layout_masks(d):
    sub = lax.broadcasted_iota(jnp.int32, (CHUNK, MLSTM_W), 0)
    lane = lax.broadcasted_iota(jnp.int32, (CHUNK, MLSTM_W), 1)
    t_lane = lane & (HEAD_DIM - 1)
    causal_t = (sub <= t_lane) if d == 0 else (sub >= t_lane)
    eye_t = sub == t_lane
    head_lane = [(lane >> 6) == h for h in range(MLSTM_HEADS)]
    brow = lax.broadcasted_iota(jnp.int32, (MLSTM_W, MLSTM_W), 0) >> 6
    bcol = lax.broadcasted_iota(jnp.int32, (MLSTM_W, MLSTM_W), 1) >> 6
    return causal_t, eye_t, head_lane, brow == bcol


def _expand3(x, e):
    return sum(jnp.dot(p, e, preferred_element_type=F32) for p in _split3(x))


def _mlstm_direction(q_ref, k_ref, v_ref, gates, frow, h_ref, tbd, tbd_t, e_r, e_f, e_h,
                     ct_ref, n_ref, m_ref, d, n):
    causal_t, eye_t, head_lane, block_diag = _head_layout_masks(d)
    last = CHUNK - 1 if d == 0 else 0
    n_chunks = REC_BLOCK // CHUNK
    lane = lax.broadcasted_iota(jnp.int32, gates.shape, 1)
    i_lo = 2 * GLA_RANK + d * MLSTM_HEADS
    is_i_lane = (lane >= i_lo) & (lane < i_lo + MLSTM_HEADS)
    fcum = _cumsum_cols(tbd, _log_sigmoid(gates))
    r_exp = _expand3(jnp.where(is_i_lane, gates, fcum), e_r)
    f_last = jnp.concatenate([fcum[c * CHUNK + last:c * CHUNK + last + 1] for c in range(n_chunks)]
                             + [jnp.zeros((SUBLANE - n_chunks, fcum.shape[1]), F32)], axis=0)
    f_tot_rows = _expand3(f_last, e_f)
    fc_rows = _cumsum_rows(_log_sigmoid(frow), tbd_t)

    ct, n_row, m_row = ct_ref[n, d], n_ref[n, d], m_ref[n, d]
    for c in _chunk_order(d):
        rows = slice(c * CHUNK, (c + 1) * CHUNK)
        q, ks, v = q_ref[n, rows, :], k_ref[n, rows, :] * Q_SCALE, v_ref[n, rows, :]
        q4 = jnp.concatenate([jnp.where(head_lane[h], q, 0.0) for h in range(MLSTM_HEADS)], axis=0)
        kn = jnp.concatenate([ks, jnp.broadcast_to(n_row, (2 * SUBLANE, MLSTM_W))], axis=0)
        z = _dot_dims(kn, q4, NT_DIMS)
        s_t, qn = z[:CHUNK], z[CHUNK:CHUNK + 1]
        yield
        rc = r_exp[rows]
        rmax_t = jnp.max(jnp.where(causal_t, rc, -jnp.inf), axis=0, keepdims=True)
        rmax = jnp.max(rc, axis=0, keepdims=True)
        p_t = s_t * jnp.exp(jnp.where(causal_t, rc - rmax_t, -jnp.inf))
        row_sum = jnp.sum(p_t, axis=0, keepdims=True)
        kw = ks * jnp.exp(rc - rmax)
        fc_row = fc_rows[c:c + 1]
        inter = fc_row + m_row
        intra = fc_row + rmax_t
        m_t = jnp.maximum(inter, intra)
        w_inter, w_intra = jnp.exp(inter - m_t), jnp.exp(intra - m_t)
        den = w_inter * qn + w_intra * row_sum
        inv = 1.0 / jnp.maximum(jnp.abs(den), jnp.exp(-m_t))
        full = _dot_dims(p_t * (w_intra * inv), v, TN_DIMS)
        yield
        out = _expand3(jnp.where(eye_t, w_inter * inv, 0.0), e_h) * _dot(q, ct)
        for h in range(MLSTM_HEADS):
            out = out + jnp.where(head_lane[h], full[h * CHUNK:(h + 1) * CHUNK], 0.0)
        h_ref[n, rows, :] = out
        m_max = jnp.maximum(m_row, rmax)
        keep, add = jnp.exp(m_row - m_max), jnp.exp(rmax - m_max)
        ct = keep * ct + add * jnp.where(block_diag, _dot_dims(kw, v, TN_DIMS), 0.0)
        n_row = keep * n_row + add * jnp.sum(kw, axis=0, keepdims=True)
        m_row = f_tot_rows[c:c + 1] + m_max
        yield
    ct_ref[n, d], n_ref[n, d], m_ref[n, d] = ct, n_row, m_row


def _mlstm_kernel(qf_ref, kf_ref, vf_ref, gcf_ref, grf_ref, qb_ref, kb_ref, vb_ref, gcb_ref, grb_ref,
                  bcol_ref, brow_ref, tbd_ref, tbdt_ref, er_ref, ef_ref, eh_ref,
                  hf_ref, hb_ref, ct_ref, n_ref, m_ref):
    @pl.when(pl.program_id(1) == 0)
    def _():
        ct_ref[...] = jnp.zeros(ct_ref.shape, F32)
        n_ref[...] = jnp.zeros(n_ref.shape, F32)
        m_ref[...] = jnp.zeros(m_ref.shape, F32)

    stages = []
    for n in range(qf_ref.shape[0]):
        stages.append(_mlstm_direction(
            qf_ref, kf_ref, vf_ref, gcf_ref[n] + bcol_ref[...], grf_ref[n, 0, 0] + brow_ref[0],
            hf_ref, tbd_ref[0], tbdt_ref[0], er_ref[0], ef_ref[0], eh_ref[...], ct_ref, n_ref, m_ref, 0, n))
        stages.append(_mlstm_direction(
            qb_ref, kb_ref, vb_ref, gcb_ref[n] + bcol_ref[...], grb_ref[n, 0, 0] + brow_ref[1],
            hb_ref, tbd_ref[1], tbdt_ref[1], er_ref[1], ef_ref[1], eh_ref[...], ct_ref, n_ref, m_ref, 1, n))
    _interleave(*stages)


def _mlstm(mls_proj, gate_proj, f_rows, bcol, brow, tbd, tbd_t, e_r, e_f, e_h):
    b, s, _ = mls_proj.shape
    nb = s // REC_BLOCK
    fwd = lambda col: pl.BlockSpec((REC_BATCH, REC_BLOCK, MLSTM_W), lambda bi, j: (bi, j, col))
    bwd = lambda col: pl.BlockSpec((REC_BATCH, REC_BLOCK, MLSTM_W), lambda bi, j: (bi, nb - 1 - j, col))
    const = lambda shape: pl.BlockSpec(shape, lambda bi, j: (0,) * len(shape))
    gcol_f = pl.BlockSpec((REC_BATCH, REC_BLOCK, GATE_W), lambda bi, j: (bi, j, 0))
    gcol_b = pl.BlockSpec((REC_BATCH, REC_BLOCK, GATE_W), lambda bi, j: (bi, nb - 1 - j, 0))
    frow_shape = (REC_BATCH, 1, 1) + f_rows.shape[3:]
    frow_f = pl.BlockSpec(frow_shape, lambda bi, j: (bi, 0, j, 0, 0))
    frow_b = pl.BlockSpec(frow_shape, lambda bi, j: (bi, 1, nb - 1 - j, 0, 0))
    out_shape = jax.ShapeDtypeStruct((b, s, MLSTM_W), F32)
    return pl.pallas_call(
        _mlstm_kernel,
        grid=(b // REC_BATCH, nb),
        in_specs=[fwd(0), fwd(1), fwd(2), gcol_f, frow_f, bwd(0), bwd(1), bwd(2), gcol_b, frow_b,
                  const(bcol.shape), const(brow.shape), const(tbd.shape), const(tbd_t.shape),
                  const(e_r.shape), const(e_f.shape), const(e_h.shape)],
        out_specs=[fwd(0), bwd(0)],
        out_shape=[out_shape, out_shape],
        scratch_shapes=[pltpu.VMEM((REC_BATCH, 2, MLSTM_W, MLSTM_W), F32),
                        pltpu.VMEM((REC_BATCH, 2, 1, MLSTM_W), F32),
                        pltpu.VMEM((REC_BATCH, 2, 1, MLSTM_W), F32)],
        compiler_params=_params("parallel", "arbitrary"),
        name="mlstm",
    )(mls_proj, mls_proj, mls_proj, gate_proj, f_rows,
      mls_proj, mls_proj, mls_proj, gate_proj, f_rows, bcol, brow, tbd, tbd_t, e_r, e_f, e_h)


def _outproj_kernel(x_ref, attn_ref, gof_ref, gob_ref, gg_ref, mhf_ref, mhb_ref, mo_ref,
                    gng_ref, mng_ref, bd_ref, wo_ref, out_ref):
    bd = bd_ref[...]

    def head_norm(t, gain):
        return t * lax.rsqrt(_group_mean(t * t, bd) + EPS) * gain

    gg = gg_ref[0]
    gla = head_norm(gof_ref[0] + gob_ref[0], gng_ref[...]) * (gg * _sigmoid(gg))
    mls = head_norm(_sigmoid(mo_ref[0]) * (mhf_ref[0] + mhb_ref[0]), mng_ref[...])
    acc = jnp.dot(attn_ref[0], wo_ref[:ATTN_W, :], preferred_element_type=F32)
    acc += jnp.dot(gla.astype(BF16), wo_ref[ATTN_W:ATTN_W + GLA_W, :], preferred_element_type=F32)
    acc += jnp.dot(mls.astype(BF16), wo_ref[ATTN_W + GLA_W:, :], preferred_element_type=F32)
    out_ref[0] = x_ref[0] + acc


def _outproj(x, attn, gof, gob, gla_proj, mhf, mhb, mls_proj, gng, mng, bd, wo):
    b, s, d = x.shape
    row = lambda n, col=0: pl.BlockSpec((1, ROW_TILE, n), lambda bi, i: (bi, i, col))
    const = lambda shape: pl.BlockSpec(shape, lambda bi, i: (0,) * len(shape))
    return pl.pallas_call(
        _outproj_kernel,
        grid=(b, s // ROW_TILE),
        in_specs=[row(d), row(ATTN_W), row(GLA_W), row(GLA_W), row(GLA_W, 3),
                  row(MLSTM_W), row(MLSTM_W), row(MLSTM_W, 3),
                  const((1, GLA_W)), const((1, MLSTM_W)), const(bd.shape), const(wo.shape)],
        out_specs=row(d),
        out_shape=jax.ShapeDtypeStruct((b, s, d), F32),
        compiler_params=_params("parallel", "parallel"),
        name="outproj",
    )(x, attn, gof, gob, gla_proj, mhf, mhb, mls_proj, gng, mng, bd, wo)


def _ffn_kernel(x_ref, prev_ref, next_ref, g_ref, wg_ref, wv_ref, cwg_ref, cwv_ref, cbg_ref, cbv_ref,
                wd_ref, out_ref, h_ref, ug_ref, uv_ref, acc_ref):
    i, j = pl.program_id(1), pl.program_id(2)
    tm = x_ref.shape[1]

    def normed(t):
        ms = jnp.mean(t * t, axis=-1, keepdims=True)
        return t * lax.rsqrt(ms + EPS) * g_ref[...]

    @pl.when(j == 0)
    def _():
        keep_prev = (i > 0).astype(F32)
        keep_next = (i < pl.num_programs(1) - 1).astype(F32)
        h_ref[0:HALO, :] = (normed(prev_ref[0]) * keep_prev).astype(BF16)
        h_ref[HALO:HALO + tm, :] = normed(x_ref[0]).astype(BF16)
        h_ref[HALO + tm:, :] = (normed(next_ref[0]) * keep_next).astype(BF16)
        acc_ref[...] = jnp.zeros(acc_ref.shape, F32)

    h = h_ref[...]
    ug_ref[...] = jnp.dot(h, wg_ref[...], preferred_element_type=F32)
    uv_ref[...] = jnp.dot(h, wv_ref[...], preferred_element_type=F32)

    def conv(u_ref, cw_ref, cb_ref):
        out = cb_ref[...] + u_ref[HALO - 1:HALO - 1 + tm, :] * cw_ref[0:1, :]
        for t in range(1, CONV_WIDTH):
            out = out + u_ref[HALO - 1 + t:HALO - 1 + t + tm, :] * cw_ref[t:t + 1, :]
        return out

    gate = conv(ug_ref, cwg_ref, cbg_ref)
    val = conv(uv_ref, cwv_ref, cbv_ref)
    act = (gate * _sigmoid(gate) * val).astype(BF16)
    acc_ref[...] += jnp.dot(act, wd_ref[...], preferred_element_type=F32)

    @pl.when(j == pl.num_programs(2) - 1)
    def _():
        out_ref[0] = x_ref[0] + acc_ref[...]


def _ffn(x, g, w_up, conv_w, conv_b, w_down):
    b, s, d = x.shape
    nj = D_FF // FF_BLOCK
    tiles_per_halo = FFN_ROW_TILE // HALO
    last_halo = s // HALO - 1
    row = pl.BlockSpec((1, FFN_ROW_TILE, d), lambda bi, i, j: (bi, i, 0))
    prev = pl.BlockSpec((1, HALO, d), lambda bi, i, j: (bi, jnp.maximum(i * tiles_per_halo - 1, 0), 0))
    nxt = pl.BlockSpec((1, HALO, d),
                       lambda bi, i, j: (bi, jnp.minimum((i + 1) * tiles_per_halo, last_halo), 0))
    gate_cols = lambda rows: pl.BlockSpec((rows, FF_BLOCK), lambda bi, i, j: (0, j))
    val_cols = lambda rows: pl.BlockSpec((rows, FF_BLOCK), lambda bi, i, j: (0, nj + j))
    return pl.pallas_call(
        _ffn_kernel,
        grid=(b, s // FFN_ROW_TILE, nj),
        in_specs=[row, prev, nxt, pl.BlockSpec((1, d), lambda bi, i, j: (0, 0)),
                  gate_cols(d), val_cols(d), gate_cols(CONV_WIDTH), val_cols(CONV_WIDTH),
                  gate_cols(1), val_cols(1),
                  pl.BlockSpec((FF_BLOCK, d), lambda bi, i, j: (j, 0))],
        out_specs=row,
        out_shape=jax.ShapeDtypeStruct((b, s, d), F32),
        scratch_shapes=[pltpu.VMEM((FFN_ROW_TILE + 2 * HALO, d), BF16),
                        pltpu.VMEM((FFN_ROW_TILE + 2 * HALO, FF_BLOCK), F32),
                        pltpu.VMEM((FFN_ROW_TILE + 2 * HALO, FF_BLOCK), F32),
                        pltpu.VMEM((FFN_ROW_TILE, d), F32)],
        compiler_params=_params("parallel", "parallel", "arbitrary"),
        name="ffn",
    )(x, x, x, g, w_up, w_up, conv_w, conv_w, conv_b, conv_b, w_down)


def _rope_tables(seq):
    n_rows = seq // GRID_W
    row = jnp.repeat(jnp.arange(n_rows, dtype=F32), GRID_W)
    col = jnp.tile(jnp.arange(GRID_W, dtype=F32), n_rows)
    n_freq = HEAD_DIM // 4
    inv_freq = jnp.power(ROPE_THETA, -jnp.arange(n_freq, dtype=F32) / n_freq)
    ang_r, ang_c = row[:, None] * inv_freq, col[:, None] * inv_freq
    cos_h = jnp.concatenate([jnp.cos(ang_r), jnp.cos(ang_r), jnp.cos(ang_c), jnp.cos(ang_c)], axis=1)
    sin_h = jnp.concatenate([-jnp.sin(ang_r), jnp.sin(ang_r), -jnp.sin(ang_c), jnp.sin(ang_c)], axis=1)
    reps = QK_W // HEAD_DIM
    return jnp.tile(cos_h, (1, reps)), jnp.tile(sin_h, (1, reps))


def _block_diag_mean(width):
    group = np.arange(width) // HEAD_DIM
    return jnp.asarray((group[:, None] == group[None, :]).astype(np.float32) / HEAD_DIM, dtype=BF16)


def _tri_constants():
    t = np.arange(REC_BLOCK)
    same_chunk = (t[:, None] // CHUNK) == (t[None, :] // CHUNK)
    prefix = (same_chunk & (t[None, :] <= t[:, None])).astype(np.float32)
    suffix = (same_chunk & (t[None, :] >= t[:, None])).astype(np.float32)
    tbd = jnp.asarray(np.stack([prefix, suffix]), dtype=BF16)
    tbd_t = jnp.asarray(np.stack([prefix.T, suffix.T]), dtype=BF16)
    return tbd, tbd_t


def _mlstm_expanders():
    e_r = np.zeros((2, GATE_W, MLSTM_W), np.float32)
    e_f = np.zeros((2, GATE_W, MLSTM_W), np.float32)
    for d in range(2):
        for h in range(MLSTM_HEADS):
            lanes = slice(h * HEAD_DIM, (h + 1) * HEAD_DIM)
            i_lane = 2 * GLA_RANK + d * MLSTM_HEADS + h
            f_lane = i_lane + 2 * MLSTM_HEADS
            e_r[d, i_lane, lanes] = 1.0
            e_r[d, f_lane, lanes] = -1.0
            e_f[d, f_lane, lanes] = 1.0
    group = np.arange(MLSTM_W) // HEAD_DIM
    e_h = (group[:, None] == group[None, :]).astype(np.float32)
    return jnp.asarray(e_r, dtype=BF16), jnp.asarray(e_f, dtype=BF16), jnp.asarray(e_h, dtype=BF16)


def _layer(x, consts, norm_mix_g, w_in, attn_qn_g, attn_kn_g, gla_gate_w2, gla_gate_b, gla_norm_g,
           mlstm_gate_b, mlstm_norm_g, w_out, norm_ffn_g, w_up, conv_w, conv_b, w_down):
    cos_t, sin_t, bd_qk, bd_head, tbd, tbd_t, e_r, e_f, e_h = consts
    b, s, _ = x.shape
    n_gates = 4 * MLSTM_HEADS
    main_w = QKV_W + 4 * GLA_W
    gla_a = w_in[:, main_w:main_w + 2 * GLA_RANK]
    mls_main = w_in[:, main_w + 2 * GLA_RANK:main_w + 2 * GLA_RANK + 4 * MLSTM_W]
    mls_g = w_in[:, main_w + 2 * GLA_RANK + 4 * MLSTM_W:]
    pad = jnp.zeros((D_MODEL, GATE_W - 2 * GLA_RANK - n_gates), F32)
    w_in_r = jnp.concatenate([w_in[:, :main_w], mls_main, gla_a, mls_g, pad], axis=1).astype(BF16)

    qkv, gla_proj, mls_proj, gate_proj = _inproj(x, norm_mix_g[None, :], w_in_r)

    qk_gain = jnp.concatenate([jnp.tile(attn_qn_g, ATTN_HEADS), jnp.tile(attn_kn_g, ATTN_KV_HEADS)])[None, :]
    q_t, k, v_t = _qkprep(qkv, cos_t, sin_t, qk_gain, bd_qk)
    score_bound = (HEAD_DIM * Q_SCALE * LOG2_E) * jnp.max(jnp.abs(attn_qn_g)) * jnp.max(jnp.abs(attn_kn_g))
    bounded = (score_bound < ATTN_MAX_SCORE_BOUND).astype(jnp.int32).reshape(1)
    attn = _attention(bounded, q_t, k, v_t)

    w2ext = jnp.zeros((2, GATE_W, GLA_W), F32)
    w2ext = w2ext.at[0, :GLA_RANK].set(gla_gate_w2[0]).at[1, GLA_RANK:2 * GLA_RANK].set(gla_gate_w2[1])
    w2_hi = w2ext.astype(BF16)
    w2_lo = (w2ext - w2_hi.astype(F32)).astype(BF16)
    cum_f, cum_b = _glagate(gate_proj, w2_hi, w2_lo, gla_gate_b[:, None, :], tbd)
    gof, gob = _gla(gla_proj, cum_f, cum_b, e_h)

    nb, cpb = s // REC_BLOCK, REC_BLOCK // CHUNK
    f_lo = 2 * GLA_RANK + 2 * MLSTM_HEADS
    f_rows = gate_proj[:, :, f_lo:f_lo + 2 * MLSTM_HEADS].reshape(b, nb, cpb, CHUNK, 2, MLSTM_HEADS)
    f_rows = f_rows.transpose(0, 4, 1, 2, 5, 3).reshape(b, 2, nb, cpb, MLSTM_W)
    f_rows = jnp.pad(f_rows, ((0, 0), (0, 0), (0, 0), (0, SUBLANE - cpb), (0, 0)))
    bcol = jnp.zeros((1, GATE_W), F32).at[0, 2 * GLA_RANK:2 * GLA_RANK + n_gates].set(
        mlstm_gate_b.reshape(n_gates))
    brow = jnp.repeat(mlstm_gate_b[2:], HEAD_DIM, axis=1)[:, None, :]
    mhf, mhb = _mlstm(mls_proj, gate_proj, f_rows, bcol, brow, tbd, tbd_t, e_r, e_f, e_h)

    x = _outproj(x, attn, gof, gob, gla_proj, mhf, mhb, mls_proj,
                 jnp.tile(gla_norm_g, GLA_HEADS)[None, :], jnp.tile(mlstm_norm_g, MLSTM_HEADS)[None, :],
                 bd_head, w_out.astype(BF16))
    return _ffn(x, norm_ffn_g[None, :], w_up.astype(BF16), conv_w, conv_b[None, :], w_down.astype(BF16))


def _trunk(x, weights):
    seq = x.shape[1]
    cos_t, sin_t = _rope_tables(seq)
    consts = ((cos_t, sin_t, _block_diag_mean(QK_W), _block_diag_mean(GLA_W))
              + _tri_constants() + _mlstm_expanders())
    depth = weights[0].shape[0]
    for l in range(depth):
        x = _layer(x, consts, *[w[l] for w in weights])
    return x


def kernel(x_prompt, x_sample, norm_mix_g, w_in, attn_qn_g, attn_kn_g, gla_gate_w2, gla_gate_b,
           gla_norm_g, mlstm_gate_b, mlstm_norm_g, w_out, norm_ffn_g, w_up, conv_w, conv_b, w_down):
    weights = (norm_mix_g, w_in, attn_qn_g, attn_kn_g, gla_gate_w2, gla_gate_b, gla_norm_g,
               mlstm_gate_b, mlstm_norm_g, w_out, norm_ffn_g, w_up, conv_w, conv_b, w_down)
    return _trunk(x_prompt, weights), _trunk(x_sample, weights)
```

```python
import functools

import jax
import jax.numpy as jnp
import numpy as np
from jax import lax
from jax.experimental import pallas as pl
from jax.experimental.pallas import tpu as pltpu

F32 = jnp.float32
BF16 = jnp.bfloat16

D_MODEL = 1024
GRID_W = 64
HEAD_DIM = 64
EPS = 1e-6
ATTN_HEADS = 8
ATTN_KV_HEADS = 2
ATTN_GROUP = ATTN_HEADS // ATTN_KV_HEADS
ROPE_THETA = 10000.0
GLA_HEADS = 4
GLA_RANK = 16
GLA_TAU = 16.0
MLSTM_HEADS = 4
CHUNK = 64
GLA_MAX_EXPONENT = 40.0
ATTN_W = ATTN_HEADS * HEAD_DIM
KV_W = ATTN_KV_HEADS * HEAD_DIM
GLA_W = GLA_HEADS * HEAD_DIM
MLSTM_W = MLSTM_HEADS * HEAD_DIM
QK_W = ATTN_W + KV_W
QKV_W = ATTN_W + 2 * KV_W
D_FF = 2816
CONV_WIDTH = 3
GATE_W = 128
Q_SCALE = HEAD_DIM ** -0.5
LOG2_E = 1.4426950408889634

SUBLANE = 8
VMEM_LIMIT = 56 * 1024 * 1024

ROW_TILE = 1024
FFN_ROW_TILE = 512
REC_BLOCK = 256
REC_BATCH = 4
ATTN_TQ = 1024
ATTN_STREAM_LANES = 256
ATTN_TK = 256
VT_ROWS = HEAD_DIM + 16
KEY_W = 128
ATTN_MAX_SCORE_BOUND = 32.0
FF_BLOCK = 1408
HALO = SUBLANE

NT_DIMS = (((1,), (1,)), ((), ()))
TN_DIMS = (((0,), (0,)), ((), ()))


def _params(*sem):
    return pltpu.CompilerParams(dimension_semantics=sem, vmem_limit_bytes=VMEM_LIMIT)


def _log_sigmoid(z):
    return jnp.minimum(z, 0.0) - jnp.log(1.0 + jnp.exp(-jnp.abs(z)))


def _sigmoid(z):
    return 1.0 / (1.0 + jnp.exp(-z))


def _dot(a, b):
    return jnp.dot(a.astype(BF16), b.astype(BF16), preferred_element_type=F32)


def _dot_dims(a, b, dims):
    return lax.dot_general(a.astype(BF16), b.astype(BF16), dims, preferred_element_type=F32)


def _split3(x):
    p1 = x.astype(BF16)
    rest = x - p1.astype(F32)
    p2 = rest.astype(BF16)
    return p1, p2, (rest - p2.astype(F32)).astype(BF16)


def _cumsum_cols(tbd, x):
    return sum(jnp.dot(tbd, p, preferred_element_type=F32) for p in _split3(x))


def _cumsum_rows(x, tbd_t):
    return sum(jnp.dot(p, tbd_t, preferred_element_type=F32) for p in _split3(x))


def _dot_split(a, b_hi, b_lo):
    a_hi = a.astype(BF16)
    a_lo = (a - a_hi.astype(F32)).astype(BF16)
    return (jnp.dot(a_hi, b_hi, preferred_element_type=F32)
            + jnp.dot(a_lo, b_hi, preferred_element_type=F32)
            + jnp.dot(a_hi, b_lo, preferred_element_type=F32))


def _group_mean(sq, bd):
    hi = sq.astype(BF16)
    lo = (sq - hi.astype(F32)).astype(BF16)
    return (jnp.dot(hi, bd, preferred_element_type=F32)
            + jnp.dot(lo, bd, preferred_element_type=F32))


def _inproj_kernel(x_ref, g_ref, w_ref, qkv_ref, gla_ref, mls_ref, gate_ref):
    x = x_ref[0]
    ms = jnp.mean(x * x, axis=-1, keepdims=True)
    h = (x * lax.rsqrt(ms + EPS) * g_ref[...]).astype(BF16)
    col = 0
    for ref in (qkv_ref, gla_ref, mls_ref, gate_ref):
        width = ref.shape[-1]
        ref[0] = jnp.dot(h, w_ref[:, col:col + width], preferred_element_type=F32)
        col += width


def _inproj(x, g, w):
    b, s, d = x.shape
    widths = (QKV_W, 4 * GLA_W, 4 * MLSTM_W, GATE_W)
    row = lambda n: pl.BlockSpec((1, ROW_TILE, n), lambda bi, i: (bi, i, 0))
    const = lambda shape: pl.BlockSpec(shape, lambda bi, i: (0,) * len(shape))
    return pl.pallas_call(
        _inproj_kernel,
        grid=(b, s // ROW_TILE),
        in_specs=[row(d), const((1, d)), const(w.shape)],
        out_specs=[row(n) for n in widths],
        out_shape=[jax.ShapeDtypeStruct((b, s, n), F32) for n in widths],
        compiler_params=_params("parallel", "parallel"),
        name="inproj",
    )(x, g, w)


def _qkprep_kernel(qkv_ref, cos_ref, sin_ref, gain_ref, bd_ref, qt_ref, k_ref, vt_ref):
    x = qkv_ref[0]
    qk = x[:, :QK_W]
    ms = _group_mean(qk * qk, bd_ref[...])
    y = qk * lax.rsqrt(ms + EPS) * gain_ref[...]
    lane = lax.broadcasted_iota(jnp.int32, y.shape, 1)
    first_half = (lane & 31) < 16
    quarter = HEAD_DIM // 4
    swapped = jnp.where(first_half, pltpu.roll(y, QK_W - quarter, 1), pltpu.roll(y, quarter, 1))
    r = y * cos_ref[...] + swapped * sin_ref[...]
    q_t = (r[:, :ATTN_W] * (Q_SCALE * LOG2_E)).T.astype(BF16)
    qt_ref[0] = q_t.reshape(ATTN_HEADS, HEAD_DIM, q_t.shape[-1])
    v_t = x[:, QK_W:].T.astype(BF16)
    v_t = v_t.reshape(ATTN_KV_HEADS, HEAD_DIM, v_t.shape[-1])
    row = lax.broadcasted_iota(jnp.int32, (ATTN_KV_HEADS, VT_ROWS - HEAD_DIM, v_t.shape[-1]), 1)
    vt_ref[0] = jnp.concatenate([v_t, jnp.where(row == 0, 1.0, 0.0).astype(BF16)], axis=1)
    ones_lane = lax.broadcasted_iota(jnp.int32, (x.shape[0], KEY_W - HEAD_DIM), 1) == 0
    k_pad = jnp.where(ones_lane, 1.0, 0.0).astype(BF16)
    for h in range(ATTN_KV_HEADS):
        k_h = r[:, ATTN_W + h * HEAD_DIM:ATTN_W + (h + 1) * HEAD_DIM].astype(BF16)
        k_ref[0, h] = jnp.concatenate([k_h, k_pad], axis=1)


def _qkprep(qkv, cos_t, sin_t, gain, bd):
    b, s, _ = qkv.shape
    const = lambda shape: pl.BlockSpec(shape, lambda bi, i: (0,) * len(shape))
    tab = pl.BlockSpec((ROW_TILE, QK_W), lambda bi, i: (i, 0))
    t_spec = lambda heads, rows: pl.BlockSpec((1, heads, rows, ROW_TILE), lambda bi, i: (bi, 0, 0, i))
    t_shape = lambda heads, rows: jax.ShapeDtypeStruct((b, heads, rows, s), BF16)
    return pl.pallas_call(
        _qkprep_kernel,
        grid=(b, s // ROW_TILE),
        in_specs=[pl.BlockSpec((1, ROW_TILE, QKV_W), lambda bi, i: (bi, i, 0)),
                  tab, tab, const((1, QK_W)), const((QK_W, QK_W))],
        out_specs=[t_spec(ATTN_HEADS, HEAD_DIM),
                   pl.BlockSpec((1, ATTN_KV_HEADS, ROW_TILE, KEY_W), lambda bi, i: (bi, 0, i, 0)),
                   t_spec(ATTN_KV_HEADS, VT_ROWS)],
        out_shape=[t_shape(ATTN_HEADS, HEAD_DIM),
                   jax.ShapeDtypeStruct((b, ATTN_KV_HEADS, s, KEY_W), BF16),
                   t_shape(ATTN_KV_HEADS, VT_ROWS)],
        compiler_params=_params("parallel", "parallel"),
        name="qkprep",
    )(qkv, cos_t, sin_t, gain, bd)


def _attn_kernel(bounded_ref, qt_ref, k_ref, vt_ref, o_ref):
    tq = qt_ref.shape[-1]
    n_chunks = k_ref.shape[2] // ATTN_TK
    keys = lambda c: slice(c * ATTN_TK, (c + 1) * ATTN_TK)
    pad_rows = KEY_W - HEAD_DIM
    lanes = ATTN_STREAM_LANES

    def finish(acc, head, lo):
        out = (acc[:HEAD_DIM] / acc[HEAD_DIM:HEAD_DIM + 1]).T
        o_ref[0, lo:lo + lanes, head * HEAD_DIM:(head + 1) * HEAD_DIM] = out.astype(o_ref.dtype)

    def scores(c, q_ext):
        return jnp.dot(k_ref[0, 0, keys(c), :], q_ext, preferred_element_type=F32)

    def weighted_values(c, p_t):
        return jnp.dot(vt_ref[0, 0, :, keys(c)], p_t, preferred_element_type=F32)

    def fixed_shift_stream(head, lo):
        q_t = qt_ref[0, head, :, lo:lo + lanes]
        s0 = scores(0, jnp.concatenate([q_t, jnp.zeros((pad_rows, lanes), BF16)], axis=0))
        shift = jnp.max(s0, axis=0, keepdims=True).astype(BF16).astype(F32)
        yield
        first_row = lax.broadcasted_iota(jnp.int32, (pad_rows, lanes), 0) == 0
        q_ext = jnp.concatenate([q_t, jnp.where(first_row, -shift, 0.0).astype(BF16)], axis=0)
        nxt = scores(1, q_ext)
        acc = weighted_values(0, jnp.exp2(s0 - shift).astype(BF16))
        yield
        for c in range(1, n_chunks):
            s_t = nxt
            if c + 1 < n_chunks:
                nxt = scores(c + 1, q_ext)
            p_t = jnp.exp2(s_t).astype(BF16)
            acc = acc + weighted_values(c, p_t)
            yield
        finish(acc, head, lo)

    def running_max_stream(head, lo):
        q_ext = jnp.concatenate([qt_ref[0, head, :, lo:lo + lanes], jnp.zeros((pad_rows, lanes), BF16)], axis=0)
        m = jnp.full((1, lanes), -jnp.inf, F32)
        acc = jnp.zeros((VT_ROWS, lanes), F32)
        nxt = scores(0, q_ext)
        yield
        for c in range(n_chunks):
            s_t = nxt
            if c + 1 < n_chunks:
                nxt = scores(c + 1, q_ext)
            m_new = jnp.maximum(m, jnp.max(s_t, axis=0, keepdims=True))
            p_t = jnp.exp2((s_t - m_new).astype(BF16))
            acc = jnp.exp2(m - m_new) * acc + weighted_values(c, p_t)
            m = m_new
            yield
        finish(acc, head, lo)

    def run(stream):
        _interleave(*[stream(j, lo) for j in range(ATTN_GROUP) for lo in range(0, tq, lanes)])

    pl.when(bounded_ref[0] == 1)(functools.partial(run, fixed_shift_stream))
    pl.when(bounded_ref[0] == 0)(functools.partial(run, running_max_stream))


def _attention(bounded, q_t, k, v_t):
    b, _, _, s = q_t.shape
    gw = ATTN_GROUP * HEAD_DIM
    return pl.pallas_call(
        _attn_kernel,
        grid=(b, ATTN_KV_HEADS, s // ATTN_TQ),
        in_specs=[pl.BlockSpec(memory_space=pltpu.SMEM),
                  pl.BlockSpec((1, ATTN_GROUP, HEAD_DIM, ATTN_TQ), lambda bi, h, i: (bi, h, 0, i)),
                  pl.BlockSpec((1, 1, s, KEY_W), lambda bi, h, i: (bi, h, 0, 0)),
                  pl.BlockSpec((1, 1, VT_ROWS, s), lambda bi, h, i: (bi, h, 0, 0))],
        out_specs=pl.BlockSpec((1, ATTN_TQ, gw), lambda bi, h, i: (bi, i, h)),
        out_shape=jax.ShapeDtypeStruct((b, s, ATTN_W), BF16),
        compiler_params=_params("parallel", "parallel", "arbitrary"),
        name="attention",
    )(bounded, q_t, k, v_t)


def _chunk_order(d):
    n_chunks = REC_BLOCK // CHUNK
    return range(n_chunks) if d == 0 else range(n_chunks - 1, -1, -1)


def _glagate_kernel(a_ref, w2hi_ref, w2lo_ref, gb_ref, tbd_ref, cf_ref, cb_ref):
    a = a_ref[0]
    for d, out_ref in enumerate((cf_ref, cb_ref)):
        z = _dot_split(a, w2hi_ref[d], w2lo_ref[d]) + gb_ref[d]
        g = _log_sigmoid(z) * (1.0 / GLA_TAU)
        for sb in range(a.shape[0] // REC_BLOCK):
            rows = slice(sb * REC_BLOCK, (sb + 1) * REC_BLOCK)
            out_ref[0, rows, :] = _cumsum_cols(tbd_ref[d], g[rows])


def _glagate(gate_proj, w2_hi, w2_lo, gb, tbd):
    b, s, _ = gate_proj.shape
    const = lambda shape: pl.BlockSpec(shape, lambda bi, i: (0,) * len(shape))
    out_spec = pl.BlockSpec((1, ROW_TILE, GLA_W), lambda bi, i: (bi, i, 0))
    out_shape = jax.ShapeDtypeStruct((b, s, GLA_W), F32)
    return pl.pallas_call(
        _glagate_kernel,
        grid=(b, s // ROW_TILE),
        in_specs=[pl.BlockSpec((1, ROW_TILE, GATE_W), lambda bi, i: (bi, i, 0)),
                  const(w2_hi.shape), const(w2_lo.shape), const(gb.shape), const(tbd.shape)],
        out_specs=[out_spec, out_spec],
        out_shape=[out_shape, out_shape],
        compiler_params=_params("parallel", "parallel"),
        name="glagate",
    )(gate_proj, w2_hi, w2_lo, gb, tbd)


def _gla_exact_scores(q, cum, k_ref, cum_ref, n, base, d, eye_t, e_h, att_ref):
    sub = lax.broadcasted_iota(jnp.int32, q.shape, 0)

    def one_key(s, carry):
        k_s = k_ref[n, pl.ds(base + s, 1), :]
        c_s = cum_ref[n, pl.ds(base + s, 1), :]
        visible = (sub >= s) if d == 0 else (sub <= s)
        w = q * k_s * jnp.exp(jnp.where(visible, cum - c_s, -jnp.inf))
        per_head = _expand3(w, e_h)
        att_ref[pl.ds(s, 1), :] = jnp.sum(jnp.where(eye_t, per_head, 0.0), axis=0, keepdims=True)
        return carry

    lax.fori_loop(0, CHUNK, one_key, 0)
    return att_ref[...]


def _gla_exponent_range(cum_ref, n):
    worst = jnp.zeros((1, GLA_W), F32)
    for i in range(REC_BLOCK // CHUNK):
        ref = cum_ref[n, i * CHUNK + CHUNK // 2:i * CHUNK + CHUNK // 2 + 1, :]
        for end in (i * CHUNK, (i + 1) * CHUNK - 1):
            worst = jnp.maximum(worst, jnp.abs(cum_ref[n, end:end + 1, :] - ref))
    return worst


def _gla_direction(q_ref, k_ref, v_ref, cum_ref, o_ref, st_ref, d, n, exact_ref=None):
    causal_t, eye_t, head_lane, block_diag = _head_layout_masks(d)
    last = CHUNK - 1 if d == 0 else 0
    st = st_ref[n, d]
    for i in _chunk_order(d):
        rows = slice(i * CHUNK, (i + 1) * CHUNK)
        cum = cum_ref[n, rows, :]
        total = cum[last:last + 1]
        q = q_ref[n, rows, :] * Q_SCALE
        k = k_ref[n, rows, :]
        v = v_ref[n, rows, :]
        qe = q * jnp.exp(cum)
        kd = k * jnp.exp(total - cum)
        if exact_ref is None:
            ref = cum[CHUNK // 2:CHUNK // 2 + 1]
            qt = q * jnp.exp(cum - ref)
            kt = k * jnp.exp(ref - cum)
            q4 = jnp.concatenate([jnp.where(head_lane[h], qt, 0.0) for h in range(GLA_HEADS)], axis=0)
            att_t = jnp.where(causal_t, _dot_dims(kt, q4, NT_DIMS), 0.0)
        else:
            att_t = _gla_exact_scores(q, cum, k_ref, cum_ref, n, i * CHUNK, d, eye_t, *exact_ref)
        yield
        full = _dot_dims(att_t, v, TN_DIMS)
        out = _dot_dims(qe, st, NT_DIMS)
        yield
        for h in range(GLA_HEADS):
            out = out + jnp.where(head_lane[h], full[h * CHUNK:(h + 1) * CHUNK], 0.0)
        o_ref[n, rows, :] = out
        st = st * jnp.exp(total) + jnp.where(block_diag, _dot_dims(v, kd, TN_DIMS), 0.0)
        yield
    st_ref[n, d] = st


def _interleave(*stages):
    stages = list(stages)
    while stages:
        for g in list(stages):
            if next(g, StopIteration) is StopIteration:
                stages.remove(g)


def _gla_kernel(qf_ref, kf_ref, vf_ref, cf_ref, qb_ref, kb_ref, vb_ref, cb_ref, eh_ref,
                of_ref, ob_ref, st_ref, att_ref):
    @pl.when(pl.program_id(1) == 0)
    def _():
        st_ref[...] = jnp.zeros(st_ref.shape, F32)

    n_seq = qf_ref.shape[0]
    worst = jnp.zeros((1, GLA_W), F32)
    for n in range(n_seq):
        worst = jnp.maximum(worst, jnp.maximum(_gla_exponent_range(cf_ref, n), _gla_exponent_range(cb_ref, n)))
    factorisable = jnp.max(worst) < GLA_MAX_EXPONENT

    def run(exact_ref):
        stages = []
        for n in range(n_seq):
            stages.append(_gla_direction(qf_ref, kf_ref, vf_ref, cf_ref, of_ref, st_ref, 0, n, exact_ref))
            stages.append(_gla_direction(qb_ref, kb_ref, vb_ref, cb_ref, ob_ref, st_ref, 1, n, exact_ref))
        if exact_ref is None:
            _interleave(*stages)
        else:
            for stage in stages:
                _interleave(stage)

    pl.when(factorisable)(functools.partial(run, None))
    pl.when(jnp.logical_not(factorisable))(functools.partial(run, (eh_ref[...], att_ref)))


def _gla(gla_proj, cum_f, cum_b, e_h):
    b, s, _ = gla_proj.shape
    nb = s // REC_BLOCK
    fwd = lambda col: pl.BlockSpec((REC_BATCH, REC_BLOCK, GLA_W), lambda bi, j: (bi, j, col))
    bwd = lambda col: pl.BlockSpec((REC_BATCH, REC_BLOCK, GLA_W), lambda bi, j: (bi, nb - 1 - j, col))
    out_shape = jax.ShapeDtypeStruct((b, s, GLA_W), F32)
    return pl.pallas_call(
        _gla_kernel,
        grid=(b // REC_BATCH, nb),
        in_specs=[fwd(0), fwd(1), fwd(2), fwd(0), bwd(0), bwd(1), bwd(2), bwd(0),
                  pl.BlockSpec(e_h.shape, lambda bi, j: (0, 0))],
        out_specs=[fwd(0), bwd(0)],
        out_shape=[out_shape, out_shape],
        scratch_shapes=[pltpu.VMEM((REC_BATCH, 2, GLA_W, GLA_W), F32),
                        pltpu.VMEM((CHUNK, GLA_W), F32)],
        compiler_params=_params("parallel", "arbitrary"),
        name="gla",
    )(gla_proj, gla_proj, gla_proj, cum_f, gla_proj, gla_proj, gla_proj, cum_b, e_h)


def _head_layout_masks(d):
    sub = lax.broadcasted_iota(jnp.int32, (CHUNK, MLSTM_W), 0)
    lane = lax.broadcasted_iota(jnp.int32, (CHUNK, MLSTM_W), 1)
    t_lane = lane & (HEAD_DIM - 1)
    causal_t = (sub <= t_lane) if d == 0 else (sub >= t_lane)
    eye_t = sub == t_lane
    head_lane = [(lane >> 6) == h for h in range(MLSTM_HEADS)]
    brow = lax.broadcasted_iota(jnp.int32, (MLSTM_W, MLSTM_W), 0) >> 6
    bcol = lax.broadcasted_iota(jnp.int32, (MLSTM_W, MLSTM_W), 1) >> 6
    return causal_t, eye_t, head_lane, brow == bcol


def _expand3(x, e):
    return sum(jnp.dot(p, e, preferred_element_type=F32) for p in _split3(x))


def _mlstm_direction(q_ref, k_ref, v_ref, gates, frow, h_ref, tbd, tbd_t, e_r, e_f, e_h,
                     ct_ref, n_ref, m_ref, d, n):
    causal_t, eye_t, head_lane, block_diag = _head_layout_masks(d)
    last = CHUNK - 1 if d == 0 else 0
    n_chunks = REC_BLOCK // CHUNK
    lane = lax.broadcasted_iota(jnp.int32, gates.shape, 1)
    i_lo = 2 * GLA_RANK + d * MLSTM_HEADS
    is_i_lane = (lane >= i_lo) & (lane < i_lo + MLSTM_HEADS)
    fcum = _cumsum_cols(tbd, _log_sigmoid(gates))
    r_exp = _expand3(jnp.where(is_i_lane, gates, fcum), e_r)
    f_last = jnp.concatenate([fcum[c * CHUNK + last:c * CHUNK + last + 1] for c in range(n_chunks)]
                             + [jnp.zeros((SUBLANE - n_chunks, fcum.shape[1]), F32)], axis=0)
    f_tot_rows = _expand3(f_last, e_f)
    fc_rows = _cumsum_rows(_log_sigmoid(frow), tbd_t)

    ct, n_row, m_row = ct_ref[n, d], n_ref[n, d], m_ref[n, d]
    for c in _chunk_order(d):
        rows = slice(c * CHUNK, (c + 1) * CHUNK)
        q, ks, v = q_ref[n, rows, :], k_ref[n, rows, :] * Q_SCALE, v_ref[n, rows, :]
        q4 = jnp.concatenate([jnp.where(head_lane[h], q, 0.0) for h in range(MLSTM_HEADS)], axis=0)
        kn = jnp.concatenate([ks, jnp.broadcast_to(n_row, (2 * SUBLANE, MLSTM_W))], axis=0)
        z = _dot_dims(kn, q4, NT_DIMS)
        s_t, qn = z[:CHUNK], z[CHUNK:CHUNK + 1]
        yield
        rc = r_exp[rows]
        rmax_t = jnp.max(jnp.where(causal_t, rc, -jnp.inf), axis=0, keepdims=True)
        rmax = jnp.max(rc, axis=0, keepdims=True)
        p_t = s_t * jnp.exp(jnp.where(causal_t, rc - rmax_t, -jnp.inf))
        row_sum = jnp.sum(p_t, axis=0, keepdims=True)
        kw = ks * jnp.exp(rc - rmax)
        fc_row = fc_rows[c:c + 1]
        inter = fc_row + m_row
        intra = fc_row + rmax_t
        m_t = jnp.maximum(inter, intra)
        w_inter, w_intra = jnp.exp(inter - m_t), jnp.exp(intra - m_t)
        den = w_inter * qn + w_intra * row_sum
        inv = 1.0 / jnp.maximum(jnp.abs(den), jnp.exp(-m_t))
        full = _dot_dims(p_t * (w_intra * inv), v, TN_DIMS)
        yield
        out = _expand3(jnp.where(eye_t, w_inter * inv, 0.0), e_h) * _dot(q, ct)
        for h in range(MLSTM_HEADS):
            out = out + jnp.where(head_lane[h], full[h * CHUNK:(h + 1) * CHUNK], 0.0)
        h_ref[n, rows, :] = out
        m_max = jnp.maximum(m_row, rmax)
        keep, add = jnp.exp(m_row - m_max), jnp.exp(rmax - m_max)
        ct = keep * ct + add * jnp.where(block_diag, _dot_dims(kw, v, TN_DIMS), 0.0)
        n_row = keep * n_row + add * jnp.sum(kw, axis=0, keepdims=True)
        m_row = f_tot_rows[c:c + 1] + m_max
        yield
    ct_ref[n, d], n_ref[n, d], m_ref[n, d] = ct, n_row, m_row


def _mlstm_kernel(qf_ref, kf_ref, vf_ref, gcf_ref, grf_ref, qb_ref, kb_ref, vb_ref, gcb_ref, grb_ref,
                  bcol_ref, brow_ref, tbd_ref, tbdt_ref, er_ref, ef_ref, eh_ref,
                  hf_ref, hb_ref, ct_ref, n_ref, m_ref):
    @pl.when(pl.program_id(1) == 0)
    def _():
        ct_ref[...] = jnp.zeros(ct_ref.shape, F32)
        n_ref[...] = jnp.zeros(n_ref.shape, F32)
        m_ref[...] = jnp.zeros(m_ref.shape, F32)

    stages = []
    for n in range(qf_ref.shape[0]):
        stages.append(_mlstm_direction(
            qf_ref, kf_ref, vf_ref, gcf_ref[n] + bcol_ref[...], grf_ref[n, 0, 0] + brow_ref[0],
            hf_ref, tbd_ref[0], tbdt_ref[0], er_ref[0], ef_ref[0], eh_ref[...], ct_ref, n_ref, m_ref, 0, n))
        stages.append(_mlstm_direction(
            qb_ref, kb_ref, vb_ref, gcb_ref[n] + bcol_ref[...], grb_ref[n, 0, 0] + brow_ref[1],
            hb_ref, tbd_ref[1], tbdt_ref[1], er_ref[1], ef_ref[1], eh_ref[...], ct_ref, n_ref, m_ref, 1, n))
    _interleave(*stages)


def _mlstm(mls_proj, gate_proj, f_rows, bcol, brow, tbd, tbd_t, e_r, e_f, e_h):
    b, s, _ = mls_proj.shape
    nb = s // REC_BLOCK
    fwd = lambda col: pl.BlockSpec((REC_BATCH, REC_BLOCK, MLSTM_W), lambda bi, j: (bi, j, col))
    bwd = lambda col: pl.BlockSpec((REC_BATCH, REC_BLOCK, MLSTM_W), lambda bi, j: (bi, nb - 1 - j, col))
    const = lambda shape: pl.BlockSpec(shape, lambda bi, j: (0,) * len(shape))
    gcol_f = pl.BlockSpec((REC_BATCH, REC_BLOCK, GATE_W), lambda bi, j: (bi, j, 0))
    gcol_b = pl.BlockSpec((REC_BATCH, REC_BLOCK, GATE_W), lambda bi, j: (bi, nb - 1 - j, 0))
    frow_shape = (REC_BATCH, 1, 1) + f_rows.shape[3:]
    frow_f = pl.BlockSpec(frow_shape, lambda bi, j: (bi, 0, j, 0, 0))
    frow_b = pl.BlockSpec(frow_shape, lambda bi, j: (bi, 1, nb - 1 - j, 0, 0))
    out_shape = jax.ShapeDtypeStruct((b, s, MLSTM_W), F32)
    return pl.pallas_call(
        _mlstm_kernel,
        grid=(b // REC_BATCH, nb),
        in_specs=[fwd(0), fwd(1), fwd(2), gcol_f, frow_f, bwd(0), bwd(1), bwd(2), gcol_b, frow_b,
                  const(bcol.shape), const(brow.shape), const(tbd.shape), const(tbd_t.shape),
                  const(e_r.shape), const(e_f.shape), const(e_h.shape)],
        out_specs=[fwd(0), bwd(0)],
        out_shape=[out_shape, out_shape],
        scratch_shapes=[pltpu.VMEM((REC_BATCH, 2, MLSTM_W, MLSTM_W), F32),
                        pltpu.VMEM((REC_BATCH, 2, 1, MLSTM_W), F32),
                        pltpu.VMEM((REC_BATCH, 2, 1, MLSTM_W), F32)],
        compiler_params=_params("parallel", "arbitrary"),
        name="mlstm",
    )(mls_proj, mls_proj, mls_proj, gate_proj, f_rows,
      mls_proj, mls_proj, mls_proj, gate_proj, f_rows, bcol, brow, tbd, tbd_t, e_r, e_f, e_h)


def _outproj_kernel(x_ref, attn_ref, gof_ref, gob_ref, gg_ref, mhf_ref, mhb_ref, mo_ref,
                    gng_ref, mng_ref, bd_ref, wo_ref, out_ref):
    bd = bd_ref[...]

    def head_norm(t, gain):
        return t * lax.rsqrt(_group_mean(t * t, bd) + EPS) * gain

    gg = gg_ref[0]
    gla = head_norm(gof_ref[0] + gob_ref[0], gng_ref[...]) * (gg * _sigmoid(gg))
    mls = head_norm(_sigmoid(mo_ref[0]) * (mhf_ref[0] + mhb_ref[0]), mng_ref[...])
    acc = jnp.dot(attn_ref[0], wo_ref[:ATTN_W, :], preferred_element_type=F32)
    acc += jnp.dot(gla.astype(BF16), wo_ref[ATTN_W:ATTN_W + GLA_W, :], preferred_element_type=F32)
    acc += jnp.dot(mls.astype(BF16), wo_ref[ATTN_W + GLA_W:, :], preferred_element_type=F32)
    out_ref[0] = x_ref[0] + acc


def _outproj(x, attn, gof, gob, gla_proj, mhf, mhb, mls_proj, gng, mng, bd, wo):
    b, s, d = x.shape
    row = lambda n, col=0: pl.BlockSpec((1, ROW_TILE, n), lambda bi, i: (bi, i, col))
    const = lambda shape: pl.BlockSpec(shape, lambda bi, i: (0,) * len(shape))
    return pl.pallas_call(
        _outproj_kernel,
        grid=(b, s // ROW_TILE),
        in_specs=[row(d), row(ATTN_W), row(GLA_W), row(GLA_W), row(GLA_W, 3),
                  row(MLSTM_W), row(MLSTM_W), row(MLSTM_W, 3),
                  const((1, GLA_W)), const((1, MLSTM_W)), const(bd.shape), const(wo.shape)],
        out_specs=row(d),
        out_shape=jax.ShapeDtypeStruct((b, s, d), F32),
        compiler_params=_params("parallel", "parallel"),
        name="outproj",
    )(x, attn, gof, gob, gla_proj, mhf, mhb, mls_proj, gng, mng, bd, wo)


def _ffn_kernel(x_ref, prev_ref, next_ref, g_ref, wg_ref, wv_ref, cwg_ref, cwv_ref, cbg_ref, cbv_ref,
                wd_ref, out_ref, h_ref, ug_ref, uv_ref, acc_ref):
    i, j = pl.program_id(1), pl.program_id(2)
    tm = x_ref.shape[1]

    def normed(t):
        ms = jnp.mean(t * t, axis=-1, keepdims=True)
        return t * lax.rsqrt(ms + EPS) * g_ref[...]

    @pl.when(j == 0)
    def _():
        keep_prev = (i > 0).astype(F32)
        keep_next = (i < pl.num_programs(1) - 1).astype(F32)
        h_ref[0:HALO, :] = (normed(prev_ref[0]) * keep_prev).astype(BF16)
        h_ref[HALO:HALO + tm, :] = normed(x_ref[0]).astype(BF16)
        h_ref[HALO + tm:, :] = (normed(next_ref[0]) * keep_next).astype(BF16)
        acc_ref[...] = jnp.zeros(acc_ref.shape, F32)

    h = h_ref[...]
    ug_ref[...] = jnp.dot(h, wg_ref[...], preferred_element_type=F32)
    uv_ref[...] = jnp.dot(h, wv_ref[...], preferred_element_type=F32)

    def conv(u_ref, cw_ref, cb_ref):
        out = cb_ref[...] + u_ref[HALO - 1:HALO - 1 + tm, :] * cw_ref[0:1, :]
        for t in range(1, CONV_WIDTH):
            out = out + u_ref[HALO - 1 + t:HALO - 1 + t + tm, :] * cw_ref[t:t + 1, :]
        return out

    gate = conv(ug_ref, cwg_ref, cbg_ref)
    val = conv(uv_ref, cwv_ref, cbv_ref)
    act = (gate * _sigmoid(gate) * val).astype(BF16)
    acc_ref[...] += jnp.dot(act, wd_ref[...], preferred_element_type=F32)

    @pl.when(j == pl.num_programs(2) - 1)
    def _():
        out_ref[0] = x_ref[0] + acc_ref[...]


def _ffn(x, g, w_up, conv_w, conv_b, w_down):
    b, s, d = x.shape
    nj = D_FF // FF_BLOCK
    tiles_per_halo = FFN_ROW_TILE // HALO
    last_halo = s // HALO - 1
    row = pl.BlockSpec((1, FFN_ROW_TILE, d), lambda bi, i, j: (bi, i, 0))
    prev = pl.BlockSpec((1, HALO, d), lambda bi, i, j: (bi, jnp.maximum(i * tiles_per_halo - 1, 0), 0))
    nxt = pl.BlockSpec((1, HALO, d),
                       lambda bi, i, j: (bi, jnp.minimum((i + 1) * tiles_per_halo, last_halo), 0))
    gate_cols = lambda rows: pl.BlockSpec((rows, FF_BLOCK), lambda bi, i, j: (0, j))
    val_cols = lambda rows: pl.BlockSpec((rows, FF_BLOCK), lambda bi, i, j: (0, nj + j))
    return pl.pallas_call(
        _ffn_kernel,
        grid=(b, s // FFN_ROW_TILE, nj),
        in_specs=[row, prev, nxt, pl.BlockSpec((1, d), lambda bi, i, j: (0, 0)),
                  gate_cols(d), val_cols(d), gate_cols(CONV_WIDTH), val_cols(CONV_WIDTH),
                  gate_cols(1), val_cols(1),
                  pl.BlockSpec((FF_BLOCK, d), lambda bi, i, j: (j, 0))],
        out_specs=row,
        out_shape=jax.ShapeDtypeStruct((b, s, d), F32),
        scratch_shapes=[pltpu.VMEM((FFN_ROW_TILE + 2 * HALO, d), BF16),
                        pltpu.VMEM((FFN_ROW_TILE + 2 * HALO, FF_BLOCK), F32),
                        pltpu.VMEM((FFN_ROW_TILE + 2 * HALO, FF_BLOCK), F32),
                        pltpu.VMEM((FFN_ROW_TILE, d), F32)],
        compiler_params=_params("parallel", "parallel", "arbitrary"),
        name="ffn",
    )(x, x, x, g, w_up, w_up, conv_w, conv_w, conv_b, conv_b, w_down)


def _rope_tables(seq):
    n_rows = seq // GRID_W
    row = jnp.repeat(jnp.arange(n_rows, dtype=F32), GRID_W)
    col = jnp.tile(jnp.arange(GRID_W, dtype=F32), n_rows)
    n_freq = HEAD_DIM // 4
    inv_freq = jnp.power(ROPE_THETA, -jnp.arange(n_freq, dtype=F32) / n_freq)
    ang_r, ang_c = row[:, None] * inv_freq, col[:, None] * inv_freq
    cos_h = jnp.concatenate([jnp.cos(ang_r), jnp.cos(ang_r), jnp.cos(ang_c), jnp.cos(ang_c)], axis=1)
    sin_h = jnp.concatenate([-jnp.sin(ang_r), jnp.sin(ang_r), -jnp.sin(ang_c), jnp.sin(ang_c)], axis=1)
    reps = QK_W // HEAD_DIM
    return jnp.tile(cos_h, (1, reps)), jnp.tile(sin_h, (1, reps))


def _block_diag_mean(width):
    group = np.arange(width) // HEAD_DIM
    return jnp.asarray((group[:, None] == group[None, :]).astype(np.float32) / HEAD_DIM, dtype=BF16)


def _tri_constants():
    t = np.arange(REC_BLOCK)
    same_chunk = (t[:, None] // CHUNK) == (t[None, :] // CHUNK)
    prefix = (same_chunk & (t[None, :] <= t[:, None])).astype(np.float32)
    suffix = (same_chunk & (t[None, :] >= t[:, None])).astype(np.float32)
    tbd = jnp.asarray(np.stack([prefix, suffix]), dtype=BF16)
    tbd_t = jnp.asarray(np.stack([prefix.T, suffix.T]), dtype=BF16)
    return tbd, tbd_t


def _mlstm_expanders():
    e_r = np.zeros((2, GATE_W, MLSTM_W), np.float32)
    e_f = np.zeros((2, GATE_W, MLSTM_W), np.float32)
    for d in range(2):
        for h in range(MLSTM_HEADS):
            lanes = slice(h * HEAD_DIM, (h + 1) * HEAD_DIM)
            i_lane = 2 * GLA_RANK + d * MLSTM_HEADS + h
            f_lane = i_lane + 2 * MLSTM_HEADS
            e_r[d, i_lane, lanes] = 1.0
            e_r[d, f_lane, lanes] = -1.0
            e_f[d, f_lane, lanes] = 1.0
    group = np.arange(MLSTM_W) // HEAD_DIM
    e_h = (group[:, None] == group[None, :]).astype(np.float32)
    return jnp.asarray(e_r, dtype=BF16), jnp.asarray(e_f, dtype=BF16), jnp.asarray(e_h, dtype=BF16)


def _layer(x, consts, norm_mix_g, w_in, attn_qn_g, attn_kn_g, gla_gate_w2, gla_gate_b, gla_norm_g,
           mlstm_gate_b, mlstm_norm_g, w_out, norm_ffn_g, w_up, conv_w, conv_b, w_down):
    cos_t, sin_t, bd_qk, bd_head, tbd, tbd_t, e_r, e_f, e_h = consts
    b, s, _ = x.shape
    n_gates = 4 * MLSTM_HEADS
    main_w = QKV_W + 4 * GLA_W
    gla_a = w_in[:, main_w:main_w + 2 * GLA_RANK]
    mls_main = w_in[:, main_w + 2 * GLA_RANK:main_w + 2 * GLA_RANK + 4 * MLSTM_W]
    mls_g = w_in[:, main_w + 2 * GLA_RANK + 4 * MLSTM_W:]
    pad = jnp.zeros((D_MODEL, GATE_W - 2 * GLA_RANK - n_gates), F32)
    w_in_r = jnp.concatenate([w_in[:, :main_w], mls_main, gla_a, mls_g, pad], axis=1).astype(BF16)

    qkv, gla_proj, mls_proj, gate_proj = _inproj(x, norm_mix_g[None, :], w_in_r)

    qk_gain = jnp.concatenate([jnp.tile(attn_qn_g, ATTN_HEADS), jnp.tile(attn_kn_g, ATTN_KV_HEADS)])[None, :]
    q_t, k, v_t = _qkprep(qkv, cos_t, sin_t, qk_gain, bd_qk)
    score_bound = (HEAD_DIM * Q_SCALE * LOG2_E) * jnp.max(jnp.abs(attn_qn_g)) * jnp.max(jnp.abs(attn_kn_g))
    bounded = (score_bound < ATTN_MAX_SCORE_BOUND).astype(jnp.int32).reshape(1)
    attn = _attention(bounded, q_t, k, v_t)

    w2ext = jnp.zeros((2, GATE_W, GLA_W), F32)
    w2ext = w2ext.at[0, :GLA_RANK].set(gla_gate_w2[0]).at[1, GLA_RANK:2 * GLA_RANK].set(gla_gate_w2[1])
    w2_hi = w2ext.astype(BF16)
    w2_lo = (w2ext - w2_hi.astype(F32)).astype(BF16)
    cum_f, cum_b = _glagate(gate_proj, w2_hi, w2_lo, gla_gate_b[:, None, :], tbd)
    gof, gob = _gla(gla_proj, cum_f, cum_b, e_h)

    nb, cpb = s // REC_BLOCK, REC_BLOCK // CHUNK
    f_lo = 2 * GLA_RANK + 2 * MLSTM_HEADS
    f_rows = gate_proj[:, :, f_lo:f_lo + 2 * MLSTM_HEADS].reshape(b, nb, cpb, CHUNK, 2, MLSTM_HEADS)
    f_rows = f_rows.transpose(0, 4, 1, 2, 5, 3).reshape(b, 2, nb, cpb, MLSTM_W)
    f_rows = jnp.pad(f_rows, ((0, 0), (0, 0), (0, 0), (0, SUBLANE - cpb), (0, 0)))
    bcol = jnp.zeros((1, GATE_W), F32).at[0, 2 * GLA_RANK:2 * GLA_RANK + n_gates].set(
        mlstm_gate_b.reshape(n_gates))
    brow = jnp.repeat(mlstm_gate_b[2:], HEAD_DIM, axis=1)[:, None, :]
    mhf, mhb = _mlstm(mls_proj, gate_proj, f_rows, bcol, brow, tbd, tbd_t, e_r, e_f, e_h)

    x = _outproj(x, attn, gof, gob, gla_proj, mhf, mhb, mls_proj,
                 jnp.tile(gla_norm_g, GLA_HEADS)[None, :], jnp.tile(mlstm_norm_g, MLSTM_HEADS)[None, :],
                 bd_head, w_out.astype(BF16))
    return _ffn(x, norm_ffn_g[None, :], w_up.astype(BF16), conv_w, conv_b[None, :], w_down.astype(BF16))


def _trunk(x, weights):
    seq = x.shape[1]
    cos_t, sin_t = _rope_tables(seq)
    consts = ((cos_t, sin_t, _block_diag_mean(QK_W), _block_diag_mean(GLA_W))
              + _tri_constants() + _mlstm_expanders())
    depth = weights[0].shape[0]
    for l in range(depth):
        x = _layer(x, consts, *[w[l] for w in weights])
    return x


def kernel(x_prompt, x_sample, norm_mix_g, w_in, attn_qn_g, attn_kn_g, gla_gate_w2, gla_gate_b,
           gla_norm_g, mlstm_gate_b, mlstm_norm_g, w_out, norm_ffn_g, w_up, conv_w, conv_b, w_down):
    weights = (norm_mix_g, w_in, attn_qn_g, attn_kn_g, gla_gate_w2, gla_gate_b, gla_norm_g,
               mlstm_gate_b, mlstm_norm_g, w_out, norm_ffn_g, w_up, conv_w, conv_b, w_down)
    return _trunk(x_prompt, weights), _trunk(x_sample, weights)
```

```python
import functools

import jax
import jax.numpy as jnp
import numpy as np
from jax import lax
from jax.experimental import pallas as pl
from jax.experimental.pallas import tpu as pltpu

F32 = jnp.float32
BF16 = jnp.bfloat16

D_MODEL = 1024
GRID_W = 64
HEAD_DIM = 64
EPS = 1e-6
ATTN_HEADS = 8
ATTN_KV_HEADS = 2
ATTN_GROUP = ATTN_HEADS // ATTN_KV_HEADS
ROPE_THETA = 10000.0
GLA_HEADS = 4
GLA_RANK = 16
GLA_TAU = 16.0
MLSTM_HEADS = 4
CHUNK = 64
GLA_MAX_EXPONENT = 40.0
ATTN_W = ATTN_HEADS * HEAD_DIM
KV_W = ATTN_KV_HEADS * HEAD_DIM
GLA_W = GLA_HEADS * HEAD_DIM
MLSTM_W = MLSTM_HEADS * HEAD_DIM
QK_W = ATTN_W + KV_W
QKV_W = ATTN_W + 2 * KV_W
D_FF = 2816
CONV_WIDTH = 3
GATE_W = 128
Q_SCALE = HEAD_DIM ** -0.5
LOG2_E = 1.4426950408889634

SUBLANE = 8
VMEM_LIMIT = 56 * 1024 * 1024

ROW_TILE = 1024
FFN_ROW_TILE = 512
REC_BLOCK = 256
REC_BATCH = 4
ATTN_TQ = 512
ATTN_STREAM_LANES = 256
ATTN_TK = 256
VT_ROWS = HEAD_DIM + 16
KEY_W = 128
ATTN_MAX_SCORE_BOUND = 32.0
FF_BLOCK = 1408
HALO = SUBLANE

NT_DIMS = (((1,), (1,)), ((), ()))
TN_DIMS = (((0,), (0,)), ((), ()))


def _params(*sem):
    return pltpu.CompilerParams(dimension_semantics=sem, vmem_limit_bytes=VMEM_LIMIT)


def _log_sigmoid(z):
    return jnp.minimum(z, 0.0) - jnp.log(1.0 + jnp.exp(-jnp.abs(z)))


def _sigmoid(z):
    return 1.0 / (1.0 + jnp.exp(-z))


def _dot(a, b):
    return jnp.dot(a.astype(BF16), b.astype(BF16), preferred_element_type=F32)


def _dot_dims(a, b, dims):
    return lax.dot_general(a.astype(BF16), b.astype(BF16), dims, preferred_element_type=F32)


def _split3(x):
    p1 = x.astype(BF16)
    rest = x - p1.astype(F32)
    p2 = rest.astype(BF16)
    return p1, p2, (rest - p2.astype(F32)).astype(BF16)


def _cumsum_cols(tbd, x):
    return sum(jnp.dot(tbd, p, preferred_element_type=F32) for p in _split3(x))


def _cumsum_rows(x, tbd_t):
    return sum(jnp.dot(p, tbd_t, preferred_element_type=F32) for p in _split3(x))


def _dot_split(a, b_hi, b_lo):
    a_hi = a.astype(BF16)
    a_lo = (a - a_hi.astype(F32)).astype(BF16)
    return (jnp.dot(a_hi, b_hi, preferred_element_type=F32)
            + jnp.dot(a_lo, b_hi, preferred_element_type=F32)
            + jnp.dot(a_hi, b_lo, preferred_element_type=F32))


def _group_mean(sq, bd):
    hi = sq.astype(BF16)
    lo = (sq - hi.astype(F32)).astype(BF16)
    return (jnp.dot(hi, bd, preferred_element_type=F32)
            + jnp.dot(lo, bd, preferred_element_type=F32))


def _inproj_kernel(x_ref, g_ref, w_ref, qkv_ref, gla_ref, mls_ref, gate_ref):
    x = x_ref[0]
    ms = jnp.mean(x * x, axis=-1, keepdims=True)
    h = (x * lax.rsqrt(ms + EPS) * g_ref[...]).astype(BF16)
    col = 0
    for ref in (qkv_ref, gla_ref, mls_ref, gate_ref):
        width = ref.shape[-1]
        ref[0] = jnp.dot(h, w_ref[:, col:col + width], preferred_element_type=F32).astype(ref.dtype)
        col += width


def _inproj(x, g, w):
    b, s, d = x.shape
    widths = (QKV_W, 4 * GLA_W, 4 * MLSTM_W, GATE_W)
    row = lambda n: pl.BlockSpec((1, ROW_TILE, n), lambda bi, i: (bi, i, 0))
    const = lambda shape: pl.BlockSpec(shape, lambda bi, i: (0,) * len(shape))
    return pl.pallas_call(
        _inproj_kernel,
        grid=(b, s // ROW_TILE),
        in_specs=[row(d), const((1, d)), const(w.shape)],
        out_specs=[row(n) for n in widths],
        out_shape=[jax.ShapeDtypeStruct((b, s, n), dt) for n, dt in zip(widths, (F32, F32, BF16, F32))],
        compiler_params=_params("parallel", "parallel"),
        name="inproj",
    )(x, g, w)


def _qkprep_kernel(qkv_ref, cos_ref, sin_ref, gain_ref, bd_ref, qt_ref, k_ref, vt_ref):
    x = qkv_ref[0]
    qk = x[:, :QK_W]
    ms = _group_mean(qk * qk, bd_ref[...])
    y = qk * lax.rsqrt(ms + EPS) * gain_ref[...]
    lane = lax.broadcasted_iota(jnp.int32, y.shape, 1)
    first_half = (lane & 31) < 16
    quarter = HEAD_DIM // 4
    swapped = jnp.where(first_half, pltpu.roll(y, QK_W - quarter, 1), pltpu.roll(y, quarter, 1))
    r = y * cos_ref[...] + swapped * sin_ref[...]
    q_t = (r[:, :ATTN_W] * (Q_SCALE * LOG2_E)).T.astype(BF16)
    qt_ref[0] = q_t.reshape(ATTN_HEADS, HEAD_DIM, q_t.shape[-1])
    v_t = x[:, QK_W:].T.astype(BF16)
    v_t = v_t.reshape(ATTN_KV_HEADS, HEAD_DIM, v_t.shape[-1])
    row = lax.broadcasted_iota(jnp.int32, (ATTN_KV_HEADS, VT_ROWS - HEAD_DIM, v_t.shape[-1]), 1)
    vt_ref[0] = jnp.concatenate([v_t, jnp.where(row == 0, 1.0, 0.0).astype(BF16)], axis=1)
    ones_lane = lax.broadcasted_iota(jnp.int32, (x.shape[0], KEY_W - HEAD_DIM), 1) == 0
    k_pad = jnp.where(ones_lane, 1.0, 0.0).astype(BF16)
    for h in range(ATTN_KV_HEADS):
        k_h = r[:, ATTN_W + h * HEAD_DIM:ATTN_W + (h + 1) * HEAD_DIM].astype(BF16)
        k_ref[0, h] = jnp.concatenate([k_h, k_pad], axis=1)


def _qkprep(qkv, cos_t, sin_t, gain, bd):
    b, s, _ = qkv.shape
    const = lambda shape: pl.BlockSpec(shape, lambda bi, i: (0,) * len(shape))
    tab = pl.BlockSpec((ROW_TILE, QK_W), lambda bi, i: (i, 0))
    t_spec = lambda heads, rows: pl.BlockSpec((1, heads, rows, ROW_TILE), lambda bi, i: (bi, 0, 0, i))
    t_shape = lambda heads, rows: jax.ShapeDtypeStruct((b, heads, rows, s), BF16)
    return pl.pallas_call(
        _qkprep_kernel,
        grid=(b, s // ROW_TILE),
        in_specs=[pl.BlockSpec((1, ROW_TILE, QKV_W), lambda bi, i: (bi, i, 0)),
                  tab, tab, const((1, QK_W)), const((QK_W, QK_W))],
        out_specs=[t_spec(ATTN_HEADS, HEAD_DIM),
                   pl.BlockSpec((1, ATTN_KV_HEADS, ROW_TILE, KEY_W), lambda bi, i: (bi, 0, i, 0)),
                   t_spec(ATTN_KV_HEADS, VT_ROWS)],
        out_shape=[t_shape(ATTN_HEADS, HEAD_DIM),
                   jax.ShapeDtypeStruct((b, ATTN_KV_HEADS, s, KEY_W), BF16),
                   t_shape(ATTN_KV_HEADS, VT_ROWS)],
        compiler_params=_params("parallel", "parallel"),
        name="qkprep",
    )(qkv, cos_t, sin_t, gain, bd)


def _attn_kernel(bounded_ref, qt_ref, k_ref, vt_ref, o_ref):
    tq = qt_ref.shape[-1]
    n_chunks = k_ref.shape[2] // ATTN_TK
    keys = lambda c: slice(c * ATTN_TK, (c + 1) * ATTN_TK)
    pad_rows = KEY_W - HEAD_DIM
    lanes = ATTN_STREAM_LANES

    def finish(acc, head, lo):
        out = (acc[:HEAD_DIM] / acc[HEAD_DIM:HEAD_DIM + 1]).T
        o_ref[0, lo:lo + lanes, head * HEAD_DIM:(head + 1) * HEAD_DIM] = out.astype(o_ref.dtype)

    def scores(c, q_ext):
        return jnp.dot(k_ref[0, 0, keys(c), :], q_ext, preferred_element_type=F32)

    def weighted_values(c, p_t):
        return jnp.dot(vt_ref[0, 0, :, keys(c)], p_t, preferred_element_type=F32)

    def fixed_shift_stream(head, lo):
        q_t = qt_ref[0, head, :, lo:lo + lanes]
        s0 = scores(0, jnp.concatenate([q_t, jnp.zeros((pad_rows, lanes), BF16)], axis=0))
        shift = jnp.max(s0, axis=0, keepdims=True).astype(BF16).astype(F32)
        yield
        first_row = lax.broadcasted_iota(jnp.int32, (pad_rows, lanes), 0) == 0
        q_ext = jnp.concatenate([q_t, jnp.where(first_row, -shift, 0.0).astype(BF16)], axis=0)
        nxt = scores(1, q_ext)
        acc = weighted_values(0, jnp.exp2(s0 - shift).astype(BF16))
        yield
        for c in range(1, n_chunks):
            s_t = nxt
            if c + 1 < n_chunks:
                nxt = scores(c + 1, q_ext)
            p_t = jnp.exp2(s_t).astype(BF16)
            acc = acc + weighted_values(c, p_t)
            yield
        finish(acc, head, lo)

    def running_max_stream(head, lo):
        q_ext = jnp.concatenate([qt_ref[0, head, :, lo:lo + lanes], jnp.zeros((pad_rows, lanes), BF16)], axis=0)
        m = jnp.full((1, lanes), -jnp.inf, F32)
        acc = jnp.zeros((VT_ROWS, lanes), F32)
        nxt = scores(0, q_ext)
        yield
        for c in range(n_chunks):
            s_t = nxt
            if c + 1 < n_chunks:
                nxt = scores(c + 1, q_ext)
            m_new = jnp.maximum(m, jnp.max(s_t, axis=0, keepdims=True))
            p_t = jnp.exp2((s_t - m_new).astype(BF16))
            acc = jnp.exp2(m - m_new) * acc + weighted_values(c, p_t)
            m = m_new
            yield
        finish(acc, head, lo)

    def run(stream):
        _interleave(*[stream(j, lo) for j in range(ATTN_GROUP) for lo in range(0, tq, lanes)])

    pl.when(bounded_ref[0] == 1)(functools.partial(run, fixed_shift_stream))
    pl.when(bounded_ref[0] == 0)(functools.partial(run, running_max_stream))


def _attention(bounded, q_t, k, v_t):
    b, _, _, s = q_t.shape
    gw = ATTN_GROUP * HEAD_DIM
    return pl.pallas_call(
        _attn_kernel,
        grid=(b, ATTN_KV_HEADS, s // ATTN_TQ),
        in_specs=[pl.BlockSpec(memory_space=pltpu.SMEM),
                  pl.BlockSpec((1, ATTN_GROUP, HEAD_DIM, ATTN_TQ), lambda bi, h, i: (bi, h, 0, i)),
                  pl.BlockSpec((1, 1, s, KEY_W), lambda bi, h, i: (bi, h, 0, 0)),
                  pl.BlockSpec((1, 1, VT_ROWS, s), lambda bi, h, i: (bi, h, 0, 0))],
        out_specs=pl.BlockSpec((1, ATTN_TQ, gw), lambda bi, h, i: (bi, i, h)),
        out_shape=jax.ShapeDtypeStruct((b, s, ATTN_W), BF16),
        compiler_params=_params("parallel", "parallel", "arbitrary"),
        name="attention",
    )(bounded, q_t, k, v_t)


def _chunk_order(d):
    n_chunks = REC_BLOCK // CHUNK
    return range(n_chunks) if d == 0 else range(n_chunks - 1, -1, -1)


def _glagate_kernel(a_ref, w2hi_ref, w2lo_ref, gb_ref, tbd_ref, cf_ref, cb_ref):
    a = a_ref[0]
    for d, out_ref in enumerate((cf_ref, cb_ref)):
        z = _dot_split(a, w2hi_ref[d], w2lo_ref[d]) + gb_ref[d]
        g = _log_sigmoid(z) * (1.0 / GLA_TAU)
        for sb in range(a.shape[0] // REC_BLOCK):
            rows = slice(sb * REC_BLOCK, (sb + 1) * REC_BLOCK)
            out_ref[0, rows, :] = _cumsum_cols(tbd_ref[d], g[rows])


def _glagate(gate_proj, w2_hi, w2_lo, gb, tbd):
    b, s, _ = gate_proj.shape
    const = lambda shape: pl.BlockSpec(shape, lambda bi, i: (0,) * len(shape))
    out_spec = pl.BlockSpec((1, ROW_TILE, GLA_W), lambda bi, i: (bi, i, 0))
    out_shape = jax.ShapeDtypeStruct((b, s, GLA_W), F32)
    return pl.pallas_call(
        _glagate_kernel,
        grid=(b, s // ROW_TILE),
        in_specs=[pl.BlockSpec((1, ROW_TILE, GATE_W), lambda bi, i: (bi, i, 0)),
                  const(w2_hi.shape), const(w2_lo.shape), const(gb.shape), const(tbd.shape)],
        out_specs=[out_spec, out_spec],
        out_shape=[out_shape, out_shape],
        compiler_params=_params("parallel", "parallel"),
        name="glagate",
    )(gate_proj, w2_hi, w2_lo, gb, tbd)


def _gla_exact_scores(q, cum, k_ref, cum_ref, n, base, d, eye_t, e_h, att_ref):
    sub = lax.broadcasted_iota(jnp.int32, q.shape, 0)

    def one_key(s, carry):
        k_s = k_ref[n, pl.ds(base + s, 1), :]
        c_s = cum_ref[n, pl.ds(base + s, 1), :]
        visible = (sub >= s) if d == 0 else (sub <= s)
        w = q * k_s * jnp.exp(jnp.where(visible, cum - c_s, -jnp.inf))
        per_head = _expand3(w, e_h)
        att_ref[pl.ds(s, 1), :] = jnp.sum(jnp.where(eye_t, per_head, 0.0), axis=0, keepdims=True)
        return carry

    lax.fori_loop(0, CHUNK, one_key, 0)
    return att_ref[...]


def _gla_exponent_range(cum_ref, n):
    worst = jnp.zeros((1, GLA_W), F32)
    for i in range(REC_BLOCK // CHUNK):
        ref = cum_ref[n, i * CHUNK + CHUNK // 2:i * CHUNK + CHUNK // 2 + 1, :]
        for end in (i * CHUNK, (i + 1) * CHUNK - 1):
            worst = jnp.maximum(worst, jnp.abs(cum_ref[n, end:end + 1, :] - ref))
    return worst


def _gla_direction(q_ref, k_ref, v_ref, cum_ref, o_ref, st_ref, d, n, exact_ref=None):
    causal_t, eye_t, head_lane, block_diag = _head_layout_masks(d)
    last = CHUNK - 1 if d == 0 else 0
    st = st_ref[n, d]
    for i in _chunk_order(d):
        rows = slice(i * CHUNK, (i + 1) * CHUNK)
        cum = cum_ref[n, rows, :]
        total = cum[last:last + 1]
        q = q_ref[n, rows, :] * Q_SCALE
        k = k_ref[n, rows, :]
        v = v_ref[n, rows, :]
        qe = q * jnp.exp(cum)
        kd = k * jnp.exp(total - cum)
        if exact_ref is None:
            ref = cum[CHUNK // 2:CHUNK // 2 + 1]
            qt = q * jnp.exp(cum - ref)
            kt = k * jnp.exp(ref - cum)
            q4 = jnp.concatenate([jnp.where(head_lane[h], qt, 0.0) for h in range(GLA_HEADS)], axis=0)
            att_t = jnp.where(causal_t, _dot_dims(kt, q4, NT_DIMS), 0.0)
        else:
            att_t = _gla_exact_scores(q, cum, k_ref, cum_ref, n, i * CHUNK, d, eye_t, *exact_ref)
        yield
        full = _dot_dims(att_t, v, TN_DIMS)
        out = _dot_dims(qe, st, NT_DIMS)
        yield
        for h in range(GLA_HEADS):
            out = out + jnp.where(head_lane[h], full[h * CHUNK:(h + 1) * CHUNK], 0.0)
        o_ref[n, rows, :] = out
        st = st * jnp.exp(total) + jnp.where(block_diag, _dot_dims(v, kd, TN_DIMS), 0.0)
        yield
    st_ref[n, d] = st


def _interleave(*stages):
    stages = list(stages)
    while stages:
        for g in list(stages):
            if next(g, StopIteration) is StopIteration:
                stages.remove(g)


def _gla_kernel(qf_ref, kf_ref, vf_ref, cf_ref, qb_ref, kb_ref, vb_ref, cb_ref, eh_ref,
                of_ref, ob_ref, st_ref, att_ref):
    @pl.when(pl.program_id(1) == 0)
    def _():
        st_ref[...] = jnp.zeros(st_ref.shape, F32)

    n_seq = qf_ref.shape[0]
    worst = jnp.zeros((1, GLA_W), F32)
    for n in range(n_seq):
        worst = jnp.maximum(worst, jnp.maximum(_gla_exponent_range(cf_ref, n), _gla_exponent_range(cb_ref, n)))
    factorisable = jnp.max(worst) < GLA_MAX_EXPONENT

    def run(exact_ref):
        stages = []
        for n in range(n_seq):
            stages.append(_gla_direction(qf_ref, kf_ref, vf_ref, cf_ref, of_ref, st_ref, 0, n, exact_ref))
            stages.append(_gla_direction(qb_ref, kb_ref, vb_ref, cb_ref, ob_ref, st_ref, 1, n, exact_ref))
        if exact_ref is None:
            _interleave(*stages)
        else:
            for stage in stages:
                _interleave(stage)

    pl.when(factorisable)(functools.partial(run, None))
    pl.when(jnp.logical_not(factorisable))(functools.partial(run, (eh_ref[...], att_ref)))


def _gla(gla_proj, cum_f, cum_b, e_h):
    b, s, _ = gla_proj.shape
    nb = s // REC_BLOCK
    fwd = lambda col: pl.BlockSpec((REC_BATCH, REC_BLOCK, GLA_W), lambda bi, j: (bi, j, col))
    bwd = lambda col: pl.BlockSpec((REC_BATCH, REC_BLOCK, GLA_W), lambda bi, j: (bi, nb - 1 - j, col))
    out_shape = jax.ShapeDtypeStruct((b, s, GLA_W), F32)
    return pl.pallas_call(
        _gla_kernel,
        grid=(b // REC_BATCH, nb),
        in_specs=[fwd(0), fwd(1), fwd(2), fwd(0), bwd(0), bwd(1), bwd(2), bwd(0),
                  pl.BlockSpec(e_h.shape, lambda bi, j: (0, 0))],
        out_specs=[fwd(0), bwd(0)],
        out_shape=[out_shape, out_shape],
        scratch_shapes=[pltpu.VMEM((REC_BATCH, 2, GLA_W, GLA_W), F32),
                        pltpu.VMEM((CHUNK, GLA_W), F32)],
        compiler_params=_params("parallel", "arbitrary"),
        name="gla",
    )(gla_proj, gla_proj, gla_proj, cum_f, gla_proj, gla_proj, gla_proj, cum_b, e_h)


def _head_layout_masks(d):
    sub = lax.broadcasted_iota(jnp.int32, (CHUNK, MLSTM_W), 0)
    lane = lax.broadcasted_iota(jnp.int32, (CHUNK, MLSTM_W), 1)
    t_lane = lane & (HEAD_DIM - 1)
    causal_t = (sub <= t_lane) if d == 0 else (sub >= t_lane)
    eye_t = sub == t_lane
    head_lane = [(lane >> 6) == h for h in range(MLSTM_HEADS)]
    brow = lax.broadcasted_iota(jnp.int32, (MLSTM_W, MLSTM_W), 0) >> 6
    bcol = lax.broadcasted_iota(jnp.int32, (MLSTM_W, MLSTM_W), 1) >> 6
    return causal_t, eye_t, head_lane, brow == bcol


def _expand3(x, e):
    return sum(jnp.dot(p, e, preferred_element_type=F32) for p in _split3(x))


def _mlstm_direction(q_ref, k_ref, v_ref, gates, frow, h_ref, tbd, tbd_t, e_r, e_f, e_h,
                     ct_ref, n_ref, m_ref, d, n):
    causal_t, eye_t, head_lane, block_diag = _head_layout_masks(d)
    last = CHUNK - 1 if d == 0 else 0
    n_chunks = REC_BLOCK // CHUNK
    lane = lax.broadcasted_iota(jnp.int32, gates.shape, 1)
    i_lo = 2 * GLA_RANK + d * MLSTM_HEADS
    is_i_lane = (lane >= i_lo) & (lane < i_lo + MLSTM_HEADS)
    fcum = _cumsum_cols(tbd, _log_sigmoid(gates))
    r_exp = _expand3(jnp.where(is_i_lane, gates, fcum), e_r)
    f_last = jnp.concatenate([fcum[c * CHUNK + last:c * CHUNK + last + 1] for c in range(n_chunks)]
                             + [jnp.zeros((SUBLANE - n_chunks, fcum.shape[1]), F32)], axis=0)
    f_tot_rows = _expand3(f_last, e_f)
    fc_rows = _cumsum_rows(_log_sigmoid(frow), tbd_t)

    ct, n_row, m_row = ct_ref[n, d], n_ref[n, d], m_ref[n, d]
    for c in _chunk_order(d):
        rows = slice(c * CHUNK, (c + 1) * CHUNK)
        q = q_ref[n, rows, :].astype(F32)
        ks = k_ref[n, rows, :].astype(F32) * Q_SCALE
        v = v_ref[n, rows, :].astype(F32)
        q4 = jnp.concatenate([jnp.where(head_lane[h], q, 0.0) for h in range(MLSTM_HEADS)], axis=0)
        kn = jnp.concatenate([ks, jnp.broadcast_to(n_row, (2 * SUBLANE, MLSTM_W))], axis=0)
        z = _dot_dims(kn, q4, NT_DIMS)
        s_t, qn = z[:CHUNK], z[CHUNK:CHUNK + 1]
        yield
        rc = r_exp[rows]
        rmax_t = jnp.max(jnp.where(causal_t, rc, -jnp.inf), axis=0, keepdims=True)
        rmax = jnp.max(rc, axis=0, keepdims=True)
        p_t = s_t * jnp.exp(jnp.where(causal_t, rc - rmax_t, -jnp.inf))
        row_sum = jnp.sum(p_t, axis=0, keepdims=True)
        kw = ks * jnp.exp(rc - rmax)
        fc_row = fc_rows[c:c + 1]
        inter = fc_row + m_row
        intra = fc_row + rmax_t
        m_t = jnp.maximum(inter, intra)
        w_inter, w_intra = jnp.exp(inter - m_t), jnp.exp(intra - m_t)
        den = w_inter * qn + w_intra * row_sum
        inv = 1.0 / jnp.maximum(jnp.abs(den), jnp.exp(-m_t))
        full = _dot_dims(p_t * (w_intra * inv), v, TN_DIMS)
        yield
        out = _expand3(jnp.where(eye_t, w_inter * inv, 0.0), e_h) * _dot(q, ct)
        for h in range(MLSTM_HEADS):
            out = out + jnp.where(head_lane[h], full[h * CHUNK:(h + 1) * CHUNK], 0.0)
        h_ref[n, rows, :] = out
        m_max = jnp.maximum(m_row, rmax)
        keep, add = jnp.exp(m_row - m_max), jnp.exp(rmax - m_max)
        ct = keep * ct + add * jnp.where(block_diag, _dot_dims(kw, v, TN_DIMS), 0.0)
        n_row = keep * n_row + add * jnp.sum(kw, axis=0, keepdims=True)
        m_row = f_tot_rows[c:c + 1] + m_max
        yield
    ct_ref[n, d], n_ref[n, d], m_ref[n, d] = ct, n_row, m_row


def _mlstm_kernel(qf_ref, kf_ref, vf_ref, gcf_ref, grf_ref, qb_ref, kb_ref, vb_ref, gcb_ref, grb_ref,
                  bcol_ref, brow_ref, tbd_ref, tbdt_ref, er_ref, ef_ref, eh_ref,
                  hf_ref, hb_ref, ct_ref, n_ref, m_ref):
    @pl.when(pl.program_id(1) == 0)
    def _():
        ct_ref[...] = jnp.zeros(ct_ref.shape, F32)
        n_ref[...] = jnp.zeros(n_ref.shape, F32)
        m_ref[...] = jnp.zeros(m_ref.shape, F32)

    stages = []
    for n in range(qf_ref.shape[0]):
        stages.append(_mlstm_direction(
            qf_ref, kf_ref, vf_ref, gcf_ref[n] + bcol_ref[...], grf_ref[n, 0, 0] + brow_ref[0],
            hf_ref, tbd_ref[0], tbdt_ref[0], er_ref[0], ef_ref[0], eh_ref[...], ct_ref, n_ref, m_ref, 0, n))
        stages.append(_mlstm_direction(
            qb_ref, kb_ref, vb_ref, gcb_ref[n] + bcol_ref[...], grb_ref[n, 0, 0] + brow_ref[1],
            hb_ref, tbd_ref[1], tbdt_ref[1], er_ref[1], ef_ref[1], eh_ref[...], ct_ref, n_ref, m_ref, 1, n))
    _interleave(*stages)


def _mlstm(mls_proj, gate_proj, f_rows, bcol, brow, tbd, tbd_t, e_r, e_f, e_h):
    b, s, _ = mls_proj.shape
    nb = s // REC_BLOCK
    fwd = lambda col: pl.BlockSpec((REC_BATCH, REC_BLOCK, MLSTM_W), lambda bi, j: (bi, j, col))
    bwd = lambda col: pl.BlockSpec((REC_BATCH, REC_BLOCK, MLSTM_W), lambda bi, j: (bi, nb - 1 - j, col))
    const = lambda shape: pl.BlockSpec(shape, lambda bi, j: (0,) * len(shape))
    gcol_f = pl.BlockSpec((REC_BATCH, REC_BLOCK, GATE_W), lambda bi, j: (bi, j, 0))
    gcol_b = pl.BlockSpec((REC_BATCH, REC_BLOCK, GATE_W), lambda bi, j: (bi, nb - 1 - j, 0))
    frow_shape = (REC_BATCH, 1, 1) + f_rows.shape[3:]
    frow_f = pl.BlockSpec(frow_shape, lambda bi, j: (bi, 0, j, 0, 0))
    frow_b = pl.BlockSpec(frow_shape, lambda bi, j: (bi, 1, nb - 1 - j, 0, 0))
    out_shape = jax.ShapeDtypeStruct((b, s, MLSTM_W), F32)
    return pl.pallas_call(
        _mlstm_kernel,
        grid=(b // REC_BATCH, nb),
        in_specs=[fwd(0), fwd(1), fwd(2), gcol_f, frow_f, bwd(0), bwd(1), bwd(2), gcol_b, frow_b,
                  const(bcol.shape), const(brow.shape), const(tbd.shape), const(tbd_t.shape),
                  const(e_r.shape), const(e_f.shape), const(e_h.shape)],
        out_specs=[fwd(0), bwd(0)],
        out_shape=[out_shape, out_shape],
        scratch_shapes=[pltpu.VMEM((REC_BATCH, 2, MLSTM_W, MLSTM_W), F32),
                        pltpu.VMEM((REC_BATCH, 2, 1, MLSTM_W), F32),
                        pltpu.VMEM((REC_BATCH, 2, 1, MLSTM_W), F32)],
        compiler_params=_params("parallel", "arbitrary"),
        name="mlstm",
    )(mls_proj, mls_proj, mls_proj, gate_proj, f_rows,
      mls_proj, mls_proj, mls_proj, gate_proj, f_rows, bcol, brow, tbd, tbd_t, e_r, e_f, e_h)


def _outproj_kernel(x_ref, attn_ref, gof_ref, gob_ref, gg_ref, mhf_ref, mhb_ref, mo_ref,
                    gng_ref, mng_ref, bd_ref, wo_ref, out_ref):
    bd = bd_ref[...]

    def head_norm(t, gain):
        return t * lax.rsqrt(_group_mean(t * t, bd) + EPS) * gain

    gg = gg_ref[0]
    gla = head_norm(gof_ref[0] + gob_ref[0], gng_ref[...]) * (gg * _sigmoid(gg))
    mls = head_norm(_sigmoid(mo_ref[0].astype(F32)) * (mhf_ref[0] + mhb_ref[0]), mng_ref[...])
    acc = jnp.dot(attn_ref[0], wo_ref[:ATTN_W, :], preferred_element_type=F32)
    acc += jnp.dot(gla.astype(BF16), wo_ref[ATTN_W:ATTN_W + GLA_W, :], preferred_element_type=F32)
    acc += jnp.dot(mls.astype(BF16), wo_ref[ATTN_W + GLA_W:, :], preferred_element_type=F32)
    out_ref[0] = x_ref[0] + acc


def _outproj(x, attn, gof, gob, gla_proj, mhf, mhb, mls_proj, gng, mng, bd, wo):
    b, s, d = x.shape
    row = lambda n, col=0: pl.BlockSpec((1, ROW_TILE, n), lambda bi, i: (bi, i, col))
    const = lambda shape: pl.BlockSpec(shape, lambda bi, i: (0,) * len(shape))
    return pl.pallas_call(
        _outproj_kernel,
        grid=(b, s // ROW_TILE),
        in_specs=[row(d), row(ATTN_W), row(GLA_W), row(GLA_W), row(GLA_W, 3),
                  row(MLSTM_W), row(MLSTM_W), row(MLSTM_W, 3),
                  const((1, GLA_W)), const((1, MLSTM_W)), const(bd.shape), const(wo.shape)],
        out_specs=row(d),
        out_shape=jax.ShapeDtypeStruct((b, s, d), F32),
        compiler_params=_params("parallel", "parallel"),
        name="outproj",
    )(x, attn, gof, gob, gla_proj, mhf, mhb, mls_proj, gng, mng, bd, wo)


def _ffn_kernel(x_ref, prev_ref, next_ref, g_ref, wg_ref, wv_ref, cwg_ref, cwv_ref, cbg_ref, cbv_ref,
                wd_ref, out_ref, h_ref, ug_ref, uv_ref, acc_ref):
    i, j = pl.program_id(1), pl.program_id(2)
    tm = x_ref.shape[1]

    def normed(t):
        ms = jnp.mean(t * t, axis=-1, keepdims=True)
        return t * lax.rsqrt(ms + EPS) * g_ref[...]

    @pl.when(j == 0)
    def _():
        keep_prev = (i > 0).astype(F32)
        keep_next = (i < pl.num_programs(1) - 1).astype(F32)
        h_ref[0:HALO, :] = (normed(prev_ref[0]) * keep_prev).astype(BF16)
        h_ref[HALO:HALO + tm, :] = normed(x_ref[0]).astype(BF16)
        h_ref[HALO + tm:, :] = (normed(next_ref[0]) * keep_next).astype(BF16)
        acc_ref[...] = jnp.zeros(acc_ref.shape, F32)

    h = h_ref[...]
    ug_ref[...] = jnp.dot(h, wg_ref[...], preferred_element_type=F32)
    uv_ref[...] = jnp.dot(h, wv_ref[...], preferred_element_type=F32)

    def conv(u_ref, cw_ref, cb_ref):
        out = cb_ref[...] + u_ref[HALO - 1:HALO - 1 + tm, :] * cw_ref[0:1, :]
        for t in range(1, CONV_WIDTH):
            out = out + u_ref[HALO - 1 + t:HALO - 1 + t + tm, :] * cw_ref[t:t + 1, :]
        return out

    gate = conv(ug_ref, cwg_ref, cbg_ref)
    val = conv(uv_ref, cwv_ref, cbv_ref)
    act = (gate * _sigmoid(gate) * val).astype(BF16)
    acc_ref[...] += jnp.dot(act, wd_ref[...], preferred_element_type=F32)

    @pl.when(j == pl.num_programs(2) - 1)
    def _():
        out_ref[0] = x_ref[0] + acc_ref[...]


def _ffn(x, g, w_up, conv_w, conv_b, w_down):
    b, s, d = x.shape
    nj = D_FF // FF_BLOCK
    tiles_per_halo = FFN_ROW_TILE // HALO
    last_halo = s // HALO - 1
    row = pl.BlockSpec((1, FFN_ROW_TILE, d), lambda bi, i, j: (bi, i, 0))
    prev = pl.BlockSpec((1, HALO, d), lambda bi, i, j: (bi, jnp.maximum(i * tiles_per_halo - 1, 0), 0))
    nxt = pl.BlockSpec((1, HALO, d),
                       lambda bi, i, j: (bi, jnp.minimum((i + 1) * tiles_per_halo, last_halo), 0))
    gate_cols = lambda rows: pl.BlockSpec((rows, FF_BLOCK), lambda bi, i, j: (0, j))
    val_cols = lambda rows: pl.BlockSpec((rows, FF_BLOCK), lambda bi, i, j: (0, nj + j))
    return pl.pallas_call(
        _ffn_kernel,
        grid=(b, s // FFN_ROW_TILE, nj),
        in_specs=[row, prev, nxt, pl.BlockSpec((1, d), lambda bi, i, j: (0, 0)),
                  gate_cols(d), val_cols(d), gate_cols(CONV_WIDTH), val_cols(CONV_WIDTH),
                  gate_cols(1), val_cols(1),
                  pl.BlockSpec((FF_BLOCK, d), lambda bi, i, j: (j, 0))],
        out_specs=row,
        out_shape=jax.ShapeDtypeStruct((b, s, d), F32),
        scratch_shapes=[pltpu.VMEM((FFN_ROW_TILE + 2 * HALO, d), BF16),
                        pltpu.VMEM((FFN_ROW_TILE + 2 * HALO, FF_BLOCK), F32),
                        pltpu.VMEM((FFN_ROW_TILE + 2 * HALO, FF_BLOCK), F32),
                        pltpu.VMEM((FFN_ROW_TILE, d), F32)],
        compiler_params=_params("parallel", "parallel", "arbitrary"),
        name="ffn",
    )(x, x, x, g, w_up, w_up, conv_w, conv_w, conv_b, conv_b, w_down)


def _rope_tables(seq):
    n_rows = seq // GRID_W
    row = jnp.repeat(jnp.arange(n_rows, dtype=F32), GRID_W)
    col = jnp.tile(jnp.arange(GRID_W, dtype=F32), n_rows)
    n_freq = HEAD_DIM // 4
    inv_freq = jnp.power(ROPE_THETA, -jnp.arange(n_freq, dtype=F32) / n_freq)
    ang_r, ang_c = row[:, None] * inv_freq, col[:, None] * inv_freq
    cos_h = jnp.concatenate([jnp.cos(ang_r), jnp.cos(ang_r), jnp.cos(ang_c), jnp.cos(ang_c)], axis=1)
    sin_h = jnp.concatenate([-jnp.sin(ang_r), jnp.sin(ang_r), -jnp.sin(ang_c), jnp.sin(ang_c)], axis=1)
    reps = QK_W // HEAD_DIM
    return jnp.tile(cos_h, (1, reps)), jnp.tile(sin_h, (1, reps))


def _block_diag_mean(width):
    group = np.arange(width) // HEAD_DIM
    return jnp.asarray((group[:, None] == group[None, :]).astype(np.float32) / HEAD_DIM, dtype=BF16)


def _tri_constants():
    t = np.arange(REC_BLOCK)
    same_chunk = (t[:, None] // CHUNK) == (t[None, :] // CHUNK)
    prefix = (same_chunk & (t[None, :] <= t[:, None])).astype(np.float32)
    suffix = (same_chunk & (t[None, :] >= t[:, None])).astype(np.float32)
    tbd = jnp.asarray(np.stack([prefix, suffix]), dtype=BF16)
    tbd_t = jnp.asarray(np.stack([prefix.T, suffix.T]), dtype=BF16)
    return tbd, tbd_t


def _mlstm_expanders():
    e_r = np.zeros((2, GATE_W, MLSTM_W), np.float32)
    e_f = np.zeros((2, GATE_W, MLSTM_W), np.float32)
    for d in range(2):
        for h in range(MLSTM_HEADS):
            lanes = slice(h * HEAD_DIM, (h + 1) * HEAD_DIM)
            i_lane = 2 * GLA_RANK + d * MLSTM_HEADS + h
            f_lane = i_lane + 2 * MLSTM_HEADS
            e_r[d, i_lane, lanes] = 1.0
            e_r[d, f_lane, lanes] = -1.0
            e_f[d, f_lane, lanes] = 1.0
    group = np.arange(MLSTM_W) // HEAD_DIM
    e_h = (group[:, None] == group[None, :]).astype(np.float32)
    return jnp.asarray(e_r, dtype=BF16), jnp.asarray(e_f, dtype=BF16), jnp.asarray(e_h, dtype=BF16)


def _layer(x, consts, norm_mix_g, w_in, attn_qn_g, attn_kn_g, gla_gate_w2, gla_gate_b, gla_norm_g,
           mlstm_gate_b, mlstm_norm_g, w_out, norm_ffn_g, w_up, conv_w, conv_b, w_down):
    cos_t, sin_t, bd_qk, bd_head, tbd, tbd_t, e_r, e_f, e_h = consts
    b, s, _ = x.shape
    n_gates = 4 * MLSTM_HEADS
    main_w = QKV_W + 4 * GLA_W
    gla_a = w_in[:, main_w:main_w + 2 * GLA_RANK]
    mls_main = w_in[:, main_w + 2 * GLA_RANK:main_w + 2 * GLA_RANK + 4 * MLSTM_W]
    mls_g = w_in[:, main_w + 2 * GLA_RANK + 4 * MLSTM_W:]
    pad = jnp.zeros((D_MODEL, GATE_W - 2 * GLA_RANK - n_gates), F32)
    w_in_r = jnp.concatenate([w_in[:, :main_w], mls_main, gla_a, mls_g, pad], axis=1).astype(BF16)

    qkv, gla_proj, mls_proj, gate_proj = _inproj(x, norm_mix_g[None, :], w_in_r)

    qk_gain = jnp.concatenate([jnp.tile(attn_qn_g, ATTN_HEADS), jnp.tile(attn_kn_g, ATTN_KV_HEADS)])[None, :]
    q_t, k, v_t = _qkprep(qkv, cos_t, sin_t, qk_gain, bd_qk)
    score_bound = (HEAD_DIM * Q_SCALE * LOG2_E) * jnp.max(jnp.abs(attn_qn_g)) * jnp.max(jnp.abs(attn_kn_g))
    bounded = (score_bound < ATTN_MAX_SCORE_BOUND).astype(jnp.int32).reshape(1)
    attn = _attention(bounded, q_t, k, v_t)

    w2ext = jnp.zeros((2, GATE_W, GLA_W), F32)
    w2ext = w2ext.at[0, :GLA_RANK].set(gla_gate_w2[0]).at[1, GLA_RANK:2 * GLA_RANK].set(gla_gate_w2[1])
    w2_hi = w2ext.astype(BF16)
    w2_lo = (w2ext - w2_hi.astype(F32)).astype(BF16)
    cum_f, cum_b = _glagate(gate_proj, w2_hi, w2_lo, gla_gate_b[:, None, :], tbd)
    gof, gob = _gla(gla_proj, cum_f, cum_b, e_h)

    nb, cpb = s // REC_BLOCK, REC_BLOCK // CHUNK
    f_lo = 2 * GLA_RANK + 2 * MLSTM_HEADS
    f_rows = gate_proj[:, :, f_lo:f_lo + 2 * MLSTM_HEADS].reshape(b, nb, cpb, CHUNK, 2, MLSTM_HEADS)
    f_rows = f_rows.transpose(0, 4, 1, 2, 5, 3).reshape(b, 2, nb, cpb, MLSTM_W)
    f_rows = jnp.pad(f_rows, ((0, 0), (0, 0), (0, 0), (0, SUBLANE - cpb), (0, 0)))
    bcol = jnp.zeros((1, GATE_W), F32).at[0, 2 * GLA_RANK:2 * GLA_RANK + n_gates].set(
        mlstm_gate_b.reshape(n_gates))
    brow = jnp.repeat(mlstm_gate_b[2:], HEAD_DIM, axis=1)[:, None, :]
    mhf, mhb = _mlstm(mls_proj, gate_proj, f_rows, bcol, brow, tbd, tbd_t, e_r, e_f, e_h)

    x = _outproj(x, attn, gof, gob, gla_proj, mhf, mhb, mls_proj,
                 jnp.tile(gla_norm_g, GLA_HEADS)[None, :], jnp.tile(mlstm_norm_g, MLSTM_HEADS)[None, :],
                 bd_head, w_out.astype(BF16))
    return _ffn(x, norm_ffn_g[None, :], w_up.astype(BF16), conv_w, conv_b[None, :], w_down.astype(BF16))


def _trunk(x, weights):
    seq = x.shape[1]
    cos_t, sin_t = _rope_tables(seq)
    consts = ((cos_t, sin_t, _block_diag_mean(QK_W), _block_diag_mean(GLA_W))
              + _tri_constants() + _mlstm_expanders())
    depth = weights[0].shape[0]
    for l in range(depth):
        x = _layer(x, consts, *[w[l] for w in weights])
    return x


def kernel(x_prompt, x_sample, norm_mix_g, w_in, attn_qn_g, attn_kn_g, gla_gate_w2, gla_gate_b,
           gla_norm_g, mlstm_gate_b, mlstm_norm_g, w_out, norm_ffn_g, w_up, conv_w, conv_b, w_down):
    weights = (norm_mix_g, w_in, attn_qn_g, attn_kn_g, gla_gate_w2, gla_gate_b, gla_norm_g,
               mlstm_gate_b, mlstm_norm_g, w_out, norm_ffn_g, w_up, conv_w, conv_b, w_down)
    return _trunk(x_prompt, weights), _trunk(x_sample, weights)
```
